```python
import math, functools
import jax, jax.numpy as jnp
from jax import lax
import numpy as np

D_MODEL = 1024
BATCH = 4
SEQ = 8192
DEPTH = 1
DEC_BATCH = 2
DEC_SEQ = 16384
PAST_LEN = 128

HG_HEADS = 4
HG_HEAD_DIM = 128
HG_WIDTH = HG_HEADS * HG_HEAD_DIM
HG_CHUNK = 64
MLA_HEADS = 8
Q_LORA = 256
KV_LORA = 256
QK_NOPE = 64
QK_ROPE = 32
V_HEAD = 64
QK_HEAD = QK_NOPE + QK_ROPE
MLA_WIDTH = MLA_HEADS * V_HEAD
Q_BLOCK = 128
ROPE_THETA = 10000.0
N_BRANCH = 2
BRANCH_WIDTH = 512
D_FF = 2816
EPS = 1e-6
SPLIT_SIZES = (HG_WIDTH, HG_WIDTH, HG_WIDTH, HG_WIDTH, HG_WIDTH,
               Q_LORA, KV_LORA, QK_ROPE,
               N_BRANCH * D_MODEL)
IN_COLS = sum(SPLIT_SIZES)

kernel_name = "hgrn2_mla_parallel_encoder"


def _rmsnorm(x, g):
    xf = x.astype(jnp.float32)
    y = xf * lax.rsqrt(jnp.mean(xf * xf, axis=-1, keepdims=True) + EPS)
    return (y * g.astype(jnp.float32)).astype(x.dtype)


def _rope(x):
    S, d = x.shape[1], x.shape[-1]
    inv = ROPE_THETA ** (-jnp.arange(0, d, 2, dtype=jnp.float32) / d)
    ang = jnp.arange(S, dtype=jnp.float32)[:, None] * inv[None, :]
    cos = jnp.cos(ang)[None, :, None, :]
    sin = jnp.sin(ang)[None, :, None, :]
    xf = x.astype(jnp.float32)
    x1, x2 = xf[..., : d // 2], xf[..., d // 2:]
    out = jnp.concatenate([x1 * cos - x2 * sin, x2 * cos + x1 * sin], axis=-1)
    return out.astype(x.dtype)


def _hgrn2_scan(q, v, g_log):
    B, S, H, D = q.shape
    C = HG_CHUNK
    nc = S // C
    k = -jnp.expm1(g_log)

    def to_chunks(t):
        return t.reshape(B, nc, C, H, D).transpose(1, 0, 3, 2, 4)

    qc, kc, vc, gc = to_chunks(q), to_chunks(k), to_chunks(v), to_chunks(g_log)
    b = jnp.cumsum(gc, axis=-2)
    b_last = b[..., -1:, :]
    ref = b[..., C // 2 - 1: C // 2, :]
    q_in = qc * jnp.exp(b - ref)
    k_in = kc * jnp.exp(ref - b)
    att = jnp.einsum('nbhtd,nbhsd->nbhts', q_in, k_in)
    mask = jnp.tril(jnp.ones((C, C), dtype=bool))
    att = jnp.where(mask, att, 0.0)
    o_intra = jnp.einsum('nbhts,nbhse->nbhte', att, vc)
    q_inter = qc * jnp.exp(b)
    k_state = kc * jnp.exp(b_last - b)
    decay = jnp.exp(b_last[..., 0, :])

    def step(state, xs):
        qi, ks, vs, dec = xs
        o = jnp.einsum('bhtd,bhde->bhte', qi, state)
        state = state * dec[..., None] + jnp.einsum('bhsd,bhse->bhde', ks, vs)
        return state, o

    s0 = jnp.zeros((B, H, D, D), jnp.float32)
    _, o_inter = lax.scan(step, s0, (q_inter, k_state, vc, decay))
    o = o_intra + o_inter
    return o.transpose(1, 0, 3, 2, 4).reshape(B, S, H, D)


def _mla_attention(q, k, v):
    B, S, H, _ = q.shape
    nq = S // Q_BLOCK
    qb = q.reshape(B, nq, Q_BLOCK, H, QK_HEAD).transpose(1, 0, 2, 3, 4)
    scale = QK_HEAD ** -0.5

    def block(qi):
        s = jnp.einsum('bqhd,bkhd->bhqk', qi, k).astype(jnp.float32) * scale
        p = jax.nn.softmax(s, axis=-1).astype(v.dtype)
        return jnp.einsum('bhqk,bkhe->bqhe', p, v)

    o = lax.map(block, qb)
    return o.transpose(1, 0, 2, 3, 4).reshape(B, S, H, V_HEAD)


def _trunk(x, g_mix, w_in, lb_param, g_onorm, g_qa, w_uq, g_kva, w_ukv,
           w_branch, w_out, g_ffn, w_gate_up, w_down, g_final):
    B, S, _ = x.shape
    f32 = jnp.float32
    lower_bounds = jnp.cumsum(jax.nn.softmax(lb_param.astype(f32), axis=0), axis=0)
    split_pts = [int(v) for v in np.cumsum(SPLIT_SIZES)[:-1]]
    for l in range(DEPTH):
        h = _rmsnorm(x, g_mix[l])
        proj = h @ w_in[l]
        p_q, p_i, p_ff, p_fb, p_g, p_qa, p_kva, p_kr, p_gates = jnp.split(proj, split_pts, axis=-1)

        hd = (B, S, HG_HEADS, HG_HEAD_DIM)
        hq = jax.nn.silu(p_q.astype(f32)).reshape(hd)
        hi = p_i.astype(f32).reshape(hd)
        lb = lower_bounds[l]
        g_fw = jnp.log(lb[0] + (1.0 - lb[0]) * jax.nn.sigmoid(p_ff.astype(f32))).reshape(hd)
        g_bw = jnp.log(lb[1] + (1.0 - lb[1]) * jax.nn.sigmoid(p_fb.astype(f32))).reshape(hd)
        o_fw = _hgrn2_scan(hq, hi, g_fw)
        o_bw = jnp.flip(_hgrn2_scan(jnp.flip(hq, 1), jnp.flip(hi, 1), jnp.flip(g_bw, 1)), 1)
        o_h = _rmsnorm(o_fw + o_bw, g_onorm[l].reshape(HG_HEADS, HG_HEAD_DIM))
        o_h = (o_h * jax.nn.silu(p_g.astype(f32)).reshape(hd)).reshape(B, S, HG_WIDTH).astype(x.dtype)

        cq = _rmsnorm(p_qa, g_qa[l])
        q = jnp.einsum('bsr,rhd->bshd', cq, w_uq[l])
        ckv = _rmsnorm(p_kva, g_kva[l])
        kv = jnp.einsum('bsr,rhd->bshd', ckv, w_ukv[l])
        k_nope, v = kv[..., :QK_NOPE], kv[..., QK_NOPE:]
        k_pe = _rope(p_kr[:, :, None, :])
        q_full = jnp.concatenate([q[..., :QK_NOPE], _rope(q[..., QK_NOPE:])], axis=-1)
        k_full = jnp.concatenate(
            [k_nope, jnp.broadcast_to(k_pe, (B, S, MLA_HEADS, QK_ROPE))], axis=-1)
        o_m = _mla_attention(q_full, k_full, v).reshape(B, S, MLA_WIDTH)

        gates = jax.nn.sigmoid(p_gates.astype(f32)).reshape(B, S, N_BRANCH, D_MODEL).astype(x.dtype)
        branches = jnp.stack([o_h, o_m], axis=2)
        bproj = jnp.einsum('bsnk,nkd->bsnd', branches, w_branch[l])
        merged = jnp.sum(gates * bproj, axis=2)
        x = x + merged @ w_out[l]

        h2 = _rmsnorm(x, g_ffn[l])
        gu = h2 @ w_gate_up[l]
        gate, up = gu[..., :D_FF], gu[..., D_FF:]
        x = x + (jax.nn.silu(gate) * up) @ w_down[l]
    return _rmsnorm(x, g_final)


def setup_inputs(seed: int = 0) -> dict:
    key = jax.random.key(seed)
    ks = jax.random.split(key, 20)
    nrm = lambda k, shape, fan: jax.random.normal(k, shape, jnp.float32) * (fan ** -0.5)
    gain = lambda k, shape: 1.0 + 0.02 * jax.random.normal(k, shape, jnp.float32)
    return {
        "x_prompt": jax.random.normal(ks[0], (BATCH, SEQ, D_MODEL), jnp.float32),
        "x_sample": jax.random.normal(ks[1], (DEC_BATCH, DEC_SEQ, D_MODEL), jnp.float32),
        "g_mix": gain(ks[2], (DEPTH, D_MODEL)),
        "w_in": nrm(ks[3], (DEPTH, D_MODEL, IN_COLS), D_MODEL),
        "lb_param": 0.1 * jax.random.normal(ks[4], (DEPTH + 1, 2, HG_WIDTH), jnp.float32),
        "g_onorm": gain(ks[5], (DEPTH, HG_WIDTH)),
        "g_qa": gain(ks[6], (DEPTH, Q_LORA)),
        "w_uq": nrm(ks[7], (DEPTH, Q_LORA, MLA_HEADS, QK_HEAD), Q_LORA),
        "g_kva": gain(ks[8], (DEPTH, KV_LORA)),
        "w_ukv": nrm(ks[9], (DEPTH, KV_LORA, MLA_HEADS, QK_NOPE + V_HEAD), KV_LORA),
        "w_branch": nrm(ks[10], (DEPTH, N_BRANCH, BRANCH_WIDTH, D_MODEL), BRANCH_WIDTH),
        "w_out": nrm(ks[11], (DEPTH, D_MODEL, D_MODEL), D_MODEL),
        "g_ffn": gain(ks[12], (DEPTH, D_MODEL)),
        "w_gate_up": nrm(ks[13], (DEPTH, D_MODEL, 2 * D_FF), D_MODEL),
        "w_down": nrm(ks[14], (DEPTH, D_FF, D_MODEL), D_FF),
        "g_final": gain(ks[15], (D_MODEL,)),
    }


def reference(x_prompt, x_sample, g_mix, w_in, lb_param, g_onorm, g_qa, w_uq, g_kva, w_ukv,
              w_branch, w_out, g_ffn, w_gate_up, w_down, g_final):
    y_prompt = _trunk(x_prompt, g_mix, w_in, lb_param, g_onorm, g_qa, w_uq, g_kva, w_ukv,
                      w_branch, w_out, g_ffn, w_gate_up, w_down, g_final)
    y_sample = _trunk(x_sample, g_mix, w_in, lb_param, g_onorm, g_qa, w_uq, g_kva, w_ukv,
                      w_branch, w_out, g_ffn, w_gate_up, w_down, g_final)
    return (y_prompt, y_sample)
```

```python
import functools

import jax
import jax.numpy as jnp
import numpy as np
from jax import lax
from jax.experimental import pallas as pl
from jax.experimental.pallas import tpu as pltpu

F32 = jnp.float32
BF16 = jnp.bfloat16

EPS = 1e-6
ROPE_THETA = 10000.0
HG_HEADS = 4
HG_HEAD_DIM = 128
HG_WIDTH = HG_HEADS * HG_HEAD_DIM
HG_CHUNK = 64
QK_NOPE = 64
QK_ROPE = 32
V_HEAD = 64
QK_HEAD = QK_NOPE + QK_ROPE
QK_PAD = 128
VMEM_LIMIT_BYTES = 56 * 1024 * 1024

NT_DIMS = (((1,), (1,)), ((), ()))
TN_DIMS = (((0,), (0,)), ((), ()))


def _sigmoid(x):
    return 1.0 / (1.0 + jnp.exp(-x))


def _rms(x, g):
    ms = jnp.mean(x * x, axis=-1, keepdims=True)
    return x * lax.rsqrt(ms + EPS) * g


def _params(*sem):
    return pltpu.CompilerParams(dimension_semantics=sem, vmem_limit_bytes=VMEM_LIMIT_BYTES)


def _inproj_kernel(x_ref, g_ref, whg_ref, wmla_ref, wgt_ref, hg_ref, mla_ref, gt_ref):
    h = _rms(x_ref[...], g_ref[...]).astype(BF16)
    hg_ref[...] = jnp.dot(h, whg_ref[...], preferred_element_type=F32)
    mla_ref[...] = jnp.dot(h, wmla_ref[...], preferred_element_type=F32)
    gt_ref[...] = jnp.dot(h, wgt_ref[...], preferred_element_type=F32)


def _in_proj(x2, g_mix, w_hg, w_mla, w_gt, tm):
    T, D = x2.shape
    const = lambda i: (0, 0)
    row = lambda i: (i, 0)
    return pl.pallas_call(
        _inproj_kernel,
        grid=(T // tm,),
        in_specs=[
            pl.BlockSpec((tm, D), row),
            pl.BlockSpec((1, D), const),
            pl.BlockSpec(w_hg.shape, const),
            pl.BlockSpec(w_mla.shape, const),
            pl.BlockSpec(w_gt.shape, const),
        ],
        out_specs=[
            pl.BlockSpec((tm, w_hg.shape[1]), row),
            pl.BlockSpec((tm, w_mla.shape[1]), row),
            pl.BlockSpec((tm, w_gt.shape[1]), row),
        ],
        out_shape=[
            jax.ShapeDtypeStruct((T, w_hg.shape[1]), F32),
            jax.ShapeDtypeStruct((T, w_mla.shape[1]), F32),
            jax.ShapeDtypeStruct((T, w_gt.shape[1]), F32),
        ],
        compiler_params=_params("parallel"),
        name="in_proj",
    )(x2, g_mix, w_hg, w_mla, w_gt)


def _hgrn_head(pq, pv, pf, lb, st_t, tri, reverse):
    C = pq.shape[0]
    f = lb + (1.0 - lb) * _sigmoid(pf)
    g = jnp.log(f)
    kk = 1.0 - f
    g_hi = g.astype(BF16)
    g_lo = (g - g_hi.astype(F32)).astype(BF16)
    tri_b = tri.astype(BF16)
    b = (jnp.dot(tri_b, g_hi, preferred_element_type=F32)
         + jnp.dot(tri_b, g_lo, preferred_element_type=F32))
    mid = C // 2 if reverse else C // 2 - 1
    last = 0 if reverse else C - 1
    ref = b[mid:mid + 1, :]
    b_last = b[last:last + 1, :]
    q = pq * _sigmoid(pq)
    q_in = q * jnp.exp(b - ref)
    k_in = kk * jnp.exp(ref - b)
    att = lax.dot_general(q_in.astype(BF16), k_in.astype(BF16), NT_DIMS,
                          preferred_element_type=F32)
    att = jnp.where(tri, att, 0.0)
    q_inter = q_in * jnp.exp(ref)
    k_state = k_in * jnp.exp(b_last - ref)
    v_b = pv.astype(BF16)
    o = (jnp.dot(att.astype(BF16), v_b, preferred_element_type=F32)
         + lax.dot_general(q_inter.astype(BF16), st_t.astype(BF16), NT_DIMS,
                           preferred_element_type=F32))
    st_new = st_t * jnp.exp(b_last) + lax.dot_general(
        v_b, k_state.astype(BF16), TN_DIMS, preferred_element_type=F32)
    return o, st_new


def _hgrn_kernel(*refs, reverse, chunk):
    if reverse:
        q_ref, v_ref, f_ref, lb_ref, og_ref, ofw_ref, gon_ref, o_ref, st_ref = refs
    else:
        q_ref, v_ref, f_ref, lb_ref, o_ref, st_ref = refs
    ts = q_ref.shape[1]
    nchunks = ts // chunk

    @pl.when(pl.program_id(1) == 0)
    def _():
        st_ref[...] = jnp.zeros_like(st_ref)

    rows = lax.broadcasted_iota(jnp.int32, (chunk, chunk), 0)
    cols = lax.broadcasted_iota(jnp.int32, (chunk, chunk), 1)
    tri = (cols >= rows) if reverse else (cols <= rows)

    def body(ci, carry):
        c = (nchunks - 1 - ci) if reverse else ci
        r0 = pl.multiple_of(c * chunk, chunk)
        rs = pl.ds(r0, chunk)
        for h in range(HG_HEADS):
            ls = slice(h * HG_HEAD_DIM, (h + 1) * HG_HEAD_DIM)
            o, st_new = _hgrn_head(q_ref[0, rs, ls], v_ref[0, rs, ls], f_ref[0, rs, ls],
                                   lb_ref[:, ls], st_ref[h], tri, reverse)
            st_ref[h] = st_new
            if reverse:
                o = o + ofw_ref[0, rs, ls]
                pg = og_ref[0, rs, ls]
                o = _rms(o, gon_ref[:, ls]) * (pg * _sigmoid(pg))
            o_ref[0, rs, ls] = o.astype(o_ref.dtype)
        return carry

    lax.fori_loop(0, nchunks, body, 0)


def _hgrn_scan(hg, lb_row, o_fw, g_onorm, *, reverse, ts):
    B, S, _ = hg.shape
    ns = S // ts
    W = HG_WIDTH
    seq = (lambda s: ns - 1 - s) if reverse else (lambda s: s)

    def col(j):
        return pl.BlockSpec((1, ts, W), lambda b, s: (b, seq(s), j))

    vec = pl.BlockSpec((1, W), lambda b, s: (0, 0))
    plain = pl.BlockSpec((1, ts, W), lambda b, s: (b, seq(s), 0))
    if reverse:
        in_specs = [col(0), col(1), col(3), vec, col(4), plain, vec]
        args = (hg, hg, hg, lb_row, hg, o_fw, g_onorm)
        out_dtype = BF16
    else:
        in_specs = [col(0), col(1), col(2), vec]
        args = (hg, hg, hg, lb_row)
        out_dtype = F32
    return pl.pallas_call(
        functools.partial(_hgrn_kernel, reverse=reverse, chunk=HG_CHUNK),
        grid=(B, ns),
        in_specs=in_specs,
        out_specs=plain,
        out_shape=jax.ShapeDtypeStruct((B, S, W), out_dtype),
        scratch_shapes=[pltpu.VMEM((HG_HEADS, HG_HEAD_DIM, HG_HEAD_DIM), F32)],
        compiler_params=_params("parallel", "arbitrary"),
        name="hgrn_bwd" if reverse else "hgrn_fwd",
    )(*args)


def _mla_prep_kernel(mla_ref, gqa_ref, gkva_ref, wq_ref, wk_ref, wv_ref,
                     cos_t_ref, sin_t_ref, cosk_ref, sina_ref, sinb_ref,
                     qt_ref, k_ref, vt_ref, *, heads, scale):
    r = gqa_ref.shape[1]
    cq = _rms(mla_ref[0, :, 0:r], gqa_ref[...]).astype(BF16)
    ckv = _rms(mla_ref[0, :, r:2 * r], gkva_ref[...]).astype(BF16)
    kr = mla_ref[0, :, 2 * r:2 * r + QK_PAD]
    half = QK_ROPE // 2
    k_pe = (kr * cosk_ref[...]
            + pltpu.roll(kr, half, 1) * sina_ref[...]
            + pltpu.roll(kr, QK_PAD - half, 1) * sinb_ref[...])
    cos_t = cos_t_ref[...]
    sin_t = sin_t_ref[...]
    for h in range(heads):
        q_t = lax.dot_general(wq_ref[h], cq, NT_DIMS, preferred_element_type=F32)
        x1 = q_t[QK_NOPE:QK_NOPE + half]
        x2 = q_t[QK_NOPE + half:QK_HEAD]
        q_rot = jnp.concatenate(
            [q_t[:QK_NOPE], x1 * cos_t - x2 * sin_t, x2 * cos_t + x1 * sin_t, q_t[QK_HEAD:]],
            axis=0)
        qt_ref[0, h] = (q_rot * scale).astype(qt_ref.dtype)
        k_h = jnp.dot(ckv, wk_ref[h], preferred_element_type=F32) + k_pe
        k_ref[0, h] = k_h.astype(k_ref.dtype)
        v_t = lax.dot_general(wv_ref[h], ckv, NT_DIMS, preferred_element_type=F32)
        vt_ref[0, h] = v_t.astype(vt_ref.dtype)


def _mla_prep(mla, g_qa, g_kva, wq_t, wk, wv_t, tabs, *, ts):
    B, S, Wm = mla.shape
    H = wq_t.shape[0]
    cos_t, sin_t, cosk, sina, sinb = tabs
    half = QK_ROPE // 2
    c2 = lambda b, s: (0, 0)
    c3 = lambda b, s: (0, 0, 0)
    return pl.pallas_call(
        functools.partial(_mla_prep_kernel, heads=H, scale=QK_HEAD ** -0.5),
        grid=(B, S // ts),
        in_specs=[
            pl.BlockSpec((1, ts, Wm), lambda b, s: (b, s, 0)),
            pl.BlockSpec(g_qa.shape, c2),
            pl.BlockSpec(g_kva.shape, c2),
            pl.BlockSpec(wq_t.shape, c3),
            pl.BlockSpec(wk.shape, c3),
            pl.BlockSpec(wv_t.shape, c3),
            pl.BlockSpec((half, ts), lambda b, s: (0, s)),
            pl.BlockSpec((half, ts), lambda b, s: (0, s)),
            pl.BlockSpec((ts, QK_PAD), lambda b, s: (s, 0)),
            pl.BlockSpec((ts, QK_PAD), lambda b, s: (s, 0)),
            pl.BlockSpec((ts, QK_PAD), lambda b, s: (s, 0)),
        ],
        out_specs=[
            pl.BlockSpec((1, H, QK_PAD, ts), lambda b, s: (b, 0, 0, s)),
            pl.BlockSpec((1, H, ts, QK_PAD), lambda b, s: (b, 0, s, 0)),
            pl.BlockSpec((1, H, V_HEAD, ts), lambda b, s: (b, 0, 0, s)),
        ],
        out_shape=[
            jax.ShapeDtypeStruct((B, H, QK_PAD, S), BF16),
            jax.ShapeDtypeStruct((B, H, S, QK_PAD), BF16),
            jax.ShapeDtypeStruct((B, H, V_HEAD, S), BF16),
        ],
        compiler_params=_params("parallel", "parallel"),
        name="mla_prep",
    )(mla, g_qa, g_kva, wq_t, wk, wv_t, cos_t, sin_t, cosk, sina, sinb)


def _attn_kernel(qt_ref, k_ref, vt_ref, o_ref, *, tk):
    q_t = qt_ref[0, 0]
    tq = q_t.shape[1]
    nk = k_ref.shape[2] // tk

    def step(j, carry):
        m, l, acc = carry
        k0 = pl.multiple_of(j * tk, tk)
        s = jnp.dot(k_ref[0, 0, pl.ds(k0, tk), :], q_t, preferred_element_type=F32)
        m_new = jnp.maximum(m, jnp.max(s, axis=0, keepdims=True))
        alpha = jnp.exp(m - m_new)
        p = jnp.exp(s - m_new)
        l_new = alpha * l + jnp.sum(p, axis=0, keepdims=True)
        pv = jnp.dot(vt_ref[0, 0, :, pl.ds(k0, tk)], p.astype(BF16),
                     preferred_element_type=F32)
        return m_new, l_new, alpha * acc + pv

    init = (jnp.full((1, tq), -jnp.inf, F32), jnp.zeros((1, tq), F32),
            jnp.zeros((vt_ref.shape[2], tq), F32))
    _, l, acc = lax.fori_loop(0, nk, step, init)
    o_ref[0, 0] = (acc / l).astype(o_ref.dtype)


def _attention(q_t, k, v_t, *, tq, tk):
    B, H, _, S = q_t.shape
    return pl.pallas_call(
        functools.partial(_attn_kernel, tk=tk),
        grid=(B, H, S // tq),
        in_specs=[
            pl.BlockSpec((1, 1, QK_PAD, tq), lambda b, h, i: (b, h, 0, i)),
            pl.BlockSpec((1, 1, S, QK_PAD), lambda b, h, i: (b, h, 0, 0)),
            pl.BlockSpec((1, 1, V_HEAD, S), lambda b, h, i: (b, h, 0, 0)),
        ],
        out_specs=pl.BlockSpec((1, 1, V_HEAD, tq), lambda b, h, i: (b, h, 0, i)),
        out_shape=jax.ShapeDtypeStruct((B, H, V_HEAD, S), BF16),
        compiler_params=_params("parallel", "parallel", "arbitrary"),
        name="attention",
    )(q_t, k, v_t)


def _merge_kernel(x_ref, oh_ref, omt_ref, gt_ref, wb0_ref, wb1_ref, wo_ref, y_ref):
    D = x_ref.shape[2]
    bp0 = jnp.dot(oh_ref[0], wb0_ref[...], preferred_element_type=F32)
    bp1 = lax.dot_general(omt_ref[0], wb1_ref[...], TN_DIMS, preferred_element_type=F32)
    gt = gt_ref[0]
    merged = _sigmoid(gt[:, :D]) * bp0 + _sigmoid(gt[:, D:]) * bp1
    y_ref[0] = x_ref[0] + jnp.dot(merged.astype(BF16), wo_ref[...], preferred_element_type=F32)


def _merge(x, o_h, o_mt, gt, wb0, wb1, wo, *, tm):
    B, S, D = x.shape
    c2 = lambda b, s: (0, 0)
    tok = lambda w: pl.BlockSpec((1, tm, w), lambda b, s: (b, s, 0))
    return pl.pallas_call(
        _merge_kernel,
        grid=(B, S // tm),
        in_specs=[
            tok(D), tok(o_h.shape[2]),
            pl.BlockSpec((1, o_mt.shape[1], tm), lambda b, s: (b, 0, s)),
            tok(gt.shape[2]),
            pl.BlockSpec(wb0.shape, c2), pl.BlockSpec(wb1.shape, c2), pl.BlockSpec(wo.shape, c2),
        ],
        out_specs=tok(D),
        out_shape=jax.ShapeDtypeStruct((B, S, D), F32),
        compiler_params=_params("parallel", "parallel"),
        name="merge",
    )(x, o_h, o_mt, gt, wb0, wb1, wo)


def _ffn_kernel(x_ref, gf_ref, wgu_ref, wd_ref, gfin_ref, y_ref):
    x = x_ref[...]
    dff = wd_ref.shape[0]
    h2 = _rms(x, gf_ref[...]).astype(BF16)
    gu = jnp.dot(h2, wgu_ref[...], preferred_element_type=F32)
    gate = gu[:, :dff]
    act = (gate * _sigmoid(gate)) * gu[:, dff:]
    x2 = x + jnp.dot(act.astype(BF16), wd_ref[...], preferred_element_type=F32)
    y_ref[...] = _rms(x2, gfin_ref[...])


def _ffn(x2d, g_ffn, wgu, wd, g_final, *, tm):
    T, D = x2d.shape
    const = lambda i: (0, 0)
    row = lambda i: (i, 0)
    return pl.pallas_call(
        _ffn_kernel,
        grid=(T // tm,),
        in_specs=[
            pl.BlockSpec((tm, D), row),
            pl.BlockSpec((1, D), const),
            pl.BlockSpec(wgu.shape, const),
            pl.BlockSpec(wd.shape, const),
            pl.BlockSpec((1, D), const),
        ],
        out_specs=pl.BlockSpec((tm, D), row),
        out_shape=jax.ShapeDtypeStruct((T, D), F32),
        compiler_params=_params("parallel"),
        name="ffn",
    )(x2d, g_ffn, wgu, wd, g_final)


def _rope_tables(S):
    d = QK_ROPE
    half = d // 2
    inv = ROPE_THETA ** (-jnp.arange(0, d, 2, dtype=F32) / d)
    ang = jnp.arange(S, dtype=F32)[:, None] * inv[None, :]
    cos, sin = jnp.cos(ang), jnp.sin(ang)
    z = lambda n: jnp.zeros((S, n), F32)
    tail = QK_PAD - QK_HEAD
    cosk = jnp.concatenate([z(QK_NOPE), cos, cos, z(tail)], axis=1)
    sina = jnp.concatenate([z(QK_NOPE + half), sin, z(tail)], axis=1)
    sinb = jnp.concatenate([z(QK_NOPE), -sin, z(half + tail)], axis=1)
    return cos.T, sin.T, cosk, sina, sinb


def _prep_weights(w_in, lb_param, w_uq, w_ukv, w_branch, w_out, w_gate_up, w_down):
    W = HG_WIDTH
    r_q, H = w_uq.shape[1], w_uq.shape[2]
    r_kv = w_ukv.shape[1]
    w = w_in[0]
    D = w.shape[0]
    c0 = 5 * W
    w_hg = w[:, :c0]
    w_qa = w[:, c0:c0 + r_q]
    w_kva = w[:, c0 + r_q:c0 + r_q + r_kv]
    c1 = c0 + r_q + r_kv
    w_kr = w[:, c1:c1 + QK_ROPE]
    w_gt = w[:, c1 + QK_ROPE:]
    zpad = lambda n: jnp.zeros((D, n), w.dtype)
    w_mla = jnp.concatenate([w_qa, w_kva, zpad(QK_NOPE), w_kr, zpad(QK_PAD - QK_HEAD)], axis=1)
    wq = jnp.transpose(w_uq[0], (1, 2, 0))
    wq_t = jnp.concatenate([wq, jnp.zeros((H, QK_PAD - QK_HEAD, r_q), wq.dtype)], axis=1)
    wkv = jnp.transpose(w_ukv[0], (1, 0, 2))
    wk = jnp.concatenate([wkv[:, :, :QK_NOPE],
                          jnp.zeros((H, r_kv, QK_PAD - QK_NOPE), wkv.dtype)], axis=2)
    wv_t = jnp.transpose(wkv[:, :, QK_NOPE:], (0, 2, 1))
    lb = jax.nn.softmax(lb_param.astype(F32), axis=0)[0]
    b16 = lambda a: a.astype(BF16)
    return dict(
        w_hg=b16(w_hg), w_mla=b16(w_mla), w_gt=b16(w_gt),
        wq_t=b16(wq_t), wk=b16(wk), wv_t=b16(wv_t),
        wb0=b16(w_branch[0, 0]), wb1=b16(w_branch[0, 1]), wo=b16(w_out[0]),
        wgu=b16(w_gate_up[0]), wd=b16(w_down[0]),
        lb_fw=lb[0:1], lb_bw=lb[1:2],
    )


def _pick(n, pref):
    t = min(n, pref)
    while n % t:
        t //= 2
    return t


def _trunk(x, p, g_mix, g_onorm, g_qa, g_kva, g_ffn, g_final):
    B, S, D = x.shape
    T = B * S
    tm = _pick(T, 256)
    hg, mla, gt = _in_proj(x.reshape(T, D), g_mix, p["w_hg"], p["w_mla"], p["w_gt"], tm)
    hg = hg.reshape(B, S, -1)
    mla = mla.reshape(B, S, -1)
    gt = gt.reshape(B, S, -1)

    ts = _pick(S, 512)
    o_fw = _hgrn_scan(hg, p["lb_fw"], None, None, reverse=False, ts=ts)
    o_h = _hgrn_scan(hg, p["lb_bw"], o_fw, g_onorm, reverse=True, ts=ts)

    tabs = _rope_tables(S)
    q_t, k, v_t = _mla_prep(mla, g_qa, g_kva, p["wq_t"], p["wk"], p["wv_t"], tabs,
                            ts=_pick(S, 512))
    o_mt = _attention(q_t, k, v_t, tq=_pick(S, 256), tk=_pick(S, 256))
    o_mt = o_mt.reshape(B, -1, S)

    x1 = _merge(x, o_h, o_mt, gt, p["wb0"], p["wb1"], p["wo"], tm=_pick(S, 256))
    y = _ffn(x1.reshape(T, D), g_ffn, p["wgu"], p["wd"], g_final[None, :], tm=tm)
    return y.reshape(B, S, D)


def kernel(x_prompt, x_sample, g_mix, w_in, lb_param, g_onorm, g_qa, w_uq, g_kva, w_ukv,
           w_branch, w_out, g_ffn, w_gate_up, w_down, g_final):
    p = _prep_weights(w_in, lb_param, w_uq, w_ukv, w_branch, w_out, w_gate_up, w_down)
    args = (p, g_mix, g_onorm, g_qa, g_kva, g_ffn, g_final)
    return (_trunk(x_prompt, *args), _trunk(x_sample, *args))
```

```python
import functools

import jax
import jax.numpy as jnp
import numpy as np
from jax import lax
from jax.experimental import pallas as pl
from jax.experimental.pallas import tpu as pltpu

F32 = jnp.float32
BF16 = jnp.bfloat16

EPS = 1e-6
ROPE_THETA = 10000.0
HG_HEADS = 4
HG_HEAD_DIM = 128
HG_WIDTH = HG_HEADS * HG_HEAD_DIM
HG_CHUNK = 64
QK_NOPE = 64
QK_ROPE = 32
V_HEAD = 64
QK_HEAD = QK_NOPE + QK_ROPE
QK_PAD = 128
V_AUG = V_HEAD + 16
LOG2E = 1.4426950408889634
VMEM_LIMIT_BYTES = 56 * 1024 * 1024

NT_DIMS = (((1,), (1,)), ((), ()))
TN_DIMS = (((0,), (0,)), ((), ()))


def _sigmoid(x):
    return 1.0 / (1.0 + jnp.exp(-x))


def _rms(x, g):
    ms = jnp.mean(x * x, axis=-1, keepdims=True)
    return x * lax.rsqrt(ms + EPS) * g


def _params(*sem):
    return pltpu.CompilerParams(dimension_semantics=sem, vmem_limit_bytes=VMEM_LIMIT_BYTES)


def _inproj_kernel(x_ref, g_ref, whg_ref, wmla_ref, wgt_ref, hg_ref, mla_ref, gt_ref):
    h = _rms(x_ref[...], g_ref[...]).astype(BF16)
    hg_ref[...] = jnp.dot(h, whg_ref[...], preferred_element_type=F32)
    mla_ref[...] = jnp.dot(h, wmla_ref[...], preferred_element_type=F32)
    gt_ref[...] = jnp.dot(h, wgt_ref[...], preferred_element_type=F32)


def _in_proj(x2, g_mix, w_hg, w_mla, w_gt, tm):
    T, D = x2.shape
    const = lambda i: (0, 0)
    row = lambda i: (i, 0)
    return pl.pallas_call(
        _inproj_kernel,
        grid=(T // tm,),
        in_specs=[
            pl.BlockSpec((tm, D), row),
            pl.BlockSpec((1, D), const),
            pl.BlockSpec(w_hg.shape, const),
            pl.BlockSpec(w_mla.shape, const),
            pl.BlockSpec(w_gt.shape, const),
        ],
        out_specs=[
            pl.BlockSpec((tm, w_hg.shape[1]), row),
            pl.BlockSpec((tm, w_mla.shape[1]), row),
            pl.BlockSpec((tm, w_gt.shape[1]), row),
        ],
        out_shape=[
            jax.ShapeDtypeStruct((T, w_hg.shape[1]), F32),
            jax.ShapeDtypeStruct((T, w_mla.shape[1]), F32),
            jax.ShapeDtypeStruct((T, w_gt.shape[1]), F32),
        ],
        compiler_params=_params("parallel"),
        name="in_proj",
    )(x2, g_mix, w_hg, w_mla, w_gt)


def _hgrn_head(pq, pv, pf, lb, st_t, tri, reverse):
    C = pq.shape[0]
    f = lb + (1.0 - lb) * _sigmoid(pf)
    g = jnp.log(f)
    kk = 1.0 - f
    g_hi = g.astype(BF16)
    g_lo = (g - g_hi.astype(F32)).astype(BF16)
    tri_b = tri.astype(BF16)
    b = (jnp.dot(tri_b, g_hi, preferred_element_type=F32)
         + jnp.dot(tri_b, g_lo, preferred_element_type=F32))
    mid = C // 2 if reverse else C // 2 - 1
    last = 0 if reverse else C - 1
    ref = b[mid:mid + 1, :]
    b_last = b[last:last + 1, :]
    q = pq * _sigmoid(pq)
    q_in = q * jnp.exp(b - ref)
    k_in = kk * jnp.exp(ref - b)
    att = lax.dot_general(q_in.astype(BF16), k_in.astype(BF16), NT_DIMS,
                          preferred_element_type=F32)
    att = jnp.where(tri, att, 0.0)
    q_inter = q_in * jnp.exp(ref)
    k_state = k_in * jnp.exp(b_last - ref)
    v_b = pv.astype(BF16)
    o = (jnp.dot(att.astype(BF16), v_b, preferred_element_type=F32)
         + lax.dot_general(q_inter.astype(BF16), st_t.astype(BF16), NT_DIMS,
                           preferred_element_type=F32))
    st_new = st_t * jnp.exp(b_last) + lax.dot_general(
        v_b, k_state.astype(BF16), TN_DIMS, preferred_element_type=F32)
    return o, st_new


def _hgrn_kernel(*refs, reverse, chunk):
    if reverse:
        q_ref, v_ref, f_ref, lb_ref, og_ref, ofw_ref, gon_ref, o_ref, st_ref = refs
    else:
        q_ref, v_ref, f_ref, lb_ref, o_ref, st_ref = refs
    ts = q_ref.shape[1]
    nchunks = ts // chunk

    @pl.when(pl.program_id(1) == 0)
    def _():
        st_ref[...] = jnp.zeros_like(st_ref)

    rows = lax.broadcasted_iota(jnp.int32, (chunk, chunk), 0)
    cols = lax.broadcasted_iota(jnp.int32, (chunk, chunk), 1)
    tri = (cols >= rows) if reverse else (cols <= rows)

    def body(ci, carry):
        c = (nchunks - 1 - ci) if reverse else ci
        r0 = pl.multiple_of(c * chunk, chunk)
        rs = pl.ds(r0, chunk)
        for h in range(HG_HEADS):
            ls = slice(h * HG_HEAD_DIM, (h + 1) * HG_HEAD_DIM)
            o, st_new = _hgrn_head(q_ref[0, rs, ls], v_ref[0, rs, ls], f_ref[0, rs, ls],
                                   lb_ref[:, ls], st_ref[h], tri, reverse)
            st_ref[h] = st_new
            if reverse:
                o = o + ofw_ref[0, rs, ls]
                pg = og_ref[0, rs, ls]
                o = _rms(o, gon_ref[:, ls]) * (pg * _sigmoid(pg))
            o_ref[0, rs, ls] = o.astype(o_ref.dtype)
        return carry

    lax.fori_loop(0, nchunks, body, 0)


def _hgrn_scan(hg, lb_row, o_fw, g_onorm, *, reverse, ts):
    B, S, _ = hg.shape
    ns = S // ts
    W = HG_WIDTH
    seq = (lambda s: ns - 1 - s) if reverse else (lambda s: s)

    def col(j):
        return pl.BlockSpec((1, ts, W), lambda b, s: (b, seq(s), j))

    vec = pl.BlockSpec((1, W), lambda b, s: (0, 0))
    plain = pl.BlockSpec((1, ts, W), lambda b, s: (b, seq(s), 0))
    if reverse:
        in_specs = [col(0), col(1), col(3), vec, col(4), plain, vec]
        args = (hg, hg, hg, lb_row, hg, o_fw, g_onorm)
        out_dtype = BF16
    else:
        in_specs = [col(0), col(1), col(2), vec]
        args = (hg, hg, hg, lb_row)
        out_dtype = F32
    return pl.pallas_call(
        functools.partial(_hgrn_kernel, reverse=reverse, chunk=HG_CHUNK),
        grid=(B, ns),
        in_specs=in_specs,
        out_specs=plain,
        out_shape=jax.ShapeDtypeStruct((B, S, W), out_dtype),
        scratch_shapes=[pltpu.VMEM((HG_HEADS, HG_HEAD_DIM, HG_HEAD_DIM), F32)],
        compiler_params=_params("parallel", "arbitrary"),
        name="hgrn_bwd" if reverse else "hgrn_fwd",
    )(*args)


def _mla_prep_kernel(mla_ref, gqa_ref, gkva_ref, wq_ref, wk_ref, wv_ref,
                     cos_t_ref, sin_t_ref, cosk_ref, sina_ref, sinb_ref,
                     qt_ref, k_ref, vt_ref, *, heads, scale):
    r = gqa_ref.shape[1]
    cq = _rms(mla_ref[0, :, 0:r], gqa_ref[...]).astype(BF16)
    ckv = _rms(mla_ref[0, :, r:2 * r], gkva_ref[...]).astype(BF16)
    kr = mla_ref[0, :, 2 * r:2 * r + QK_PAD]
    half = QK_ROPE // 2
    k_pe = (kr * cosk_ref[...]
            + pltpu.roll(kr, half, 1) * sina_ref[...]
            + pltpu.roll(kr, QK_PAD - half, 1) * sinb_ref[...])
    cos_t = cos_t_ref[...]
    sin_t = sin_t_ref[...]
    pad_rows = vt_ref.shape[2] - V_HEAD
    ones_rows = (lax.broadcasted_iota(jnp.int32, (pad_rows, cos_t.shape[1]), 0) == 0).astype(F32)
    for h in range(heads):
        q_t = lax.dot_general(wq_ref[h], cq, NT_DIMS, preferred_element_type=F32)
        x1 = q_t[QK_NOPE:QK_NOPE + half]
        x2 = q_t[QK_NOPE + half:QK_HEAD]
        q_rot = jnp.concatenate(
            [q_t[:QK_NOPE], x1 * cos_t - x2 * sin_t, x2 * cos_t + x1 * sin_t, q_t[QK_HEAD:]],
            axis=0)
        qt_ref[0, h] = (q_rot * scale).astype(qt_ref.dtype)
        k_h = jnp.dot(ckv, wk_ref[h], preferred_element_type=F32) + k_pe
        k_ref[0, h] = k_h.astype(k_ref.dtype)
        v_t = lax.dot_general(wv_ref[h], ckv, NT_DIMS, preferred_element_type=F32)
        vt_ref[0, h] = jnp.concatenate([v_t, ones_rows], axis=0).astype(vt_ref.dtype)


def _mla_prep(mla, g_qa, g_kva, wq_t, wk, wv_t, tabs, *, ts):
    B, S, Wm = mla.shape
    H = wq_t.shape[0]
    cos_t, sin_t, cosk, sina, sinb = tabs
    half = QK_ROPE // 2
    c2 = lambda b, s: (0, 0)
    c3 = lambda b, s: (0, 0, 0)
    return pl.pallas_call(
        functools.partial(_mla_prep_kernel, heads=H, scale=QK_HEAD ** -0.5 * LOG2E),
        grid=(B, S // ts),
        in_specs=[
            pl.BlockSpec((1, ts, Wm), lambda b, s: (b, s, 0)),
            pl.BlockSpec(g_qa.shape, c2),
            pl.BlockSpec(g_kva.shape, c2),
            pl.BlockSpec(wq_t.shape, c3),
            pl.BlockSpec(wk.shape, c3),
            pl.BlockSpec(wv_t.shape, c3),
            pl.BlockSpec((half, ts), lambda b, s: (0, s)),
            pl.BlockSpec((half, ts), lambda b, s: (0, s)),
            pl.BlockSpec((ts, QK_PAD), lambda b, s: (s, 0)),
            pl.BlockSpec((ts, QK_PAD), lambda b, s: (s, 0)),
            pl.BlockSpec((ts, QK_PAD), lambda b, s: (s, 0)),
        ],
        out_specs=[
            pl.BlockSpec((1, H, QK_PAD, ts), lambda b, s: (b, 0, 0, s)),
            pl.BlockSpec((1, H, ts, QK_PAD), lambda b, s: (b, 0, s, 0)),
            pl.BlockSpec((1, H, V_AUG, ts), lambda b, s: (b, 0, 0, s)),
        ],
        out_shape=[
            jax.ShapeDtypeStruct((B, H, QK_PAD, S), BF16),
            jax.ShapeDtypeStruct((B, H, S, QK_PAD), BF16),
            jax.ShapeDtypeStruct((B, H, V_AUG, S), BF16),
        ],
        compiler_params=_params("parallel", "parallel"),
        name="mla_prep",
    )(mla, g_qa, g_kva, wq_t, wk, wv_t, cos_t, sin_t, cosk, sina, sinb)


def _attn_kernel(qt_ref, k_ref, vt_ref, o_ref, s_ref, acc_ref, m_ref, *, tk):
    q_t = qt_ref[0, 0]
    nk = k_ref.shape[2] // tk

    def scores(j, slot):
        k0 = pl.multiple_of(jnp.minimum(j, nk - 1) * tk, tk)
        s_ref[slot] = jnp.dot(k_ref[0, 0, pl.ds(k0, tk), :], q_t, preferred_element_type=F32)

    def softmax_pv(j, slot):
        s = s_ref[slot]
        m_old = m_ref[...]
        m_new = jnp.maximum(m_old, jnp.max(s, axis=0, keepdims=True))
        p = jnp.exp2(s - m_new).astype(BF16)
        k0 = pl.multiple_of(j * tk, tk)
        pv = jnp.dot(vt_ref[0, 0, :, pl.ds(k0, tk)], p, preferred_element_type=F32)
        acc_ref[...] = jnp.exp2(m_old - m_new) * acc_ref[...] + pv
        m_ref[...] = m_new

    m_ref[...] = jnp.full_like(m_ref, -1e30)
    acc_ref[...] = jnp.zeros_like(acc_ref)
    scores(0, 0)

    def pair(i, carry):
        j = 2 * i
        scores(j + 1, 1)
        softmax_pv(j, 0)
        scores(j + 2, 0)
        softmax_pv(j + 1, 1)
        return carry

    lax.fori_loop(0, nk // 2, pair, 0)
    acc = acc_ref[...]
    o_ref[0, 0] = (acc[:V_HEAD] / acc[V_HEAD:V_HEAD + 1]).astype(o_ref.dtype)


def _attention(q_t, k, v_t, *, tq, tk):
    B, H, _, S = q_t.shape
    VA = v_t.shape[2]
    assert (S // tk) % 2 == 0
    return pl.pallas_call(
        functools.partial(_attn_kernel, tk=tk),
        grid=(B, H, S // tq),
        in_specs=[
            pl.BlockSpec((1, 1, QK_PAD, tq), lambda b, h, i: (b, h, 0, i)),
            pl.BlockSpec((1, 1, S, QK_PAD), lambda b, h, i: (b, h, 0, 0)),
            pl.BlockSpec((1, 1, VA, S), lambda b, h, i: (b, h, 0, 0)),
        ],
        out_specs=pl.BlockSpec((1, 1, V_HEAD, tq), lambda b, h, i: (b, h, 0, i)),
        out_shape=jax.ShapeDtypeStruct((B, H, V_HEAD, S), BF16),
        scratch_shapes=[pltpu.VMEM((2, tk, tq), F32), pltpu.VMEM((VA, tq), F32),
                        pltpu.VMEM((1, tq), F32)],
        compiler_params=_params("parallel", "parallel", "arbitrary"),
        name="attention",
    )(q_t, k, v_t)


def _merge_kernel(x_ref, oh_ref, omt_ref, gt_ref, wb0_ref, wb1_ref, wo_ref, y_ref):
    D = x_ref.shape[2]
    bp0 = jnp.dot(oh_ref[0], wb0_ref[...], preferred_element_type=F32)
    bp1 = lax.dot_general(omt_ref[0], wb1_ref[...], TN_DIMS, preferred_element_type=F32)
    gt = gt_ref[0]
    merged = _sigmoid(gt[:, :D]) * bp0 + _sigmoid(gt[:, D:]) * bp1
    y_ref[0] = x_ref[0] + jnp.dot(merged.astype(BF16), wo_ref[...], preferred_element_type=F32)


def _merge(x, o_h, o_mt, gt, wb0, wb1, wo, *, tm):
    B, S, D = x.shape
    c2 = lambda b, s: (0, 0)
    tok = lambda w: pl.BlockSpec((1, tm, w), lambda b, s: (b, s, 0))
    return pl.pallas_call(
        _merge_kernel,
        grid=(B, S // tm),
        in_specs=[
            tok(D), tok(o_h.shape[2]),
            pl.BlockSpec((1, o_mt.shape[1], tm), lambda b, s: (b, 0, s)),
            tok(gt.shape[2]),
            pl.BlockSpec(wb0.shape, c2), pl.BlockSpec(wb1.shape, c2), pl.BlockSpec(wo.shape, c2),
        ],
        out_specs=tok(D),
        out_shape=jax.ShapeDtypeStruct((B, S, D), F32),
        compiler_params=_params("parallel", "parallel"),
        name="merge",
    )(x, o_h, o_mt, gt, wb0, wb1, wo)


def _ffn_kernel(x_ref, gf_ref, wgu_ref, wd_ref, gfin_ref, y_ref):
    x = x_ref[...]
    dff = wd_ref.shape[0]
    h2 = _rms(x, gf_ref[...]).astype(BF16)
    gu = jnp.dot(h2, wgu_ref[...], preferred_element_type=F32)
    gate = gu[:, :dff]
    act = (gate * _sigmoid(gate)) * gu[:, dff:]
    x2 = x + jnp.dot(act.astype(BF16), wd_ref[...], preferred_element_type=F32)
    y_ref[...] = _rms(x2, gfin_ref[...])


def _ffn(x2d, g_ffn, wgu, wd, g_final, *, tm):
    T, D = x2d.shape
    const = lambda i: (0, 0)
    row = lambda i: (i, 0)
    return pl.pallas_call(
        _ffn_kernel,
        grid=(T // tm,),
        in_specs=[
            pl.BlockSpec((tm, D), row),
            pl.BlockSpec((1, D), const),
            pl.BlockSpec(wgu.shape, const),
            pl.BlockSpec(wd.shape, const),
            pl.BlockSpec((1, D), const),
        ],
        out_specs=pl.BlockSpec((tm, D), row),
        out_shape=jax.ShapeDtypeStruct((T, D), F32),
        compiler_params=_params("parallel"),
        name="ffn",
    )(x2d, g_ffn, wgu, wd, g_final)


def _rope_tables(S):
    d = QK_ROPE
    half = d // 2
    inv = ROPE_THETA ** (-jnp.arange(0, d, 2, dtype=F32) / d)
    ang = jnp.arange(S, dtype=F32)[:, None] * inv[None, :]
    cos, sin = jnp.cos(ang), jnp.sin(ang)
    z = lambda n: jnp.zeros((S, n), F32)
    tail = QK_PAD - QK_HEAD
    cosk = jnp.concatenate([z(QK_NOPE), cos, cos, z(tail)], axis=1)
    sina = jnp.concatenate([z(QK_NOPE + half), sin, z(tail)], axis=1)
    sinb = jnp.concatenate([z(QK_NOPE), -sin, z(half + tail)], axis=1)
    return cos.T, sin.T, cosk, sina, sinb


def _prep_weights(w_in, lb_param, w_uq, w_ukv, w_branch, w_out, w_gate_up, w_down):
    W = HG_WIDTH
    r_q, H = w_uq.shape[1], w_uq.shape[2]
    r_kv = w_ukv.shape[1]
    w = w_in[0]
    D = w.shape[0]
    c0 = 5 * W
    w_hg = w[:, :c0]
    w_qa = w[:, c0:c0 + r_q]
    w_kva = w[:, c0 + r_q:c0 + r_q + r_kv]
    c1 = c0 + r_q + r_kv
    w_kr = w[:, c1:c1 + QK_ROPE]
    w_gt = w[:, c1 + QK_ROPE:]
    zpad = lambda n: jnp.zeros((D, n), w.dtype)
    w_mla = jnp.concatenate([w_qa, w_kva, zpad(QK_NOPE), w_kr, zpad(QK_PAD - QK_HEAD)], axis=1)
    wq = jnp.transpose(w_uq[0], (1, 2, 0))
    wq_t = jnp.concatenate([wq, jnp.zeros((H, QK_PAD - QK_HEAD, r_q), wq.dtype)], axis=1)
    wkv = jnp.transpose(w_ukv[0], (1, 0, 2))
    wk = jnp.concatenate([wkv[:, :, :QK_NOPE],
                          jnp.zeros((H, r_kv, QK_PAD - QK_NOPE), wkv.dtype)], axis=2)
    wv_t = jnp.transpose(wkv[:, :, QK_NOPE:], (0, 2, 1))
    lb = jax.nn.softmax(lb_param.astype(F32), axis=0)[0]
    b16 = lambda a: a.astype(BF16)
    return dict(
        w_hg=b16(w_hg), w_mla=b16(w_mla), w_gt=b16(w_gt),
        wq_t=b16(wq_t), wk=b16(wk), wv_t=b16(wv_t),
        wb0=b16(w_branch[0, 0]), wb1=b16(w_branch[0, 1]), wo=b16(w_out[0]),
        wgu=b16(w_gate_up[0]), wd=b16(w_down[0]),
        lb_fw=lb[0:1], lb_bw=lb[1:2],
    )


def _pick(n, pref):
    t = min(n, pref)
    while n % t:
        t //= 2
    return t


def _trunk(x, p, g_mix, g_onorm, g_qa, g_kva, g_ffn, g_final):
    B, S, D = x.shape
    T = B * S
    tm = _pick(T, 256)
    hg, mla, gt = _in_proj(x.reshape(T, D), g_mix, p["w_hg"], p["w_mla"], p["w_gt"], tm)
    hg = hg.reshape(B, S, -1)
    mla = mla.reshape(B, S, -1)
    gt = gt.reshape(B, S, -1)

    ts = _pick(S, 512)
    o_fw = _hgrn_scan(hg, p["lb_fw"], None, None, reverse=False, ts=ts)
    o_h = _hgrn_scan(hg, p["lb_bw"], o_fw, g_onorm, reverse=True, ts=ts)

    tabs = _rope_tables(S)
    q_t, k, v_t = _mla_prep(mla, g_qa, g_kva, p["wq_t"], p["wk"], p["wv_t"], tabs,
                            ts=_pick(S, 512))
    o_mt = _attention(q_t, k, v_t, tq=_pick(S, 512), tk=_pick(S // 2, 1024))
    o_mt = o_mt.reshape(B, -1, S)

    x1 = _merge(x, o_h, o_mt, gt, p["wb0"], p["wb1"], p["wo"], tm=_pick(S, 256))
    y = _ffn(x1.reshape(T, D), g_ffn, p["wgu"], p["wd"], g_final[None, :], tm=tm)
    return y.reshape(B, S, D)


def kernel(x_prompt, x_sample, g_mix, w_in, lb_param, g_onorm, g_qa, w_uq, g_kva, w_ukv,
           w_branch, w_out, g_ffn, w_gate_up, w_down, g_final):
    p = _prep_weights(w_in, lb_param, w_uq, w_ukv, w_branch, w_out, w_gate_up, w_down)
    args = (p, g_mix, g_onorm, g_qa, g_kva, g_ffn, g_final)
    return (_trunk(x_prompt, *args), _trunk(x_sample, *args))
```

```python
import functools

import jax
import jax.numpy as jnp
import numpy as np
from jax import lax
from jax.experimental import pallas as pl
from jax.experimental.pallas import tpu as pltpu

F32 = jnp.float32
BF16 = jnp.bfloat16

EPS = 1e-6
ROPE_THETA = 10000.0
HG_HEADS = 4
HG_HEAD_DIM = 128
HG_WIDTH = HG_HEADS * HG_HEAD_DIM
HG_CHUNK = 64
QK_NOPE = 64
QK_ROPE = 32
V_HEAD = 64
QK_HEAD = QK_NOPE + QK_ROPE
QK_PAD = 128
V_AUG = V_HEAD + 16
LOG2E = 1.4426950408889634
BOUND_SLACK = 1.0 + 2.0 ** -6
DENOM_FLOOR = 2.0 ** -64
VMEM_LIMIT_BYTES = 56 * 1024 * 1024

NT_DIMS = (((1,), (1,)), ((), ()))
TN_DIMS = (((0,), (0,)), ((), ()))


def _sigmoid(x):
    return 1.0 / (1.0 + jnp.exp(-x))


def _rms(x, g):
    ms = jnp.mean(x * x, axis=-1, keepdims=True)
    return x * lax.rsqrt(ms + EPS) * g


def _params(*sem):
    return pltpu.CompilerParams(dimension_semantics=sem, vmem_limit_bytes=VMEM_LIMIT_BYTES)


def _inproj_kernel(x_ref, g_ref, whg_ref, wmla_ref, wgt_ref, hg_ref, mla_ref, gt_ref):
    h = _rms(x_ref[...], g_ref[...]).astype(BF16)
    hg_ref[...] = jnp.dot(h, whg_ref[...], preferred_element_type=F32)
    mla_ref[...] = jnp.dot(h, wmla_ref[...], preferred_element_type=F32)
    gt_ref[...] = jnp.dot(h, wgt_ref[...], preferred_element_type=F32)


def _in_proj(x2, g_mix, w_hg, w_mla, w_gt, tm):
    T, D = x2.shape
    const = lambda i: (0, 0)
    row = lambda i: (i, 0)
    return pl.pallas_call(
        _inproj_kernel,
        grid=(T // tm,),
        in_specs=[
            pl.BlockSpec((tm, D), row),
            pl.BlockSpec((1, D), const),
            pl.BlockSpec(w_hg.shape, const),
            pl.BlockSpec(w_mla.shape, const),
            pl.BlockSpec(w_gt.shape, const),
        ],
        out_specs=[
            pl.BlockSpec((tm, w_hg.shape[1]), row),
            pl.BlockSpec((tm, w_mla.shape[1]), row),
            pl.BlockSpec((tm, w_gt.shape[1]), row),
        ],
        out_shape=[
            jax.ShapeDtypeStruct((T, w_hg.shape[1]), F32),
            jax.ShapeDtypeStruct((T, w_mla.shape[1]), F32),
            jax.ShapeDtypeStruct((T, w_gt.shape[1]), F32),
        ],
        compiler_params=_params("parallel"),
        name="in_proj",
    )(x2, g_mix, w_hg, w_mla, w_gt)


def _hgrn_head(pq, pv, pf, lb, st_t, tri, reverse):
    C = pq.shape[0]
    f = lb + (1.0 - lb) * _sigmoid(pf)
    g = jnp.log(f)
    kk = 1.0 - f
    g_hi = g.astype(BF16)
    g_lo = (g - g_hi.astype(F32)).astype(BF16)
    tri_b = tri.astype(BF16)
    b = (jnp.dot(tri_b, g_hi, preferred_element_type=F32)
         + jnp.dot(tri_b, g_lo, preferred_element_type=F32))
    mid = C // 2 if reverse else C // 2 - 1
    last = 0 if reverse else C - 1
    ref = b[mid:mid + 1, :]
    b_last = b[last:last + 1, :]
    q = pq * _sigmoid(pq)
    q_in = q * jnp.exp(b - ref)
    k_in = kk * jnp.exp(ref - b)
    att = lax.dot_general(q_in.astype(BF16), k_in.astype(BF16), NT_DIMS,
                          preferred_element_type=F32)
    att = jnp.where(tri, att, 0.0)
    q_inter = q_in * jnp.exp(ref)
    k_state = k_in * jnp.exp(b_last - ref)
    v_b = pv.astype(BF16)
    o = (jnp.dot(att.astype(BF16), v_b, preferred_element_type=F32)
         + lax.dot_general(q_inter.astype(BF16), st_t.astype(BF16), NT_DIMS,
                           preferred_element_type=F32))
    st_new = st_t * jnp.exp(b_last) + lax.dot_general(
        v_b, k_state.astype(BF16), TN_DIMS, preferred_element_type=F32)
    return o, st_new


def _hgrn_kernel(*refs, reverse, chunk):
    if reverse:
        q_ref, v_ref, f_ref, lb_ref, og_ref, ofw_ref, gon_ref, o_ref, st_ref = refs
    else:
        q_ref, v_ref, f_ref, lb_ref, o_ref, st_ref = refs
    ts = q_ref.shape[1]
    nchunks = ts // chunk

    @pl.when(pl.program_id(1) == 0)
    def _():
        st_ref[...] = jnp.zeros_like(st_ref)

    rows = lax.broadcasted_iota(jnp.int32, (chunk, chunk), 0)
    cols = lax.broadcasted_iota(jnp.int32, (chunk, chunk), 1)
    tri = (cols >= rows) if reverse else (cols <= rows)

    def body(ci, carry):
        c = (nchunks - 1 - ci) if reverse else ci
        r0 = pl.multiple_of(c * chunk, chunk)
        rs = pl.ds(r0, chunk)
        for h in range(HG_HEADS):
            ls = slice(h * HG_HEAD_DIM, (h + 1) * HG_HEAD_DIM)
            o, st_new = _hgrn_head(q_ref[0, rs, ls], v_ref[0, rs, ls], f_ref[0, rs, ls],
                                   lb_ref[:, ls], st_ref[h], tri, reverse)
            st_ref[h] = st_new
            if reverse:
                o = o + ofw_ref[0, rs, ls]
                pg = og_ref[0, rs, ls]
                o = _rms(o, gon_ref[:, ls]) * (pg * _sigmoid(pg))
            o_ref[0, rs, ls] = o.astype(o_ref.dtype)
        return carry

    lax.fori_loop(0, nchunks, body, 0)


def _hgrn_scan(hg, lb_row, o_fw, g_onorm, *, reverse, ts):
    B, S, _ = hg.shape
    ns = S // ts
    W = HG_WIDTH
    seq = (lambda s: ns - 1 - s) if reverse else (lambda s: s)

    def col(j):
        return pl.BlockSpec((1, ts, W), lambda b, s: (b, seq(s), j))

    vec = pl.BlockSpec((1, W), lambda b, s: (0, 0))
    plain = pl.BlockSpec((1, ts, W), lambda b, s: (b, seq(s), 0))
    if reverse:
        in_specs = [col(0), col(1), col(3), vec, col(4), plain, vec]
        args = (hg, hg, hg, lb_row, hg, o_fw, g_onorm)
        out_dtype = BF16
    else:
        in_specs = [col(0), col(1), col(2), vec]
        args = (hg, hg, hg, lb_row)
        out_dtype = F32
    return pl.pallas_call(
        functools.partial(_hgrn_kernel, reverse=reverse, chunk=HG_CHUNK),
        grid=(B, ns),
        in_specs=in_specs,
        out_specs=plain,
        out_shape=jax.ShapeDtypeStruct((B, S, W), out_dtype),
        scratch_shapes=[pltpu.VMEM((HG_HEADS, HG_HEAD_DIM, HG_HEAD_DIM), F32)],
        compiler_params=_params("parallel", "arbitrary"),
        name="hgrn_bwd" if reverse else "hgrn_fwd",
    )(*args)


def _mla_prep_kernel(mla_ref, gqa_ref, gkva_ref, wq_ref, wk_ref, wv_ref,
                     cos_t_ref, sin_t_ref, cosk_ref, sina_ref, sinb_ref,
                     qt_ref, k_ref, vt_ref, ksq_ref, *, heads, scale):
    r = gqa_ref.shape[1]
    cq = _rms(mla_ref[0, :, 0:r], gqa_ref[...]).astype(BF16)
    ckv = _rms(mla_ref[0, :, r:2 * r], gkva_ref[...]).astype(BF16)
    kr = mla_ref[0, :, 2 * r:2 * r + QK_PAD]
    half = QK_ROPE // 2
    k_pe = (kr * cosk_ref[...]
            + pltpu.roll(kr, half, 1) * sina_ref[...]
            + pltpu.roll(kr, QK_PAD - half, 1) * sinb_ref[...])
    cos_t = cos_t_ref[...]
    sin_t = sin_t_ref[...]
    pad_rows = vt_ref.shape[2] - V_HEAD
    ones_rows = (lax.broadcasted_iota(jnp.int32, (pad_rows, cos_t.shape[1]), 0) == 0).astype(F32)
    ones_tile = jnp.ones((8, QK_PAD), BF16)
    one_lane = (lax.broadcasted_iota(jnp.int32, (1, QK_PAD), 1) == QK_HEAD).astype(F32)
    for h in range(heads):
        q_t = lax.dot_general(wq_ref[h], cq, NT_DIMS, preferred_element_type=F32)
        x1 = q_t[QK_NOPE:QK_NOPE + half]
        x2 = q_t[QK_NOPE + half:QK_HEAD]
        q_rot = jnp.concatenate(
            [q_t[:QK_NOPE], x1 * cos_t - x2 * sin_t, x2 * cos_t + x1 * sin_t, q_t[QK_HEAD:]],
            axis=0)
        qt_ref[0, h] = (q_rot * scale).astype(qt_ref.dtype)
        k_b = (jnp.dot(ckv, wk_ref[h], preferred_element_type=F32) + k_pe).astype(BF16)
        k_f = k_b.astype(F32)
        ksq = lax.dot_general(ones_tile, (k_f * k_f).astype(BF16), NT_DIMS,
                              preferred_element_type=F32)
        ksq_ref[0, h] = ksq[0:1]
        k_ref[0, h] = (k_f + one_lane).astype(k_ref.dtype)
        v_t = lax.dot_general(wv_ref[h], ckv, NT_DIMS, preferred_element_type=F32)
        vt_ref[0, h] = jnp.concatenate([v_t, ones_rows], axis=0).astype(vt_ref.dtype)


def _mla_prep(mla, g_qa, g_kva, wq_t, wk, wv_t, tabs, *, ts):
    B, S, Wm = mla.shape
    H = wq_t.shape[0]
    cos_t, sin_t, cosk, sina, sinb = tabs
    half = QK_ROPE // 2
    c2 = lambda b, s: (0, 0)
    c3 = lambda b, s: (0, 0, 0)
    return pl.pallas_call(
        functools.partial(_mla_prep_kernel, heads=H, scale=QK_HEAD ** -0.5 * LOG2E),
        grid=(B, S // ts),
        in_specs=[
            pl.BlockSpec((1, ts, Wm), lambda b, s: (b, s, 0)),
            pl.BlockSpec(g_qa.shape, c2),
            pl.BlockSpec(g_kva.shape, c2),
            pl.BlockSpec(wq_t.shape, c3),
            pl.BlockSpec(wk.shape, c3),
            pl.BlockSpec(wv_t.shape, c3),
            pl.BlockSpec((half, ts), lambda b, s: (0, s)),
            pl.BlockSpec((half, ts), lambda b, s: (0, s)),
            pl.BlockSpec((ts, QK_PAD), lambda b, s: (s, 0)),
            pl.BlockSpec((ts, QK_PAD), lambda b, s: (s, 0)),
            pl.BlockSpec((ts, QK_PAD), lambda b, s: (s, 0)),
        ],
        out_specs=[
            pl.BlockSpec((1, H, QK_PAD, ts), lambda b, s: (b, 0, 0, s)),
            pl.BlockSpec((1, H, ts, QK_PAD), lambda b, s: (b, 0, s, 0)),
            pl.BlockSpec((1, H, V_AUG, ts), lambda b, s: (b, 0, 0, s)),
            pl.BlockSpec((1, H, 1, ts), lambda b, s: (b, 0, 0, s)),
        ],
        out_shape=[
            jax.ShapeDtypeStruct((B, H, QK_PAD, S), BF16),
            jax.ShapeDtypeStruct((B, H, S, QK_PAD), BF16),
            jax.ShapeDtypeStruct((B, H, V_AUG, S), BF16),
            jax.ShapeDtypeStruct((B, H, 1, S), F32),
        ],
        compiler_params=_params("parallel", "parallel"),
        name="mla_prep",
    )(mla, g_qa, g_kva, wq_t, wk, wv_t, cos_t, sin_t, cosk, sina, sinb)


def _attn_kernel(qt_ref, k_ref, vt_ref, ksq_ref, o_ref, s_ref, acc_ref, m_ref, *, tk, tk_main):
    q_t = qt_ref[0, 0]
    nk = k_ref.shape[2] // tk

    q_f = q_t.astype(F32)
    q_sq = jnp.sum(q_f * q_f, axis=0, keepdims=True)
    k_sq_max = jnp.max(ksq_ref[0, 0], axis=1, keepdims=True)
    bound = jnp.sqrt(q_sq * k_sq_max) * BOUND_SLACK
    row = lax.broadcasted_iota(jnp.int32, q_f.shape, 0)
    q_off = jnp.where(row == QK_HEAD, -bound, q_f).astype(BF16)

    def bounded_step(j, carry):
        k0 = pl.multiple_of(j * tk_main, tk_main)
        s = jnp.dot(k_ref[0, 0, pl.ds(k0, tk_main), :], q_off, preferred_element_type=F32)
        p = jnp.exp2(s).astype(BF16)
        acc_ref[...] += jnp.dot(vt_ref[0, 0, :, pl.ds(k0, tk_main)], p,
                                preferred_element_type=F32)
        return carry

    acc_ref[...] = jnp.zeros_like(acc_ref)
    lax.fori_loop(0, k_ref.shape[2] // tk_main, bounded_step, 0)

    def scores(j, slot):
        k0 = pl.multiple_of(jnp.minimum(j, nk - 1) * tk, tk)
        s_ref[slot] = jnp.dot(k_ref[0, 0, pl.ds(k0, tk), :], q_t, preferred_element_type=F32)

    def softmax_pv(j, slot):
        s = s_ref[slot]
        m_old = m_ref[...]
        m_new = jnp.maximum(m_old, jnp.max(s, axis=0, keepdims=True))
        p = jnp.exp2(s - m_new).astype(BF16)
        k0 = pl.multiple_of(j * tk, tk)
        pv = jnp.dot(vt_ref[0, 0, :, pl.ds(k0, tk)], p, preferred_element_type=F32)
        acc_ref[...] = jnp.exp2(m_old - m_new) * acc_ref[...] + pv
        m_ref[...] = m_new

    def pair(i, carry):
        j = 2 * i
        scores(j + 1, 1)
        softmax_pv(j, 0)
        scores(j + 2, 0)
        softmax_pv(j + 1, 1)
        return carry

    denom_min = jnp.min(acc_ref[V_HEAD:V_HEAD + 1, :])

    @pl.when(jnp.logical_not(denom_min >= DENOM_FLOOR))
    def _():
        m_ref[...] = jnp.full_like(m_ref, -1e30)
        acc_ref[...] = jnp.zeros_like(acc_ref)
        scores(0, 0)
        lax.fori_loop(0, nk // 2, pair, 0)

    acc = acc_ref[...]
    o_ref[0, 0] = (acc[:V_HEAD] / acc[V_HEAD:V_HEAD + 1]).astype(o_ref.dtype)


def _attention(q_t, k, v_t, k_sq, *, tq, tk, tk_main):
    B, H, _, S = q_t.shape
    VA = v_t.shape[2]
    assert (S // tk) % 2 == 0
    return pl.pallas_call(
        functools.partial(_attn_kernel, tk=tk, tk_main=tk_main),
        grid=(B, H, S // tq),
        in_specs=[
            pl.BlockSpec((1, 1, QK_PAD, tq), lambda b, h, i: (b, h, 0, i)),
            pl.BlockSpec((1, 1, S, QK_PAD), lambda b, h, i: (b, h, 0, 0)),
            pl.BlockSpec((1, 1, VA, S), lambda b, h, i: (b, h, 0, 0)),
            pl.BlockSpec((1, 1, 1, S), lambda b, h, i: (b, h, 0, 0)),
        ],
        out_specs=pl.BlockSpec((1, 1, V_HEAD, tq), lambda b, h, i: (b, h, 0, i)),
        out_shape=jax.ShapeDtypeStruct((B, H, V_HEAD, S), BF16),
        scratch_shapes=[pltpu.VMEM((2, tk, tq), F32), pltpu.VMEM((VA, tq), F32),
                        pltpu.VMEM((1, tq), F32)],
        compiler_params=_params("parallel", "parallel", "arbitrary"),
        name="attention",
    )(q_t, k, v_t, k_sq)


def _merge_kernel(x_ref, oh_ref, omt_ref, gt_ref, wb0_ref, wb1_ref, wo_ref, y_ref):
    D = x_ref.shape[2]
    bp0 = jnp.dot(oh_ref[0], wb0_ref[...], preferred_element_type=F32)
    bp1 = lax.dot_general(omt_ref[0], wb1_ref[...], TN_DIMS, preferred_element_type=F32)
    gt = gt_ref[0]
    merged = _sigmoid(gt[:, :D]) * bp0 + _sigmoid(gt[:, D:]) * bp1
    y_ref[0] = x_ref[0] + jnp.dot(merged.astype(BF16), wo_ref[...], preferred_element_type=F32)


def _merge(x, o_h, o_mt, gt, wb0, wb1, wo, *, tm):
    B, S, D = x.shape
    c2 = lambda b, s: (0, 0)
    tok = lambda w: pl.BlockSpec((1, tm, w), lambda b, s: (b, s, 0))
    return pl.pallas_call(
        _merge_kernel,
        grid=(B, S // tm),
        in_specs=[
            tok(D), tok(o_h.shape[2]),
            pl.BlockSpec((1, o_mt.shape[1], tm), lambda b, s: (b, 0, s)),
            tok(gt.shape[2]),
            pl.BlockSpec(wb0.shape, c2), pl.BlockSpec(wb1.shape, c2), pl.BlockSpec(wo.shape, c2),
        ],
        out_specs=tok(D),
        out_shape=jax.ShapeDtypeStruct((B, S, D), F32),
        compiler_params=_params("parallel", "parallel"),
        name="merge",
    )(x, o_h, o_mt, gt, wb0, wb1, wo)


def _ffn_kernel(x_ref, gf_ref, wgu_ref, wd_ref, gfin_ref, y_ref):
    x = x_ref[...]
    dff = wd_ref.shape[0]
    h2 = _rms(x, gf_ref[...]).astype(BF16)
    gu = jnp.dot(h2, wgu_ref[...], preferred_element_type=F32)
    gate = gu[:, :dff]
    act = (gate * _sigmoid(gate)) * gu[:, dff:]
    x2 = x + jnp.dot(act.astype(BF16), wd_ref[...], preferred_element_type=F32)
    y_ref[...] = _rms(x2, gfin_ref[...])


def _ffn(x2d, g_ffn, wgu, wd, g_final, *, tm):
    T, D = x2d.shape
    const = lambda i: (0, 0)
    row = lambda i: (i, 0)
    return pl.pallas_call(
        _ffn_kernel,
        grid=(T // tm,),
        in_specs=[
            pl.BlockSpec((tm, D), row),
            pl.BlockSpec((1, D), const),
            pl.BlockSpec(wgu.shape, const),
            pl.BlockSpec(wd.shape, const),
            pl.BlockSpec((1, D), const),
        ],
        out_specs=pl.BlockSpec((tm, D), row),
        out_shape=jax.ShapeDtypeStruct((T, D), F32),
        compiler_params=_params("parallel"),
        name="ffn",
    )(x2d, g_ffn, wgu, wd, g_final)


def _rope_tables(S):
    d = QK_ROPE
    half = d // 2
    inv = ROPE_THETA ** (-jnp.arange(0, d, 2, dtype=F32) / d)
    ang = jnp.arange(S, dtype=F32)[:, None] * inv[None, :]
    cos, sin = jnp.cos(ang), jnp.sin(ang)
    z = lambda n: jnp.zeros((S, n), F32)
    tail = QK_PAD - QK_HEAD
    cosk = jnp.concatenate([z(QK_NOPE), cos, cos, z(tail)], axis=1)
    sina = jnp.concatenate([z(QK_NOPE + half), sin, z(tail)], axis=1)
    sinb = jnp.concatenate([z(QK_NOPE), -sin, z(half + tail)], axis=1)
    return cos.T, sin.T, cosk, sina, sinb


def _prep_weights(w_in, lb_param, w_uq, w_ukv, w_branch, w_out, w_gate_up, w_down):
    W = HG_WIDTH
    r_q, H = w_uq.shape[1], w_uq.shape[2]
    r_kv = w_ukv.shape[1]
    w = w_in[0]
    D = w.shape[0]
    c0 = 5 * W
    w_hg = w[:, :c0]
    w_qa = w[:, c0:c0 + r_q]
    w_kva = w[:, c0 + r_q:c0 + r_q + r_kv]
    c1 = c0 + r_q + r_kv
    w_kr = w[:, c1:c1 + QK_ROPE]
    w_gt = w[:, c1 + QK_ROPE:]
    zpad = lambda n: jnp.zeros((D, n), w.dtype)
    w_mla = jnp.concatenate([w_qa, w_kva, zpad(QK_NOPE), w_kr, zpad(QK_PAD - QK_HEAD)], axis=1)
    wq = jnp.transpose(w_uq[0], (1, 2, 0))
    wq_t = jnp.concatenate([wq, jnp.zeros((H, QK_PAD - QK_HEAD, r_q), wq.dtype)], axis=1)
    wkv = jnp.transpose(w_ukv[0], (1, 0, 2))
    wk = jnp.concatenate([wkv[:, :, :QK_NOPE],
                          jnp.zeros((H, r_kv, QK_PAD - QK_NOPE), wkv.dtype)], axis=2)
    wv_t = jnp.transpose(wkv[:, :, QK_NOPE:], (0, 2, 1))
    lb = jax.nn.softmax(lb_param.astype(F32), axis=0)[0]
    b16 = lambda a: a.astype(BF16)
    return dict(
        w_hg=b16(w_hg), w_mla=b16(w_mla), w_gt=b16(w_gt),
        wq_t=b16(wq_t), wk=b16(wk), wv_t=b16(wv_t),
        wb0=b16(w_branch[0, 0]), wb1=b16(w_branch[0, 1]), wo=b16(w_out[0]),
        wgu=b16(w_gate_up[0]), wd=b16(w_down[0]),
        lb_fw=lb[0:1], lb_bw=lb[1:2],
    )


def _pick(n, pref):
    t = min(n, pref)
    while n % t:
        t //= 2
    return t


def _trunk(x, p, g_mix, g_onorm, g_qa, g_kva, g_ffn, g_final):
    B, S, D = x.shape
    T = B * S
    tm = _pick(T, 256)
    hg, mla, gt = _in_proj(x.reshape(T, D), g_mix, p["w_hg"], p["w_mla"], p["w_gt"], tm)
    hg = hg.reshape(B, S, -1)
    mla = mla.reshape(B, S, -1)
    gt = gt.reshape(B, S, -1)

    ts = _pick(S, 512)
    o_fw = _hgrn_scan(hg, p["lb_fw"], None, None, reverse=False, ts=ts)
    o_h = _hgrn_scan(hg, p["lb_bw"], o_fw, g_onorm, reverse=True, ts=ts)

    tabs = _rope_tables(S)
    q_t, k, v_t, k_sq = _mla_prep(mla, g_qa, g_kva, p["wq_t"], p["wk"], p["wv_t"], tabs,
                            ts=_pick(S, 512))
    o_mt = _attention(q_t, k, v_t, k_sq, tq=_pick(S, 512), tk=_pick(S // 2, 1024),
                      tk_main=_pick(S, 4096))
    o_mt = o_mt.reshape(B, -1, S)

    x1 = _merge(x, o_h, o_mt, gt, p["wb0"], p["wb1"], p["wo"], tm=_pick(S, 256))
    y = _ffn(x1.reshape(T, D), g_ffn, p["wgu"], p["wd"], g_final[None, :], tm=tm)
    return y.reshape(B, S, D)


def kernel(x_prompt, x_sample, g_mix, w_in, lb_param, g_onorm, g_qa, w_uq, g_kva, w_ukv,
           w_branch, w_out, g_ffn, w_gate_up, w_down, g_final):
    p = _prep_weights(w_in, lb_param, w_uq, w_ukv, w_branch, w_out, w_gate_up, w_down)
    args = (p, g_mix, g_onorm, g_qa, g_kva, g_ffn, g_final)
    return (_trunk(x_prompt, *args), _trunk(x_sample, *args))
```

```python
import functools

import jax
import jax.numpy as jnp
import numpy as np
from jax import lax
from jax.experimental import pallas as pl
from jax.experimental.pallas import tpu as pltpu

F32 = jnp.float32
BF16 = jnp.bfloat16

EPS = 1e-6
ROPE_THETA = 10000.0
HG_HEADS = 4
HG_HEAD_DIM = 128
HG_WIDTH = HG_HEADS * HG_HEAD_DIM
HG_CHUNK = 64
QK_NOPE = 64
QK_ROPE = 32
V_HEAD = 64
QK_HEAD = QK_NOPE + QK_ROPE
QK_PAD = 128
V_AUG = V_HEAD + 16
LOG2E = 1.4426950408889634
BOUND_SLACK = 1.0 + 2.0 ** -6
DENOM_FLOOR = 2.0 ** -64
VMEM_LIMIT_BYTES = 56 * 1024 * 1024

NT_DIMS = (((1,), (1,)), ((), ()))
TN_DIMS = (((0,), (0,)), ((), ()))


def _sigmoid(x):
    return 1.0 / (1.0 + jnp.exp(-x))


def _rms(x, g):
    ms = jnp.mean(x * x, axis=-1, keepdims=True)
    return x * lax.rsqrt(ms + EPS) * g


def _params(*sem):
    return pltpu.CompilerParams(dimension_semantics=sem, vmem_limit_bytes=VMEM_LIMIT_BYTES)


def _inproj_kernel(x_ref, g_ref, whg_ref, wmla_ref, wgt_ref, hg_ref, mla_ref, gt_ref):
    h = _rms(x_ref[...], g_ref[...]).astype(BF16)
    hg_ref[...] = jnp.dot(h, whg_ref[...], preferred_element_type=F32)
    mla_ref[...] = jnp.dot(h, wmla_ref[...], preferred_element_type=F32)
    gt_ref[...] = jnp.dot(h, wgt_ref[...], preferred_element_type=F32)


def _in_proj(x2, g_mix, w_hg, w_mla, w_gt, tm):
    T, D = x2.shape
    const = lambda i: (0, 0)
    row = lambda i: (i, 0)
    return pl.pallas_call(
        _inproj_kernel,
        grid=(T // tm,),
        in_specs=[
            pl.BlockSpec((tm, D), row),
            pl.BlockSpec((1, D), const),
            pl.BlockSpec(w_hg.shape, const),
            pl.BlockSpec(w_mla.shape, const),
            pl.BlockSpec(w_gt.shape, const),
        ],
        out_specs=[
            pl.BlockSpec((tm, w_hg.shape[1]), row),
            pl.BlockSpec((tm, w_mla.shape[1]), row),
            pl.BlockSpec((tm, w_gt.shape[1]), row),
        ],
        out_shape=[
            jax.ShapeDtypeStruct((T, w_hg.shape[1]), F32),
            jax.ShapeDtypeStruct((T, w_mla.shape[1]), F32),
            jax.ShapeDtypeStruct((T, w_gt.shape[1]), F32),
        ],
        compiler_params=_params("parallel"),
        name="in_proj",
    )(x2, g_mix, w_hg, w_mla, w_gt)


def _hgrn_kernel(*refs, reverse, chunk):
    if reverse:
        q_ref, v_ref, f_ref, lb_ref, og_ref, ofw_ref, gon_ref, o_ref, st_ref = refs
    else:
        q_ref, v_ref, f_ref, lb_ref, o_ref, st_ref = refs
    ts = q_ref.shape[1]
    C = chunk
    nc = ts // C
    D = HG_HEAD_DIM
    heads = range(HG_HEADS)

    @pl.when(pl.program_id(1) == 0)
    def _():
        st_ref[...] = jnp.zeros_like(st_ref)

    rows = lax.broadcasted_iota(jnp.int32, (C, C), 0)
    cols = lax.broadcasted_iota(jnp.int32, (C, C), 1)
    tri = (cols >= rows) if reverse else (cols <= rows)
    tri_b = tri.astype(BF16)
    mid = C // 2 if reverse else C // 2 - 1
    last = 0 if reverse else C - 1
    lb = lb_ref[...]

    q_in, k_in, q_inter, k_state, v_b, decay = [], [], [], [], [], []
    for c in range(nc):
        rs = slice(c * C, (c + 1) * C)
        f = lb + (1.0 - lb) * _sigmoid(f_ref[0, rs, :])
        g = jnp.log(f)
        kk = 1.0 - f
        g_hi = g.astype(BF16)
        g_lo = (g - g_hi.astype(F32)).astype(BF16)
        b = (jnp.dot(tri_b, g_hi, preferred_element_type=F32)
             + jnp.dot(tri_b, g_lo, preferred_element_type=F32))
        ref = b[mid:mid + 1, :]
        b_last = b[last:last + 1, :]
        pq = q_ref[0, rs, :]
        qi = (pq * _sigmoid(pq)) * jnp.exp(b - ref)
        ki = kk * jnp.exp(ref - b)
        q_in.append(qi.astype(BF16))
        k_in.append(ki.astype(BF16))
        q_inter.append((qi * jnp.exp(ref)).astype(BF16))
        k_state.append((ki * jnp.exp(b_last - ref)).astype(BF16))
        v_b.append(v_ref[0, rs, :].astype(BF16))
        decay.append(jnp.exp(b_last))

    o_intra = [[None] * HG_HEADS for _ in range(nc)]
    incr = [[None] * HG_HEADS for _ in range(nc)]
    for c in range(nc):
        for h in heads:
            ls = slice(h * D, (h + 1) * D)
            att = lax.dot_general(q_in[c][:, ls], k_in[c][:, ls], NT_DIMS,
                                  preferred_element_type=F32)
            att = jnp.where(tri, att, 0.0).astype(BF16)
            o_intra[c][h] = jnp.dot(att, v_b[c][:, ls], preferred_element_type=F32)
            incr[c][h] = lax.dot_general(v_b[c][:, ls], k_state[c][:, ls], TN_DIMS,
                                         preferred_element_type=F32)

    st = [st_ref[h] for h in heads]
    for c in (range(nc - 1, -1, -1) if reverse else range(nc)):
        rs = slice(c * C, (c + 1) * C)
        for h in heads:
            ls = slice(h * D, (h + 1) * D)
            o = o_intra[c][h] + lax.dot_general(q_inter[c][:, ls], st[h].astype(BF16), NT_DIMS,
                                                preferred_element_type=F32)
            st[h] = st[h] * decay[c][:, ls] + incr[c][h]
            if reverse:
                o = o + ofw_ref[0, rs, ls]
                pg = og_ref[0, rs, ls]
                o = _rms(o, gon_ref[:, ls]) * (pg * _sigmoid(pg))
            o_ref[0, rs, ls] = o.astype(o_ref.dtype)
    for h in heads:
        st_ref[h] = st[h]


def _hgrn_scan(hg, lb_row, o_fw, g_onorm, *, reverse, ts):
    B, S, _ = hg.shape
    ns = S // ts
    W = HG_WIDTH
    seq = (lambda s: ns - 1 - s) if reverse else (lambda s: s)

    def col(j):
        return pl.BlockSpec((1, ts, W), lambda b, s: (b, seq(s), j))

    vec = pl.BlockSpec((1, W), lambda b, s: (0, 0))
    plain = pl.BlockSpec((1, ts, W), lambda b, s: (b, seq(s), 0))
    if reverse:
        in_specs = [col(0), col(1), col(3), vec, col(4), plain, vec]
        args = (hg, hg, hg, lb_row, hg, o_fw, g_onorm)
        out_dtype = BF16
    else:
        in_specs = [col(0), col(1), col(2), vec]
        args = (hg, hg, hg, lb_row)
        out_dtype = F32
    return pl.pallas_call(
        functools.partial(_hgrn_kernel, reverse=reverse, chunk=HG_CHUNK),
        grid=(B, ns),
        in_specs=in_specs,
        out_specs=plain,
        out_shape=jax.ShapeDtypeStruct((B, S, W), out_dtype),
        scratch_shapes=[pltpu.VMEM((HG_HEADS, HG_HEAD_DIM, HG_HEAD_DIM), F32)],
        compiler_params=_params("parallel", "arbitrary"),
        name="hgrn_bwd" if reverse else "hgrn_fwd",
    )(*args)


def _mla_prep_kernel(mla_ref, gqa_ref, gkva_ref, wq_ref, wk_ref, wv_ref,
                     cos_t_ref, sin_t_ref, cosk_ref, sina_ref, sinb_ref,
                     qt_ref, k_ref, vt_ref, ksq_ref, *, heads, scale):
    r = gqa_ref.shape[1]
    cq = _rms(mla_ref[0, :, 0:r], gqa_ref[...]).astype(BF16)
    ckv = _rms(mla_ref[0, :, r:2 * r], gkva_ref[...]).astype(BF16)
    kr = mla_ref[0, :, 2 * r:2 * r + QK_PAD]
    half = QK_ROPE // 2
    k_pe = (kr * cosk_ref[...]
            + pltpu.roll(kr, half, 1) * sina_ref[...]
            + pltpu.roll(kr, QK_PAD - half, 1) * sinb_ref[...])
    cos_t = cos_t_ref[...]
    sin_t = sin_t_ref[...]
    pad_rows = vt_ref.shape[2] - V_HEAD
    ones_rows = (lax.broadcasted_iota(jnp.int32, (pad_rows, cos_t.shape[1]), 0) == 0).astype(F32)
    ones_tile = jnp.ones((8, QK_PAD), BF16)
    one_lane = (lax.broadcasted_iota(jnp.int32, (1, QK_PAD), 1) == QK_HEAD).astype(F32)
    for h in range(heads):
        q_t = lax.dot_general(wq_ref[h], cq, NT_DIMS, preferred_element_type=F32)
        x1 = q_t[QK_NOPE:QK_NOPE + half]
        x2 = q_t[QK_NOPE + half:QK_HEAD]
        q_rot = jnp.concatenate(
            [q_t[:QK_NOPE], x1 * cos_t - x2 * sin_t, x2 * cos_t + x1 * sin_t, q_t[QK_HEAD:]],
            axis=0)
        qt_ref[0, h] = (q_rot * scale).astype(qt_ref.dtype)
        k_b = (jnp.dot(ckv, wk_ref[h], preferred_element_type=F32) + k_pe).astype(BF16)
        k_f = k_b.astype(F32)
        ksq = lax.dot_general(ones_tile, (k_f * k_f).astype(BF16), NT_DIMS,
                              preferred_element_type=F32)
        ksq_ref[0, h] = ksq[0:1]
        k_ref[0, h] = (k_f + one_lane).astype(k_ref.dtype)
        v_t = lax.dot_general(wv_ref[h], ckv, NT_DIMS, preferred_element_type=F32)
        vt_ref[0, h] = jnp.concatenate([v_t, ones_rows], axis=0).astype(vt_ref.dtype)


def _mla_prep(mla, g_qa, g_kva, wq_t, wk, wv_t, tabs, *, ts):
    B, S, Wm = mla.shape
    H = wq_t.shape[0]
    cos_t, sin_t, cosk, sina, sinb = tabs
    half = QK_ROPE // 2
    c2 = lambda b, s: (0, 0)
    c3 = lambda b, s: (0, 0, 0)
    return pl.pallas_call(
        functools.partial(_mla_prep_kernel, heads=H, scale=QK_HEAD ** -0.5 * LOG2E),
        grid=(B, S // ts),
        in_specs=[
            pl.BlockSpec((1, ts, Wm), lambda b, s: (b, s, 0)),
            pl.BlockSpec(g_qa.shape, c2),
            pl.BlockSpec(g_kva.shape, c2),
            pl.BlockSpec(wq_t.shape, c3),
            pl.BlockSpec(wk.shape, c3),
            pl.BlockSpec(wv_t.shape, c3),
            pl.BlockSpec((half, ts), lambda b, s: (0, s)),
            pl.BlockSpec((half, ts), lambda b, s: (0, s)),
            pl.BlockSpec((ts, QK_PAD), lambda b, s: (s, 0)),
            pl.BlockSpec((ts, QK_PAD), lambda b, s: (s, 0)),
            pl.BlockSpec((ts, QK_PAD), lambda b, s: (s, 0)),
        ],
        out_specs=[
            pl.BlockSpec((1, H, QK_PAD, ts), lambda b, s: (b, 0, 0, s)),
            pl.BlockSpec((1, H, ts, QK_PAD), lambda b, s: (b, 0, s, 0)),
            pl.BlockSpec((1, H, V_AUG, ts), lambda b, s: (b, 0, 0, s)),
            pl.BlockSpec((1, H, 1, ts), lambda b, s: (b, 0, 0, s)),
        ],
        out_shape=[
            jax.ShapeDtypeStruct((B, H, QK_PAD, S), BF16),
            jax.ShapeDtypeStruct((B, H, S, QK_PAD), BF16),
            jax.ShapeDtypeStruct((B, H, V_AUG, S), BF16),
            jax.ShapeDtypeStruct((B, H, 1, S), F32),
        ],
        compiler_params=_params("parallel", "parallel"),
        name="mla_prep",
    )(mla, g_qa, g_kva, wq_t, wk, wv_t, cos_t, sin_t, cosk, sina, sinb)


def _attn_kernel(qt_ref, k_ref, vt_ref, ksq_ref, o_ref, s_ref, acc_ref, m_ref, *, tk, tk_main):
    q_t = qt_ref[0, 0]
    nk = k_ref.shape[2] // tk

    q_f = q_t.astype(F32)
    q_sq = jnp.sum(q_f * q_f, axis=0, keepdims=True)
    k_sq_max = jnp.max(ksq_ref[0, 0], axis=1, keepdims=True)
    bound = jnp.sqrt(q_sq * k_sq_max) * BOUND_SLACK
    row = lax.broadcasted_iota(jnp.int32, q_f.shape, 0)
    q_off = jnp.where(row == QK_HEAD, -bound, q_f).astype(BF16)

    def bounded_step(j, carry):
        k0 = pl.multiple_of(j * tk_main, tk_main)
        s = jnp.dot(k_ref[0, 0, pl.ds(k0, tk_main), :], q_off, preferred_element_type=F32)
        p = jnp.exp2(s).astype(BF16)
        acc_ref[...] += jnp.dot(vt_ref[0, 0, :, pl.ds(k0, tk_main)], p,
                                preferred_element_type=F32)
        return carry

    acc_ref[...] = jnp.zeros_like(acc_ref)
    lax.fori_loop(0, k_ref.shape[2] // tk_main, bounded_step, 0)

    def scores(j, slot):
        k0 = pl.multiple_of(jnp.minimum(j, nk - 1) * tk, tk)
        s_ref[slot] = jnp.dot(k_ref[0, 0, pl.ds(k0, tk), :], q_t, preferred_element_type=F32)

    def softmax_pv(j, slot):
        s = s_ref[slot]
        m_old = m_ref[...]
        m_new = jnp.maximum(m_old, jnp.max(s, axis=0, keepdims=True))
        p = jnp.exp2(s - m_new).astype(BF16)
        k0 = pl.multiple_of(j * tk, tk)
        pv = jnp.dot(vt_ref[0, 0, :, pl.ds(k0, tk)], p, preferred_element_type=F32)
        acc_ref[...] = jnp.exp2(m_old - m_new) * acc_ref[...] + pv
        m_ref[...] = m_new

    def pair(i, carry):
        j = 2 * i
        scores(j + 1, 1)
        softmax_pv(j, 0)
        scores(j + 2, 0)
        softmax_pv(j + 1, 1)
        return carry

    denom_min = jnp.min(acc_ref[V_HEAD:V_HEAD + 1, :])

    @pl.when(jnp.logical_not(denom_min >= DENOM_FLOOR))
    def _():
        m_ref[...] = jnp.full_like(m_ref, -1e30)
        acc_ref[...] = jnp.zeros_like(acc_ref)
        scores(0, 0)
        lax.fori_loop(0, nk // 2, pair, 0)

    acc = acc_ref[...]
    o_ref[0, 0] = (acc[:V_HEAD] / acc[V_HEAD:V_HEAD + 1]).astype(o_ref.dtype)


def _attention(q_t, k, v_t, k_sq, *, tq, tk, tk_main):
    B, H, _, S = q_t.shape
    VA = v_t.shape[2]
    assert (S // tk) % 2 == 0
    return pl.pallas_call(
        functools.partial(_attn_kernel, tk=tk, tk_main=tk_main),
        grid=(B, H, S // tq),
        in_specs=[
            pl.BlockSpec((1, 1, QK_PAD, tq), lambda b, h, i: (b, h, 0, i)),
            pl.BlockSpec((1, 1, S, QK_PAD), lambda b, h, i: (b, h, 0, 0)),
            pl.BlockSpec((1, 1, VA, S), lambda b, h, i: (b, h, 0, 0)),
            pl.BlockSpec((1, 1, 1, S), lambda b, h, i: (b, h, 0, 0)),
        ],
        out_specs=pl.BlockSpec((1, 1, V_HEAD, tq), lambda b, h, i: (b, h, 0, i)),
        out_shape=jax.ShapeDtypeStruct((B, H, V_HEAD, S), BF16),
        scratch_shapes=[pltpu.VMEM((2, tk, tq), F32), pltpu.VMEM((VA, tq), F32),
                        pltpu.VMEM((1, tq), F32)],
        compiler_params=_params("parallel", "parallel", "arbitrary"),
        name="attention",
    )(q_t, k, v_t, k_sq)


def _merge_kernel(x_ref, oh_ref, omt_ref, gt_ref, wb0_ref, wb1_ref, wo_ref, y_ref):
    D = x_ref.shape[2]
    bp0 = jnp.dot(oh_ref[0], wb0_ref[...], preferred_element_type=F32)
    bp1 = lax.dot_general(omt_ref[0], wb1_ref[...], TN_DIMS, preferred_element_type=F32)
    gt = gt_ref[0]
    merged = _sigmoid(gt[:, :D]) * bp0 + _sigmoid(gt[:, D:]) * bp1
    y_ref[0] = x_ref[0] + jnp.dot(merged.astype(BF16), wo_ref[...], preferred_element_type=F32)


def _merge(x, o_h, o_mt, gt, wb0, wb1, wo, *, tm):
    B, S, D = x.shape
    c2 = lambda b, s: (0, 0)
    tok = lambda w: pl.BlockSpec((1, tm, w), lambda b, s: (b, s, 0))
    return pl.pallas_call(
        _merge_kernel,
        grid=(B, S // tm),
        in_specs=[
            tok(D), tok(o_h.shape[2]),
            pl.BlockSpec((1, o_mt.shape[1], tm), lambda b, s: (b, 0, s)),
            tok(gt.shape[2]),
            pl.BlockSpec(wb0.shape, c2), pl.BlockSpec(wb1.shape, c2), pl.BlockSpec(wo.shape, c2),
        ],
        out_specs=tok(D),
        out_shape=jax.ShapeDtypeStruct((B, S, D), F32),
        compiler_params=_params("parallel", "parallel"),
        name="merge",
    )(x, o_h, o_mt, gt, wb0, wb1, wo)


def _ffn_kernel(x_ref, gf_ref, wgu_ref, wd_ref, gfin_ref, y_ref):
    x = x_ref[...]
    dff = wd_ref.shape[0]
    h2 = _rms(x, gf_ref[...]).astype(BF16)
    gu = jnp.dot(h2, wgu_ref[...], preferred_element_type=F32)
    gate = gu[:, :dff]
    act = (gate * _sigmoid(gate)) * gu[:, dff:]
    x2 = x + jnp.dot(act.astype(BF16), wd_ref[...], preferred_element_type=F32)
    y_ref[...] = _rms(x2, gfin_ref[...])


def _ffn(x2d, g_ffn, wgu, wd, g_final, *, tm):
    T, D = x2d.shape
    const = lambda i: (0, 0)
    row = lambda i: (i, 0)
    return pl.pallas_call(
        _ffn_kernel,
        grid=(T // tm,),
        in_specs=[
            pl.BlockSpec((tm, D), row),
            pl.BlockSpec((1, D), const),
            pl.BlockSpec(wgu.shape, const),
            pl.BlockSpec(wd.shape, const),
            pl.BlockSpec((1, D), const),
        ],
        out_specs=pl.BlockSpec((tm, D), row),
        out_shape=jax.ShapeDtypeStruct((T, D), F32),
        compiler_params=_params("parallel"),
        name="ffn",
    )(x2d, g_ffn, wgu, wd, g_final)


def _rope_tables(S):
    d = QK_ROPE
    half = d // 2
    inv = ROPE_THETA ** (-jnp.arange(0, d, 2, dtype=F32) / d)
    ang = jnp.arange(S, dtype=F32)[:, None] * inv[None, :]
    cos, sin = jnp.cos(ang), jnp.sin(ang)
    z = lambda n: jnp.zeros((S, n), F32)
    tail = QK_PAD - QK_HEAD
    cosk = jnp.concatenate([z(QK_NOPE), cos, cos, z(tail)], axis=1)
    sina = jnp.concatenate([z(QK_NOPE + half), sin, z(tail)], axis=1)
    sinb = jnp.concatenate([z(QK_NOPE), -sin, z(half + tail)], axis=1)
    return cos.T, sin.T, cosk, sina, sinb


def _prep_weights(w_in, lb_param, w_uq, w_ukv, w_branch, w_out, w_gate_up, w_down):
    W = HG_WIDTH
    r_q, H = w_uq.shape[1], w_uq.shape[2]
    r_kv = w_ukv.shape[1]
    w = w_in[0]
    D = w.shape[0]
    c0 = 5 * W
    w_hg = w[:, :c0]
    w_qa = w[:, c0:c0 + r_q]
    w_kva = w[:, c0 + r_q:c0 + r_q + r_kv]
    c1 = c0 + r_q + r_kv
    w_kr = w[:, c1:c1 + QK_ROPE]
    w_gt = w[:, c1 + QK_ROPE:]
    zpad = lambda n: jnp.zeros((D, n), w.dtype)
    w_mla = jnp.concatenate([w_qa, w_kva, zpad(QK_NOPE), w_kr, zpad(QK_PAD - QK_HEAD)], axis=1)
    wq = jnp.transpose(w_uq[0], (1, 2, 0))
    wq_t = jnp.concatenate([wq, jnp.zeros((H, QK_PAD - QK_HEAD, r_q), wq.dtype)], axis=1)
    wkv = jnp.transpose(w_ukv[0], (1, 0, 2))
    wk = jnp.concatenate([wkv[:, :, :QK_NOPE],
                          jnp.zeros((H, r_kv, QK_PAD - QK_NOPE), wkv.dtype)], axis=2)
    wv_t = jnp.transpose(wkv[:, :, QK_NOPE:], (0, 2, 1))
    lb = jax.nn.softmax(lb_param.astype(F32), axis=0)[0]
    b16 = lambda a: a.astype(BF16)
    return dict(
        w_hg=b16(w_hg), w_mla=b16(w_mla), w_gt=b16(w_gt),
        wq_t=b16(wq_t), wk=b16(wk), wv_t=b16(wv_t),
        wb0=b16(w_branch[0, 0]), wb1=b16(w_branch[0, 1]), wo=b16(w_out[0]),
        wgu=b16(w_gate_up[0]), wd=b16(w_down[0]),
        lb_fw=lb[0:1], lb_bw=lb[1:2],
    )


def _pick(n, pref):
    t = min(n, pref)
    while n % t:
        t //= 2
    return t


def _trunk(x, p, g_mix, g_onorm, g_qa, g_kva, g_ffn, g_final):
    B, S, D = x.shape
    T = B * S
    tm = _pick(T, 256)
    hg, mla, gt = _in_proj(x.reshape(T, D), g_mix, p["w_hg"], p["w_mla"], p["w_gt"], tm)
    hg = hg.reshape(B, S, -1)
    mla = mla.reshape(B, S, -1)
    gt = gt.reshape(B, S, -1)

    ts = _pick(S, 512)
    o_fw = _hgrn_scan(hg, p["lb_fw"], None, None, reverse=False, ts=ts)
    o_h = _hgrn_scan(hg, p["lb_bw"], o_fw, g_onorm, reverse=True, ts=ts)

    tabs = _rope_tables(S)
    q_t, k, v_t, k_sq = _mla_prep(mla, g_qa, g_kva, p["wq_t"], p["wk"], p["wv_t"], tabs,
                            ts=_pick(S, 512))
    o_mt = _attention(q_t, k, v_t, k_sq, tq=_pick(S, 512), tk=_pick(S // 2, 1024),
                      tk_main=_pick(S, 4096))
    o_mt = o_mt.reshape(B, -1, S)

    x1 = _merge(x, o_h, o_mt, gt, p["wb0"], p["wb1"], p["wo"], tm=_pick(S, 256))
    y = _ffn(x1.reshape(T, D), g_ffn, p["wgu"], p["wd"], g_final[None, :], tm=tm)
    return y.reshape(B, S, D)


def kernel(x_prompt, x_sample, g_mix, w_in, lb_param, g_onorm, g_qa, w_uq, g_kva, w_ukv,
           w_branch, w_out, g_ffn, w_gate_up, w_down, g_final):
    p = _prep_weights(w_in, lb_param, w_uq, w_ukv, w_branch, w_out, w_gate_up, w_down)
    args = (p, g_mix, g_onorm, g_qa, g_kva, g_ffn, g_final)
    return (_trunk(x_prompt, *args), _trunk(x_sample, *args))
```

```python
import functools

import jax
import jax.numpy as jnp
import numpy as np
from jax import lax
from jax.experimental import pallas as pl
from jax.experimental.pallas import tpu as pltpu

F32 = jnp.float32
BF16 = jnp.bfloat16

EPS = 1e-6
ROPE_THETA = 10000.0
HG_HEADS = 4
HG_HEAD_DIM = 128
HG_WIDTH = HG_HEADS * HG_HEAD_DIM
HG_CHUNK = 64
QK_NOPE = 64
QK_ROPE = 32
V_HEAD = 64
QK_HEAD = QK_NOPE + QK_ROPE
QK_PAD = 128
V_AUG = V_HEAD + 16
LOG2E = 1.4426950408889634
BOUND_SLACK = 1.0 + 2.0 ** -6
DENOM_FLOOR = 2.0 ** -64
VMEM_LIMIT_BYTES = 56 * 1024 * 1024

NT_DIMS = (((1,), (1,)), ((), ()))
TN_DIMS = (((0,), (0,)), ((), ()))


def _sigmoid(x):
    return 1.0 / (1.0 + jnp.exp(-x))


def _rms(x, g):
    ms = jnp.mean(x * x, axis=-1, keepdims=True)
    return x * lax.rsqrt(ms + EPS) * g


def _params(*sem):
    return pltpu.CompilerParams(dimension_semantics=sem, vmem_limit_bytes=VMEM_LIMIT_BYTES)


def _inproj_kernel(x_ref, g_ref, wact_ref, wdec_ref, wmla_ref, wgt_ref,
                   act_ref, dec_ref, mla_ref, gt_ref):
    W = HG_WIDTH
    h = _rms(x_ref[...], g_ref[...]).astype(BF16)
    a = jnp.dot(h, wact_ref[...], preferred_element_type=F32)
    q, i, g = a[:, :W], a[:, W:2 * W], a[:, 2 * W:]
    act_ref[...] = jnp.concatenate([q * _sigmoid(q), i, g * _sigmoid(g)], axis=1).astype(BF16)
    dec_ref[...] = jnp.dot(h, wdec_ref[...], preferred_element_type=F32)
    mla_ref[...] = jnp.dot(h, wmla_ref[...], preferred_element_type=F32)
    gt_ref[...] = _sigmoid(jnp.dot(h, wgt_ref[...], preferred_element_type=F32)).astype(BF16)


def _resident(shape):
    zeros = (0,) * len(shape)
    return pl.BlockSpec(shape, lambda *_: zeros, pipeline_mode=pl.Buffered(1))


def _in_proj(x2, g_mix, w_act, w_dec, w_mla, w_gt, tm):
    T, D = x2.shape
    row = lambda i: (i, 0)
    ws = (w_act, w_dec, w_mla, w_gt)
    dtypes = (BF16, F32, F32, BF16)
    return pl.pallas_call(
        _inproj_kernel,
        grid=(T // tm,),
        in_specs=[pl.BlockSpec((tm, D), row), _resident((1, D))] + [_resident(w.shape) for w in ws],
        out_specs=[pl.BlockSpec((tm, w.shape[1]), row) for w in ws],
        out_shape=[jax.ShapeDtypeStruct((T, w.shape[1]), dt) for w, dt in zip(ws, dtypes)],
        compiler_params=_params("parallel"),
        name="in_proj",
    )(x2, g_mix, *ws)


def _hgrn_kernel(*refs, reverse, chunk):
    if reverse:
        q_ref, v_ref, f_ref, lb_ref, og_ref, ofw_ref, gon_ref, o_ref, st_ref = refs
    else:
        q_ref, v_ref, f_ref, lb_ref, o_ref, st_ref = refs
    ts = q_ref.shape[1]
    C = chunk
    nc = ts // C
    D = HG_HEAD_DIM
    heads = range(HG_HEADS)

    @pl.when(pl.program_id(1) == 0)
    def _():
        st_ref[...] = jnp.zeros_like(st_ref)

    rows = lax.broadcasted_iota(jnp.int32, (C, C), 0)
    cols = lax.broadcasted_iota(jnp.int32, (C, C), 1)
    tri = (cols >= rows) if reverse else (cols <= rows)
    tri_b = tri.astype(BF16)
    mid = C // 2 if reverse else C // 2 - 1
    last = 0 if reverse else C - 1
    lb = lb_ref[...]

    q_in, k_in, q_inter, k_state, v_b, decay = [], [], [], [], [], []
    for c in range(nc):
        rs = slice(c * C, (c + 1) * C)
        f = lb + (1.0 - lb) * _sigmoid(f_ref[0, rs, :])
        g = jnp.log(f)
        kk = 1.0 - f
        g_hi = g.astype(BF16)
        g_lo = (g - g_hi.astype(F32)).astype(BF16)
        b = (jnp.dot(tri_b, g_hi, preferred_element_type=F32)
             + jnp.dot(tri_b, g_lo, preferred_element_type=F32))
        ref = b[mid:mid + 1, :]
        b_last = b[last:last + 1, :]
        qi = q_ref[0, rs, :].astype(F32) * jnp.exp(b - ref)
        ki = kk * jnp.exp(ref - b)
        q_in.append(qi.astype(BF16))
        k_in.append(ki.astype(BF16))
        q_inter.append((qi * jnp.exp(ref)).astype(BF16))
        k_state.append((ki * jnp.exp(b_last - ref)).astype(BF16))
        v_b.append(v_ref[0, rs, :])
        decay.append(jnp.exp(b_last))

    o_intra = [[None] * HG_HEADS for _ in range(nc)]
    incr = [[None] * HG_HEADS for _ in range(nc)]
    for c in range(nc):
        for h in heads:
            ls = slice(h * D, (h + 1) * D)
            att = lax.dot_general(q_in[c][:, ls], k_in[c][:, ls], NT_DIMS,
                                  preferred_element_type=F32)
            att = jnp.where(tri, att, 0.0).astype(BF16)
            o_intra[c][h] = jnp.dot(att, v_b[c][:, ls], preferred_element_type=F32)
            incr[c][h] = lax.dot_general(v_b[c][:, ls], k_state[c][:, ls], TN_DIMS,
                                         preferred_element_type=F32)

    st = [st_ref[h] for h in heads]
    for c in (range(nc - 1, -1, -1) if reverse else range(nc)):
        rs = slice(c * C, (c + 1) * C)
        for h in heads:
            ls = slice(h * D, (h + 1) * D)
            o = o_intra[c][h] + lax.dot_general(q_inter[c][:, ls], st[h].astype(BF16), NT_DIMS,
                                                preferred_element_type=F32)
            st[h] = st[h] * decay[c][:, ls] + incr[c][h]
            if reverse:
                o = o + ofw_ref[0, rs, ls]
                o = _rms(o, gon_ref[:, ls]) * og_ref[0, rs, ls].astype(F32)
            o_ref[0, rs, ls] = o.astype(o_ref.dtype)
    for h in heads:
        st_ref[h] = st[h]


def _hgrn_scan(act, dec, lb_row, o_fw, g_onorm, *, reverse, ts):
    B, S, _ = act.shape
    ns = S // ts
    W = HG_WIDTH
    seq = (lambda s: ns - 1 - s) if reverse else (lambda s: s)

    def col(j):
        return pl.BlockSpec((1, ts, W), lambda b, s: (b, seq(s), j))

    vec = pl.BlockSpec((1, W), lambda b, s: (0, 0))
    plain = pl.BlockSpec((1, ts, W), lambda b, s: (b, seq(s), 0))
    if reverse:
        in_specs = [col(0), col(1), col(1), vec, col(2), plain, vec]
        args = (act, act, dec, lb_row, act, o_fw, g_onorm)
        out_dtype = BF16
    else:
        in_specs = [col(0), col(1), col(0), vec]
        args = (act, act, dec, lb_row)
        out_dtype = F32
    return pl.pallas_call(
        functools.partial(_hgrn_kernel, reverse=reverse, chunk=HG_CHUNK),
        grid=(B, ns),
        in_specs=in_specs,
        out_specs=plain,
        out_shape=jax.ShapeDtypeStruct((B, S, W), out_dtype),
        scratch_shapes=[pltpu.VMEM((HG_HEADS, HG_HEAD_DIM, HG_HEAD_DIM), F32)],
        compiler_params=_params("parallel", "arbitrary"),
        name="hgrn_bwd" if reverse else "hgrn_fwd",
    )(*args)


def _mla_prep_kernel(mla_ref, gqa_ref, gkva_ref, wq_ref, wk_ref, wv_ref,
                     cos_t_ref, sin_t_ref, cosk_ref, sina_ref, sinb_ref,
                     qt_ref, k_ref, vt_ref, ksq_ref, *, heads, scale):
    r = gqa_ref.shape[1]
    cq = _rms(mla_ref[0, :, 0:r], gqa_ref[...]).astype(BF16)
    ckv = _rms(mla_ref[0, :, r:2 * r], gkva_ref[...]).astype(BF16)
    kr = mla_ref[0, :, 2 * r:2 * r + QK_PAD]
    half = QK_ROPE // 2
    k_pe = (kr * cosk_ref[...]
            + pltpu.roll(kr, half, 1) * sina_ref[...]
            + pltpu.roll(kr, QK_PAD - half, 1) * sinb_ref[...])
    cos_t = cos_t_ref[...]
    sin_t = sin_t_ref[...]
    pad_rows = vt_ref.shape[2] - V_HEAD
    ones_rows = (lax.broadcasted_iota(jnp.int32, (pad_rows, cos_t.shape[1]), 0) == 0).astype(F32)
    k_sq_rows = []
    one_lane = (lax.broadcasted_iota(jnp.int32, (1, QK_PAD), 1) == QK_HEAD).astype(F32)
    for h in range(heads):
        q_t = lax.dot_general(wq_ref[h], cq, NT_DIMS, preferred_element_type=F32)
        x1 = q_t[QK_NOPE:QK_NOPE + half]
        x2 = q_t[QK_NOPE + half:QK_HEAD]
        q_rot = jnp.concatenate(
            [q_t[:QK_NOPE], x1 * cos_t - x2 * sin_t, x2 * cos_t + x1 * sin_t, q_t[QK_HEAD:]],
            axis=0)
        qt_ref[0, h] = (q_rot * scale).astype(qt_ref.dtype)
        k_b = (jnp.dot(ckv, wk_ref[h], preferred_element_type=F32) + k_pe).astype(BF16)
        k_f = k_b.astype(F32)
        k_sq = jnp.sum(k_f * k_f, axis=1, keepdims=True)
        k_sq_rows.append(jnp.broadcast_to(jnp.max(k_sq, axis=0, keepdims=True), (1, QK_PAD)))
        k_ref[0, h] = (k_f + one_lane).astype(k_ref.dtype)
        v_t = lax.dot_general(wv_ref[h], ckv, NT_DIMS, preferred_element_type=F32)
        vt_ref[0, h] = jnp.concatenate([v_t, ones_rows], axis=0).astype(vt_ref.dtype)
    ksq_ref[0, 0] = jnp.concatenate(k_sq_rows, axis=0)


def _mla_prep(mla, g_qa, g_kva, wq_t, wk, wv_t, tabs, *, ts):
    B, S, Wm = mla.shape
    H = wq_t.shape[0]
    cos_t, sin_t, cosk, sina, sinb = tabs
    half = QK_ROPE // 2
    c2 = lambda b, s: (0, 0)
    c3 = lambda b, s: (0, 0, 0)
    return pl.pallas_call(
        functools.partial(_mla_prep_kernel, heads=H, scale=QK_HEAD ** -0.5 * LOG2E),
        grid=(B, S // ts),
        in_specs=[
            pl.BlockSpec((1, ts, Wm), lambda b, s: (b, s, 0)),
            pl.BlockSpec(g_qa.shape, c2),
            pl.BlockSpec(g_kva.shape, c2),
            pl.BlockSpec(wq_t.shape, c3),
            pl.BlockSpec(wk.shape, c3),
            pl.BlockSpec(wv_t.shape, c3),
            pl.BlockSpec((half, ts), lambda b, s: (0, s)),
            pl.BlockSpec((half, ts), lambda b, s: (0, s)),
            pl.BlockSpec((ts, QK_PAD), lambda b, s: (s, 0)),
            pl.BlockSpec((ts, QK_PAD), lambda b, s: (s, 0)),
            pl.BlockSpec((ts, QK_PAD), lambda b, s: (s, 0)),
        ],
        out_specs=[
            pl.BlockSpec((1, H, QK_PAD, ts), lambda b, s: (b, 0, 0, s)),
            pl.BlockSpec((1, H, ts, QK_PAD), lambda b, s: (b, 0, s, 0)),
            pl.BlockSpec((1, H, V_AUG, ts), lambda b, s: (b, 0, 0, s)),
            pl.BlockSpec((1, 1, H, QK_PAD), lambda b, s: (b, s, 0, 0)),
        ],
        out_shape=[
            jax.ShapeDtypeStruct((B, H, QK_PAD, S), BF16),
            jax.ShapeDtypeStruct((B, H, S, QK_PAD), BF16),
            jax.ShapeDtypeStruct((B, H, V_AUG, S), BF16),
            jax.ShapeDtypeStruct((B, S // ts, H, QK_PAD), F32),
        ],
        compiler_params=_params("parallel", "parallel"),
        name="mla_prep",
    )(mla, g_qa, g_kva, wq_t, wk, wv_t, cos_t, sin_t, cosk, sina, sinb)


def _attn_kernel(qt_ref, k_ref, vt_ref, ksq_ref, o_ref, s_ref, acc_ref, m_ref, *, tk, tk_main):
    q_t = qt_ref[0, 0]
    nk = k_ref.shape[2] // tk

    q_f = q_t.astype(F32)
    q_sq = jnp.sum(q_f * q_f, axis=0, keepdims=True)
    k_sq_max = jnp.max(jnp.max(ksq_ref[0, 0], axis=1, keepdims=True), axis=0, keepdims=True)
    bound = jnp.sqrt(q_sq * k_sq_max) * BOUND_SLACK
    row = lax.broadcasted_iota(jnp.int32, q_f.shape, 0)
    q_off = jnp.where(row == QK_HEAD, -bound, q_f).astype(BF16)

    def bounded_step(j, carry):
        k0 = pl.multiple_of(j * tk_main, tk_main)
        s = jnp.dot(k_ref[0, 0, pl.ds(k0, tk_main), :], q_off, preferred_element_type=F32)
        p = jnp.exp2(s).astype(BF16)
        acc_ref[...] += jnp.dot(vt_ref[0, 0, :, pl.ds(k0, tk_main)], p,
                                preferred_element_type=F32)
        return carry

    acc_ref[...] = jnp.zeros_like(acc_ref)
    lax.fori_loop(0, k_ref.shape[2] // tk_main, bounded_step, 0)

    def scores(j, slot):
        k0 = pl.multiple_of(jnp.minimum(j, nk - 1) * tk, tk)
        s_ref[slot] = jnp.dot(k_ref[0, 0, pl.ds(k0, tk), :], q_t, preferred_element_type=F32)

    def softmax_pv(j, slot):
        s = s_ref[slot]
        m_old = m_ref[...]
        m_new = jnp.maximum(m_old, jnp.max(s, axis=0, keepdims=True))
        p = jnp.exp2(s - m_new).astype(BF16)
        k0 = pl.multiple_of(j * tk, tk)
        pv = jnp.dot(vt_ref[0, 0, :, pl.ds(k0, tk)], p, preferred_element_type=F32)
        acc_ref[...] = jnp.exp2(m_old - m_new) * acc_ref[...] + pv
        m_ref[...] = m_new

    def pair(i, carry):
        j = 2 * i
        scores(j + 1, 1)
        softmax_pv(j, 0)
        scores(j + 2, 0)
        softmax_pv(j + 1, 1)
        return carry

    denom_min = jnp.min(acc_ref[V_HEAD:V_HEAD + 1, :])

    @pl.when(jnp.logical_not(denom_min >= DENOM_FLOOR))
    def _():
        m_ref[...] = jnp.full_like(m_ref, -1e30)
        acc_ref[...] = jnp.zeros_like(acc_ref)
        scores(0, 0)
        lax.fori_loop(0, nk // 2, pair, 0)

    acc = acc_ref[...]
    o_ref[0, 0] = (acc[:V_HEAD] / acc[V_HEAD:V_HEAD + 1]).astype(o_ref.dtype)


def _attention(q_t, k, v_t, k_sq, *, tq, tk, tk_main):
    B, H, _, S = q_t.shape
    VA = v_t.shape[2]
    assert (S // tk) % 2 == 0
    return pl.pallas_call(
        functools.partial(_attn_kernel, tk=tk, tk_main=tk_main),
        grid=(B, H, S // tq),
        in_specs=[
            pl.BlockSpec((1, 1, QK_PAD, tq), lambda b, h, i: (b, h, 0, i)),
            pl.BlockSpec((1, 1, S, QK_PAD), lambda b, h, i: (b, h, 0, 0)),
            pl.BlockSpec((1, 1, VA, S), lambda b, h, i: (b, h, 0, 0)),
            pl.BlockSpec((1, 1) + k_sq.shape[2:], lambda b, h, i: (b, h, 0, 0)),
        ],
        out_specs=pl.BlockSpec((1, 1, V_HEAD, tq), lambda b, h, i: (b, h, 0, i)),
        out_shape=jax.ShapeDtypeStruct((B, H, V_HEAD, S), BF16),
        scratch_shapes=[pltpu.VMEM((2, tk, tq), F32), pltpu.VMEM((VA, tq), F32),
                        pltpu.VMEM((1, tq), F32)],
        compiler_params=_params("parallel", "parallel", "arbitrary"),
        name="attention",
    )(q_t, k, v_t, k_sq)


def _mix_ffn_kernel(x_ref, oh_ref, omt_ref, gt_ref, wb0_ref, wb1_ref, wo_ref,
                    gf_ref, wgu_ref, wd_ref, gfin_ref, y_ref):
    D = x_ref.shape[2]
    dff = wd_ref.shape[0]
    bp0 = jnp.dot(oh_ref[0], wb0_ref[...], preferred_element_type=F32)
    bp1 = lax.dot_general(omt_ref[0], wb1_ref[...], TN_DIMS, preferred_element_type=F32)
    gt = gt_ref[0].astype(F32)
    merged = gt[:, :D] * bp0 + gt[:, D:] * bp1
    x1 = x_ref[0] + jnp.dot(merged.astype(BF16), wo_ref[...], preferred_element_type=F32)
    h2 = _rms(x1, gf_ref[...]).astype(BF16)
    gu = jnp.dot(h2, wgu_ref[...], preferred_element_type=F32)
    gate = gu[:, :dff]
    act = (gate * _sigmoid(gate)) * gu[:, dff:]
    x2 = x1 + jnp.dot(act.astype(BF16), wd_ref[...], preferred_element_type=F32)
    y_ref[0] = _rms(x2, gfin_ref[...])


def _mix_ffn(x, o_h, o_mt, gt, wb0, wb1, wo, g_ffn, wgu, wd, g_final, *, tm):
    B, S, D = x.shape
    tok = lambda w: pl.BlockSpec((1, tm, w), lambda b, s: (b, s, 0))
    consts = (wb0, wb1, wo, g_ffn, wgu, wd, g_final)
    return pl.pallas_call(
        _mix_ffn_kernel,
        grid=(B, S // tm),
        in_specs=[
            tok(D), tok(o_h.shape[2]),
            pl.BlockSpec((1, o_mt.shape[1], tm), lambda b, s: (b, 0, s)),
            tok(gt.shape[2]),
        ] + [_resident(c.shape) for c in consts],
        out_specs=tok(D),
        out_shape=jax.ShapeDtypeStruct((B, S, D), F32),
        compiler_params=_params("parallel", "parallel"),
        name="mix_ffn",
    )(x, o_h, o_mt, gt, *consts)


def _rope_tables(S):
    d = QK_ROPE
    half = d // 2
    inv = ROPE_THETA ** (-jnp.arange(0, d, 2, dtype=F32) / d)
    ang = jnp.arange(S, dtype=F32)[:, None] * inv[None, :]
    cos, sin = jnp.cos(ang), jnp.sin(ang)
    z = lambda n: jnp.zeros((S, n), F32)
    tail = QK_PAD - QK_HEAD
    cosk = jnp.concatenate([z(QK_NOPE), cos, cos, z(tail)], axis=1)
    sina = jnp.concatenate([z(QK_NOPE + half), sin, z(tail)], axis=1)
    sinb = jnp.concatenate([z(QK_NOPE), -sin, z(half + tail)], axis=1)
    return cos.T, sin.T, cosk, sina, sinb


def _prep_weights(w_in, lb_param, w_uq, w_ukv, w_branch, w_out, w_gate_up, w_down):
    W = HG_WIDTH
    r_q, H = w_uq.shape[1], w_uq.shape[2]
    r_kv = w_ukv.shape[1]
    w = w_in[0]
    D = w.shape[0]
    c0 = 5 * W
    w_act = jnp.concatenate([w[:, :2 * W], w[:, 4 * W:c0]], axis=1)
    w_dec = w[:, 2 * W:4 * W]
    w_qa = w[:, c0:c0 + r_q]
    w_kva = w[:, c0 + r_q:c0 + r_q + r_kv]
    c1 = c0 + r_q + r_kv
    w_kr = w[:, c1:c1 + QK_ROPE]
    w_gt = w[:, c1 + QK_ROPE:]
    zpad = lambda n: jnp.zeros((D, n), w.dtype)
    w_mla = jnp.concatenate([w_qa, w_kva, zpad(QK_NOPE), w_kr, zpad(QK_PAD - QK_HEAD)], axis=1)
    wq = jnp.transpose(w_uq[0], (1, 2, 0))
    wq_t = jnp.concatenate([wq, jnp.zeros((H, QK_PAD - QK_HEAD, r_q), wq.dtype)], axis=1)
    wkv = jnp.transpose(w_ukv[0], (1, 0, 2))
    wk = jnp.concatenate([wkv[:, :, :QK_NOPE],
                          jnp.zeros((H, r_kv, QK_PAD - QK_NOPE), wkv.dtype)], axis=2)
    wv_t = jnp.transpose(wkv[:, :, QK_NOPE:], (0, 2, 1))
    lb = jax.nn.softmax(lb_param.astype(F32), axis=0)[0]
    b16 = lambda a: a.astype(BF16)
    return dict(
        w_act=b16(w_act), w_dec=b16(w_dec), w_mla=b16(w_mla), w_gt=b16(w_gt),
        wq_t=b16(wq_t), wk=b16(wk), wv_t=b16(wv_t),
        wb0=b16(w_branch[0, 0]), wb1=b16(w_branch[0, 1]), wo=b16(w_out[0]),
        wgu=b16(w_gate_up[0]), wd=b16(w_down[0]),
        lb_fw=lb[0:1], lb_bw=lb[1:2],
    )


def _pick(n, pref):
    t = min(n, pref)
    while n % t:
        t //= 2
    return t


def _trunk(x, p, g_mix, g_onorm, g_qa, g_kva, g_ffn, g_final):
    B, S, D = x.shape
    T = B * S
    tm = _pick(T, 256)
    act, dec, mla, gt = _in_proj(x.reshape(T, D), g_mix, p["w_act"], p["w_dec"], p["w_mla"],
                                 p["w_gt"], tm)
    act = act.reshape(B, S, -1)
    dec = dec.reshape(B, S, -1)
    mla = mla.reshape(B, S, -1)
    gt = gt.reshape(B, S, -1)

    ts = _pick(S, 512)
    o_fw = _hgrn_scan(act, dec, p["lb_fw"], None, None, reverse=False, ts=ts)
    o_h = _hgrn_scan(act, dec, p["lb_bw"], o_fw, g_onorm, reverse=True, ts=ts)

    tabs = _rope_tables(S)
    q_t, k, v_t, k_sq = _mla_prep(mla, g_qa, g_kva, p["wq_t"], p["wk"], p["wv_t"], tabs,
                                  ts=_pick(S, 512))
    k_sq = jnp.transpose(k_sq, (0, 2, 1, 3))
    o_mt = _attention(q_t, k, v_t, k_sq, tq=_pick(S, 1024), tk=_pick(S // 2, 512),
                      tk_main=_pick(S, 8192))
    o_mt = o_mt.reshape(B, -1, S)

    return _mix_ffn(x, o_h, o_mt, gt, p["wb0"], p["wb1"], p["wo"], g_ffn, p["wgu"], p["wd"],
                    g_final[None, :], tm=_pick(S, 256))


def kernel(x_prompt, x_sample, g_mix, w_in, lb_param, g_onorm, g_qa, w_uq, g_kva, w_ukv,
           w_branch, w_out, g_ffn, w_gate_up, w_down, g_final):
    p = _prep_weights(w_in, lb_param, w_uq, w_ukv, w_branch, w_out, w_gate_up, w_down)
    args = (p, g_mix, g_onorm, g_qa, g_kva, g_ffn, g_final)
    return (_trunk(x_prompt, *args), _trunk(x_sample, *args))
```

```python
import functools

import jax
import jax.numpy as jnp
import numpy as np
from jax import lax
from jax.experimental import pallas as pl
from jax.experimental.pallas import tpu as pltpu

F32 = jnp.float32
BF16 = jnp.bfloat16

EPS = 1e-6
ROPE_THETA = 10000.0
HG_HEADS = 4
HG_HEAD_DIM = 128
HG_WIDTH = HG_HEADS * HG_HEAD_DIM
HG_CHUNK = 64
QK_NOPE = 64
QK_ROPE = 32
V_HEAD = 64
QK_HEAD = QK_NOPE + QK_ROPE
QK_PAD = 128
V_AUG = V_HEAD + 16
LOG2E = 1.4426950408889634
BOUND_SLACK = 1.0 + 2.0 ** -6
DENOM_FLOOR = 2.0 ** -64
VMEM_LIMIT_BYTES = 56 * 1024 * 1024

NT_DIMS = (((1,), (1,)), ((), ()))
TN_DIMS = (((0,), (0,)), ((), ()))


def _sigmoid(x):
    return 1.0 / (1.0 + jnp.exp(-x))


def _rms(x, g):
    ms = jnp.mean(x * x, axis=-1, keepdims=True)
    return x * lax.rsqrt(ms + EPS) * g


def _params(*sem):
    return pltpu.CompilerParams(dimension_semantics=sem, vmem_limit_bytes=VMEM_LIMIT_BYTES)


def _hgrn_tile(q_of, v_of, f_of, lb, st_ref, emit, *, nc, chunk, reverse):
    C = chunk
    D = HG_HEAD_DIM
    heads = range(HG_HEADS)
    rows = lax.broadcasted_iota(jnp.int32, (C, C), 0)
    cols = lax.broadcasted_iota(jnp.int32, (C, C), 1)
    tri = (cols >= rows) if reverse else (cols <= rows)
    tri_b = tri.astype(BF16)
    mid = C // 2 if reverse else C // 2 - 1
    last = 0 if reverse else C - 1

    q_in, k_in, q_inter, k_state, v_b, decay = [], [], [], [], [], []
    for c in range(nc):
        f = lb + (1.0 - lb) * _sigmoid(f_of(c))
        g = jnp.log(f)
        kk = 1.0 - f
        g_hi = g.astype(BF16)
        g_lo = (g - g_hi.astype(F32)).astype(BF16)
        b = (jnp.dot(tri_b, g_hi, preferred_element_type=F32)
             + jnp.dot(tri_b, g_lo, preferred_element_type=F32))
        ref = b[mid:mid + 1, :]
        b_last = b[last:last + 1, :]
        qi = q_of(c) * jnp.exp(b - ref)
        ki = kk * jnp.exp(ref - b)
        q_in.append(qi.astype(BF16))
        k_in.append(ki.astype(BF16))
        q_inter.append((qi * jnp.exp(ref)).astype(BF16))
        k_state.append((ki * jnp.exp(b_last - ref)).astype(BF16))
        v_b.append(v_of(c))
        decay.append(jnp.exp(b_last))

    o_intra = [[None] * HG_HEADS for _ in range(nc)]
    incr = [[None] * HG_HEADS for _ in range(nc)]
    for c in range(nc):
        for h in heads:
            ls = slice(h * D, (h + 1) * D)
            att = lax.dot_general(q_in[c][:, ls], k_in[c][:, ls], NT_DIMS,
                                  preferred_element_type=F32)
            att = jnp.where(tri, att, 0.0).astype(BF16)
            o_intra[c][h] = jnp.dot(att, v_b[c][:, ls], preferred_element_type=F32)
            incr[c][h] = lax.dot_general(v_b[c][:, ls], k_state[c][:, ls], TN_DIMS,
                                         preferred_element_type=F32)

    st = [st_ref[h] for h in heads]
    for c in (range(nc - 1, -1, -1) if reverse else range(nc)):
        for h in heads:
            ls = slice(h * D, (h + 1) * D)
            o = o_intra[c][h] + lax.dot_general(q_inter[c][:, ls], st[h].astype(BF16), NT_DIMS,
                                                preferred_element_type=F32)
            st[h] = st[h] * decay[c][:, ls] + incr[c][h]
            emit(c, h, o)
    for h in heads:
        st_ref[h] = st[h]


def _resident(shape):
    zeros = (0,) * len(shape)
    return pl.BlockSpec(shape, lambda *_: zeros, pipeline_mode=pl.Buffered(1))


def _inproj_kernel(x_ref, g_ref, wact_ref, wdec_ref, wmla_ref, wgt_ref, lb_ref,
                   act_ref, fb_ref, mla_ref, gt_ref, ofw_ref, st_ref, *, chunk):
    W = HG_WIDTH
    C = chunk

    @pl.when(pl.program_id(1) == 0)
    def _():
        st_ref[...] = jnp.zeros_like(st_ref)

    h = _rms(x_ref[0], g_ref[...]).astype(BF16)
    a = jnp.dot(h, wact_ref[...], preferred_element_type=F32)
    q, i, g = a[:, :W], a[:, W:2 * W], a[:, 2 * W:]
    q_b = (q * _sigmoid(q)).astype(BF16)
    i_b = i.astype(BF16)
    act_ref[0] = jnp.concatenate([q_b, i_b, (g * _sigmoid(g)).astype(BF16)], axis=1)
    dec = jnp.dot(h, wdec_ref[...], preferred_element_type=F32)
    fb_ref[0] = dec[:, W:]
    mla_ref[0] = jnp.dot(h, wmla_ref[...], preferred_element_type=F32)
    gt_ref[0] = _sigmoid(jnp.dot(h, wgt_ref[...], preferred_element_type=F32)).astype(BF16)

    def emit(c, hd, o):
        ofw_ref[0, c * C:(c + 1) * C, hd * HG_HEAD_DIM:(hd + 1) * HG_HEAD_DIM] = o

    rs = lambda c: slice(c * C, (c + 1) * C)
    _hgrn_tile(lambda c: q_b[rs(c)].astype(F32), lambda c: i_b[rs(c)], lambda c: dec[rs(c), :W],
               lb_ref[...], st_ref, emit, nc=x_ref.shape[1] // C, chunk=C, reverse=False)


def _in_proj(x, g_mix, w_act, w_dec, w_mla, w_gt, lb_row, *, ts):
    B, S, D = x.shape
    W = HG_WIDTH
    tok = lambda w: pl.BlockSpec((1, ts, w), lambda b, s: (b, s, 0))
    consts = (g_mix, w_act, w_dec, w_mla, w_gt, lb_row)
    widths = (w_act.shape[1], W, w_mla.shape[1], w_gt.shape[1], W)
    dtypes = (BF16, F32, F32, BF16, F32)
    return pl.pallas_call(
        functools.partial(_inproj_kernel, chunk=HG_CHUNK),
        grid=(B, S // ts),
        in_specs=[tok(D)] + [_resident(c.shape) for c in consts],
        out_specs=[tok(w) for w in widths],
        out_shape=[jax.ShapeDtypeStruct((B, S, w), dt) for w, dt in zip(widths, dtypes)],
        scratch_shapes=[pltpu.VMEM((HG_HEADS, HG_HEAD_DIM, HG_HEAD_DIM), F32)],
        compiler_params=_params("parallel", "arbitrary"),
        name="in_proj",
    )(x, *consts)


def _hgrn_bwd_kernel(q_ref, v_ref, f_ref, lb_ref, og_ref, ofw_ref, gon_ref, o_ref, st_ref, *,
                     chunk):
    C = chunk

    @pl.when(pl.program_id(1) == 0)
    def _():
        st_ref[...] = jnp.zeros_like(st_ref)

    def emit(c, hd, o):
        rs = slice(c * C, (c + 1) * C)
        ls = slice(hd * HG_HEAD_DIM, (hd + 1) * HG_HEAD_DIM)
        o = o + ofw_ref[0, rs, ls]
        o = _rms(o, gon_ref[:, ls]) * og_ref[0, rs, ls].astype(F32)
        o_ref[0, rs, ls] = o.astype(o_ref.dtype)

    rs = lambda c: slice(c * C, (c + 1) * C)
    _hgrn_tile(lambda c: q_ref[0, rs(c), :].astype(F32), lambda c: v_ref[0, rs(c), :],
               lambda c: f_ref[0, rs(c), :], lb_ref[...], st_ref, emit,
               nc=q_ref.shape[1] // C, chunk=C, reverse=True)


def _hgrn_bwd(act, fb, lb_row, o_fw, g_onorm, *, ts):
    B, S, W = fb.shape
    ns = S // ts
    col = lambda j: pl.BlockSpec((1, ts, W), lambda b, s: (b, ns - 1 - s, j))
    vec = pl.BlockSpec((1, W), lambda b, s: (0, 0))
    return pl.pallas_call(
        functools.partial(_hgrn_bwd_kernel, chunk=HG_CHUNK),
        grid=(B, ns),
        in_specs=[col(0), col(1), col(0), vec, col(2), col(0), vec],
        out_specs=col(0),
        out_shape=jax.ShapeDtypeStruct((B, S, W), BF16),
        scratch_shapes=[pltpu.VMEM((HG_HEADS, HG_HEAD_DIM, HG_HEAD_DIM), F32)],
        compiler_params=_params("parallel", "arbitrary"),
        name="hgrn_bwd",
    )(act, act, fb, lb_row, act, o_fw, g_onorm)


def _mla_prep_kernel(mla_ref, gqa_ref, gkva_ref, wq_ref, wk_ref, wv_ref,
                     cos_t_ref, sin_t_ref, cosk_ref, sina_ref, sinb_ref,
                     qt_ref, k_ref, vt_ref, ksq_ref, *, heads, scale):
    r = gqa_ref.shape[1]
    cq = _rms(mla_ref[0, :, 0:r], gqa_ref[...]).astype(BF16)
    ckv = _rms(mla_ref[0, :, r:2 * r], gkva_ref[...]).astype(BF16)
    kr = mla_ref[0, :, 2 * r:2 * r + QK_PAD]
    half = QK_ROPE // 2
    k_pe = (kr * cosk_ref[...]
            + pltpu.roll(kr, half, 1) * sina_ref[...]
            + pltpu.roll(kr, QK_PAD - half, 1) * sinb_ref[...])
    cos_t = cos_t_ref[...]
    sin_t = sin_t_ref[...]
    pad_rows = vt_ref.shape[2] - V_HEAD
    ones_rows = (lax.broadcasted_iota(jnp.int32, (pad_rows, cos_t.shape[1]), 0) == 0).astype(F32)
    k_sq_rows = []
    one_lane = (lax.broadcasted_iota(jnp.int32, (1, QK_PAD), 1) == QK_HEAD).astype(F32)
    for h in range(heads):
        q_t = lax.dot_general(wq_ref[h], cq, NT_DIMS, preferred_element_type=F32)
        x1 = q_t[QK_NOPE:QK_NOPE + half]
        x2 = q_t[QK_NOPE + half:QK_HEAD]
        q_rot = jnp.concatenate(
            [q_t[:QK_NOPE], x1 * cos_t - x2 * sin_t, x2 * cos_t + x1 * sin_t, q_t[QK_HEAD:]],
            axis=0)
        qt_ref[0, h] = (q_rot * scale).astype(qt_ref.dtype)
        k_b = (jnp.dot(ckv, wk_ref[h], preferred_element_type=F32) + k_pe).astype(BF16)
        k_f = k_b.astype(F32)
        k_sq = jnp.sum(k_f * k_f, axis=1, keepdims=True)
        k_sq_rows.append(jnp.broadcast_to(jnp.max(k_sq, axis=0, keepdims=True), (1, QK_PAD)))
        k_ref[0, h] = (k_f + one_lane).astype(k_ref.dtype)
        v_t = lax.dot_general(wv_ref[h], ckv, NT_DIMS, preferred_element_type=F32)
        vt_ref[0, h] = jnp.concatenate([v_t, ones_rows], axis=0).astype(vt_ref.dtype)
    ksq_ref[0, 0] = jnp.concatenate(k_sq_rows, axis=0)


def _mla_prep(mla, g_qa, g_kva, wq_t, wk, wv_t, tabs, *, ts):
    B, S, Wm = mla.shape
    H = wq_t.shape[0]
    cos_t, sin_t, cosk, sina, sinb = tabs
    half = QK_ROPE // 2
    c2 = lambda b, s: (0, 0)
    c3 = lambda b, s: (0, 0, 0)
    return pl.pallas_call(
        functools.partial(_mla_prep_kernel, heads=H, scale=QK_HEAD ** -0.5 * LOG2E),
        grid=(B, S // ts),
        in_specs=[
            pl.BlockSpec((1, ts, Wm), lambda b, s: (b, s, 0)),
            pl.BlockSpec(g_qa.shape, c2),
            pl.BlockSpec(g_kva.shape, c2),
            pl.BlockSpec(wq_t.shape, c3),
            pl.BlockSpec(wk.shape, c3),
            pl.BlockSpec(wv_t.shape, c3),
            pl.BlockSpec((half, ts), lambda b, s: (0, s)),
            pl.BlockSpec((half, ts), lambda b, s: (0, s)),
            pl.BlockSpec((ts, QK_PAD), lambda b, s: (s, 0)),
            pl.BlockSpec((ts, QK_PAD), lambda b, s: (s, 0)),
            pl.BlockSpec((ts, QK_PAD), lambda b, s: (s, 0)),
        ],
        out_specs=[
            pl.BlockSpec((1, H, QK_PAD, ts), lambda b, s: (b, 0, 0, s)),
            pl.BlockSpec((1, H, ts, QK_PAD), lambda b, s: (b, 0, s, 0)),
            pl.BlockSpec((1, H, V_AUG, ts), lambda b, s: (b, 0, 0, s)),
            pl.BlockSpec((1, 1, H, QK_PAD), lambda b, s: (b, s, 0, 0)),
        ],
        out_shape=[
            jax.ShapeDtypeStruct((B, H, QK_PAD, S), BF16),
            jax.ShapeDtypeStruct((B, H, S, QK_PAD), BF16),
            jax.ShapeDtypeStruct((B, H, V_AUG, S), BF16),
            jax.ShapeDtypeStruct((B, S // ts, H, QK_PAD), F32),
        ],
        compiler_params=_params("parallel", "parallel"),
        name="mla_prep",
    )(mla, g_qa, g_kva, wq_t, wk, wv_t, cos_t, sin_t, cosk, sina, sinb)


def _attn_kernel(qt_ref, k_ref, vt_ref, ksq_ref, o_ref, s_ref, acc_ref, m_ref, *, tk, tk_main):
    q_t = qt_ref[0, 0]
    nk = k_ref.shape[2] // tk

    q_f = q_t.astype(F32)
    q_sq = jnp.sum(q_f * q_f, axis=0, keepdims=True)
    k_sq_max = jnp.max(jnp.max(ksq_ref[0, 0], axis=1, keepdims=True), axis=0, keepdims=True)
    bound = jnp.sqrt(q_sq * k_sq_max) * BOUND_SLACK
    row = lax.broadcasted_iota(jnp.int32, q_f.shape, 0)
    q_off = jnp.where(row == QK_HEAD, -bound, q_f).astype(BF16)

    def bounded_step(j, carry):
        k0 = pl.multiple_of(j * tk_main, tk_main)
        s = jnp.dot(k_ref[0, 0, pl.ds(k0, tk_main), :], q_off, preferred_element_type=F32)
        p = jnp.exp2(s).astype(BF16)
        acc_ref[...] += jnp.dot(vt_ref[0, 0, :, pl.ds(k0, tk_main)], p,
                                preferred_element_type=F32)
        return carry

    acc_ref[...] = jnp.zeros_like(acc_ref)
    lax.fori_loop(0, k_ref.shape[2] // tk_main, bounded_step, 0)

    def scores(j, slot):
        k0 = pl.multiple_of(jnp.minimum(j, nk - 1) * tk, tk)
        s_ref[slot] = jnp.dot(k_ref[0, 0, pl.ds(k0, tk), :], q_t, preferred_element_type=F32)

    def softmax_pv(j, slot):
        s = s_ref[slot]
        m_old = m_ref[...]
        m_new = jnp.maximum(m_old, jnp.max(s, axis=0, keepdims=True))
        p = jnp.exp2(s - m_new).astype(BF16)
        k0 = pl.multiple_of(j * tk, tk)
        pv = jnp.dot(vt_ref[0, 0, :, pl.ds(k0, tk)], p, preferred_element_type=F32)
        acc_ref[...] = jnp.exp2(m_old - m_new) * acc_ref[...] + pv
        m_ref[...] = m_new

    def pair(i, carry):
        j = 2 * i
        scores(j + 1, 1)
        softmax_pv(j, 0)
        scores(j + 2, 0)
        softmax_pv(j + 1, 1)
        return carry

    denom_min = jnp.min(acc_ref[V_HEAD:V_HEAD + 1, :])

    @pl.when(jnp.logical_not(denom_min >= DENOM_FLOOR))
    def _():
        m_ref[...] = jnp.full_like(m_ref, -1e30)
        acc_ref[...] = jnp.zeros_like(acc_ref)
        scores(0, 0)
        lax.fori_loop(0, nk // 2, pair, 0)

    acc = acc_ref[...]
    o_ref[0, 0] = (acc[:V_HEAD] / acc[V_HEAD:V_HEAD + 1]).astype(o_ref.dtype)


def _attention(q_t, k, v_t, k_sq, *, tq, tk, tk_main):
    B, H, _, S = q_t.shape
    VA = v_t.shape[2]
    assert (S // tk) % 2 == 0
    return pl.pallas_call(
        functools.partial(_attn_kernel, tk=tk, tk_main=tk_main),
        grid=(B, H, S // tq),
        in_specs=[
            pl.BlockSpec((1, 1, QK_PAD, tq), lambda b, h, i: (b, h, 0, i)),
            pl.BlockSpec((1, 1, S, QK_PAD), lambda b, h, i: (b, h, 0, 0)),
            pl.BlockSpec((1, 1, VA, S), lambda b, h, i: (b, h, 0, 0)),
            pl.BlockSpec((1, 1) + k_sq.shape[2:], lambda b, h, i: (b, h, 0, 0)),
        ],
        out_specs=pl.BlockSpec((1, 1, V_HEAD, tq), lambda b, h, i: (b, h, 0, i)),
        out_shape=jax.ShapeDtypeStruct((B, H, V_HEAD, S), BF16),
        scratch_shapes=[pltpu.VMEM((2, tk, tq), F32), pltpu.VMEM((VA, tq), F32),
                        pltpu.VMEM((1, tq), F32)],
        compiler_params=_params("parallel", "parallel", "arbitrary"),
        name="attention",
    )(q_t, k, v_t, k_sq)


def _mix_ffn_kernel(x_ref, oh_ref, omt_ref, gt_ref, wb0_ref, wb1_ref, wo_ref,
                    gf_ref, wgu_ref, wd_ref, gfin_ref, y_ref):
    D = x_ref.shape[2]
    dff = wd_ref.shape[0]
    bp0 = jnp.dot(oh_ref[0], wb0_ref[...], preferred_element_type=F32)
    bp1 = lax.dot_general(omt_ref[0], wb1_ref[...], TN_DIMS, preferred_element_type=F32)
    gt = gt_ref[0].astype(F32)
    merged = gt[:, :D] * bp0 + gt[:, D:] * bp1
    x1 = x_ref[0] + jnp.dot(merged.astype(BF16), wo_ref[...], preferred_element_type=F32)
    h2 = _rms(x1, gf_ref[...]).astype(BF16)
    gu = jnp.dot(h2, wgu_ref[...], preferred_element_type=F32)
    gate = gu[:, :dff]
    act = (gate * _sigmoid(gate)) * gu[:, dff:]
    x2 = x1 + jnp.dot(act.astype(BF16), wd_ref[...], preferred_element_type=F32)
    y_ref[0] = _rms(x2, gfin_ref[...])


def _mix_ffn(x, o_h, o_mt, gt, wb0, wb1, wo, g_ffn, wgu, wd, g_final, *, tm):
    B, S, D = x.shape
    tok = lambda w: pl.BlockSpec((1, tm, w), lambda b, s: (b, s, 0))
    consts = (wb0, wb1, wo, g_ffn, wgu, wd, g_final)
    return pl.pallas_call(
        _mix_ffn_kernel,
        grid=(B, S // tm),
        in_specs=[
            tok(D), tok(o_h.shape[2]),
            pl.BlockSpec((1, o_mt.shape[1], tm), lambda b, s: (b, 0, s)),
            tok(gt.shape[2]),
        ] + [_resident(c.shape) for c in consts],
        out_specs=tok(D),
        out_shape=jax.ShapeDtypeStruct((B, S, D), F32),
        compiler_params=_params("parallel", "parallel"),
        name="mix_ffn",
    )(x, o_h, o_mt, gt, *consts)


def _rope_tables(S):
    d = QK_ROPE
    half = d // 2
    inv = ROPE_THETA ** (-jnp.arange(0, d, 2, dtype=F32) / d)
    ang = jnp.arange(S, dtype=F32)[:, None] * inv[None, :]
    cos, sin = jnp.cos(ang), jnp.sin(ang)
    z = lambda n: jnp.zeros((S, n), F32)
    tail = QK_PAD - QK_HEAD
    cosk = jnp.concatenate([z(QK_NOPE), cos, cos, z(tail)], axis=1)
    sina = jnp.concatenate([z(QK_NOPE + half), sin, z(tail)], axis=1)
    sinb = jnp.concatenate([z(QK_NOPE), -sin, z(half + tail)], axis=1)
    return cos.T, sin.T, cosk, sina, sinb


def _prep_weights(w_in, lb_param, w_uq, w_ukv, w_branch, w_out, w_gate_up, w_down):
    W = HG_WIDTH
    r_q, H = w_uq.shape[1], w_uq.shape[2]
    r_kv = w_ukv.shape[1]
    w = w_in[0]
    D = w.shape[0]
    c0 = 5 * W
    w_act = jnp.concatenate([w[:, :2 * W], w[:, 4 * W:c0]], axis=1)
    w_dec = w[:, 2 * W:4 * W]
    w_qa = w[:, c0:c0 + r_q]
    w_kva = w[:, c0 + r_q:c0 + r_q + r_kv]
    c1 = c0 + r_q + r_kv
    w_kr = w[:, c1:c1 + QK_ROPE]
    w_gt = w[:, c1 + QK_ROPE:]
    zpad = lambda n: jnp.zeros((D, n), w.dtype)
    w_mla = jnp.concatenate([w_qa, w_kva, zpad(QK_NOPE), w_kr, zpad(QK_PAD - QK_HEAD)], axis=1)
    wq = jnp.transpose(w_uq[0], (1, 2, 0))
    wq_t = jnp.concatenate([wq, jnp.zeros((H, QK_PAD - QK_HEAD, r_q), wq.dtype)], axis=1)
    wkv = jnp.transpose(w_ukv[0], (1, 0, 2))
    wk = jnp.concatenate([wkv[:, :, :QK_NOPE],
                          jnp.zeros((H, r_kv, QK_PAD - QK_NOPE), wkv.dtype)], axis=2)
    wv_t = jnp.transpose(wkv[:, :, QK_NOPE:], (0, 2, 1))
    lb = jax.nn.softmax(lb_param.astype(F32), axis=0)[0]
    b16 = lambda a: a.astype(BF16)
    return dict(
        w_act=b16(w_act), w_dec=b16(w_dec), w_mla=b16(w_mla), w_gt=b16(w_gt),
        wq_t=b16(wq_t), wk=b16(wk), wv_t=b16(wv_t),
        wb0=b16(w_branch[0, 0]), wb1=b16(w_branch[0, 1]), wo=b16(w_out[0]),
        wgu=b16(w_gate_up[0]), wd=b16(w_down[0]),
        lb_fw=lb[0:1], lb_bw=lb[1:2],
    )


def _pick(n, pref):
    t = min(n, pref)
    while n % t:
        t //= 2
    return t


def _trunk(x, p, g_mix, g_onorm, g_qa, g_kva, g_ffn, g_final):
    B, S, D = x.shape
    ts = _pick(S, 512)
    act, fb, mla, gt, o_fw = _in_proj(x, g_mix, p["w_act"], p["w_dec"], p["w_mla"], p["w_gt"],
                                      p["lb_fw"], ts=ts)
    o_h = _hgrn_bwd(act, fb, p["lb_bw"], o_fw, g_onorm, ts=ts)

    tabs = _rope_tables(S)
    q_t, k, v_t, k_sq = _mla_prep(mla, g_qa, g_kva, p["wq_t"], p["wk"], p["wv_t"], tabs,
                                  ts=_pick(S, 512))
    k_sq = jnp.transpose(k_sq, (0, 2, 1, 3))
    o_mt = _attention(q_t, k, v_t, k_sq, tq=_pick(S, 1024), tk=_pick(S // 2, 512),
                      tk_main=_pick(S, 8192))
    o_mt = o_mt.reshape(B, -1, S)

    return _mix_ffn(x, o_h, o_mt, gt, p["wb0"], p["wb1"], p["wo"], g_ffn, p["wgu"], p["wd"],
                    g_final[None, :], tm=_pick(S, 256))


def kernel(x_prompt, x_sample, g_mix, w_in, lb_param, g_onorm, g_qa, w_uq, g_kva, w_ukv,
           w_branch, w_out, g_ffn, w_gate_up, w_down, g_final):
    p = _prep_weights(w_in, lb_param, w_uq, w_ukv, w_branch, w_out, w_gate_up, w_down)
    args = (p, g_mix, g_onorm, g_qa, g_kva, g_ffn, g_final)
    return (_trunk(x_prompt, *args), _trunk(x_sample, *args))
```

```python
import functools

import jax
import jax.numpy as jnp
from jax import lax
from jax.experimental import pallas as pl
from jax.experimental.pallas import tpu as pltpu

F32 = jnp.float32
BF16 = jnp.bfloat16

EPS = 1e-6
ROPE_THETA = 10000.0
HG_HEADS = 4
HG_HEAD_DIM = 128
HG_WIDTH = HG_HEADS * HG_HEAD_DIM
HG_CHUNK = 64
QK_NOPE = 64
QK_ROPE = 32
V_HEAD = 64
QK_HEAD = QK_NOPE + QK_ROPE
QK_PAD = 128
V_AUG = V_HEAD + 16
LOG2E = 1.4426950408889634
BOUND_SLACK = 1.0 + 2.0 ** -6
DENOM_FLOOR = 2.0 ** -64
VMEM_LIMIT_BYTES = 56 * 1024 * 1024

NT_DIMS = (((1,), (1,)), ((), ()))
TN_DIMS = (((0,), (0,)), ((), ()))


def _sigmoid(x):
    return 1.0 / (1.0 + jnp.exp(-x))


def _rms(x, g):
    ms = jnp.mean(x * x, axis=-1, keepdims=True)
    return x * lax.rsqrt(ms + EPS) * g


def _params(*sem):
    return pltpu.CompilerParams(dimension_semantics=sem, vmem_limit_bytes=VMEM_LIMIT_BYTES)


def _resident(shape):
    zeros = (0,) * len(shape)
    return pl.BlockSpec(shape, lambda *_: zeros, pipeline_mode=pl.Buffered(1))


def _hgrn_tile(q_of, v_of, f_of, lb, st_ref, emit, *, nc, chunk, reverse):
    C = chunk
    D = HG_HEAD_DIM
    heads = range(HG_HEADS)
    rows = lax.broadcasted_iota(jnp.int32, (C, C), 0)
    cols = lax.broadcasted_iota(jnp.int32, (C, C), 1)
    tri = (cols >= rows) if reverse else (cols <= rows)
    tri_b = tri.astype(BF16)
    mid = C // 2 if reverse else C // 2 - 1
    last = 0 if reverse else C - 1

    q_in, k_in, q_inter, k_state, v_b, decay = [], [], [], [], [], []
    for c in range(nc):
        f = lb + (1.0 - lb) * _sigmoid(f_of(c))
        g = jnp.log(f)
        kk = 1.0 - f
        g_hi = g.astype(BF16)
        g_lo = (g - g_hi.astype(F32)).astype(BF16)
        b = (jnp.dot(tri_b, g_hi, preferred_element_type=F32)
             + jnp.dot(tri_b, g_lo, preferred_element_type=F32))
        ref = b[mid:mid + 1, :]
        b_last = b[last:last + 1, :]
        qi = q_of(c) * jnp.exp(b - ref)
        ki = kk * jnp.exp(ref - b)
        q_in.append(qi.astype(BF16))
        k_in.append(ki.astype(BF16))
        q_inter.append((qi * jnp.exp(ref)).astype(BF16))
        k_state.append((ki * jnp.exp(b_last - ref)).astype(BF16))
        v_b.append(v_of(c))
        decay.append(jnp.exp(b_last))

    o_intra = [[None] * HG_HEADS for _ in range(nc)]
    incr = [[None] * HG_HEADS for _ in range(nc)]
    for c in range(nc):
        for h in heads:
            ls = slice(h * D, (h + 1) * D)
            att = lax.dot_general(q_in[c][:, ls], k_in[c][:, ls], NT_DIMS,
                                  preferred_element_type=F32)
            att = jnp.where(tri, att, 0.0).astype(BF16)
            o_intra[c][h] = jnp.dot(att, v_b[c][:, ls], preferred_element_type=F32)
            incr[c][h] = lax.dot_general(v_b[c][:, ls], k_state[c][:, ls], TN_DIMS,
                                         preferred_element_type=F32)

    st = [st_ref[h] for h in heads]
    for c in (range(nc - 1, -1, -1) if reverse else range(nc)):
        for h in heads:
            ls = slice(h * D, (h + 1) * D)
            o = o_intra[c][h] + lax.dot_general(q_inter[c][:, ls], st[h].astype(BF16), NT_DIMS,
                                                preferred_element_type=F32)
            st[h] = st[h] * decay[c][:, ls] + incr[c][h]
            emit(c, h, o)
    for h in heads:
        st_ref[h] = st[h]


def _mla_tile(mla, gqa_ref, gkva_ref, wq_ref, wk_ref, wv_ref,
              cos_t_ref, sin_t_ref, cosk_ref, sina_ref, sinb_ref,
              qt_ref, k_ref, vt_ref, ksq_ref):
    heads = wq_ref.shape[0]
    scale = QK_HEAD ** -0.5 * LOG2E
    r = gqa_ref.shape[1]
    cq = _rms(mla[:, 0:r], gqa_ref[...]).astype(BF16)
    ckv = _rms(mla[:, r:2 * r], gkva_ref[...]).astype(BF16)
    kr = mla[:, 2 * r:2 * r + QK_PAD]
    half = QK_ROPE // 2
    k_pe = (kr * cosk_ref[...]
            + pltpu.roll(kr, half, 1) * sina_ref[...]
            + pltpu.roll(kr, QK_PAD - half, 1) * sinb_ref[...])
    cos_t = cos_t_ref[...]
    sin_t = sin_t_ref[...]
    pad_rows = vt_ref.shape[2] - V_HEAD
    ones_rows = (lax.broadcasted_iota(jnp.int32, (pad_rows, cos_t.shape[1]), 0) == 0).astype(F32)
    k_sq_rows = []
    one_lane = (lax.broadcasted_iota(jnp.int32, (1, QK_PAD), 1) == QK_HEAD).astype(F32)
    for h in range(heads):
        q_t = lax.dot_general(wq_ref[h], cq, NT_DIMS, preferred_element_type=F32)
        x1 = q_t[QK_NOPE:QK_NOPE + half]
        x2 = q_t[QK_NOPE + half:QK_HEAD]
        q_rot = jnp.concatenate(
            [q_t[:QK_NOPE], x1 * cos_t - x2 * sin_t, x2 * cos_t + x1 * sin_t, q_t[QK_HEAD:]],
            axis=0)
        qt_ref[0, h] = (q_rot * scale).astype(qt_ref.dtype)
        k_b = (jnp.dot(ckv, wk_ref[h], preferred_element_type=F32) + k_pe).astype(BF16)
        k_f = k_b.astype(F32)
        k_sq = jnp.sum(k_f * k_f, axis=1, keepdims=True)
        k_sq_rows.append(jnp.broadcast_to(jnp.max(k_sq, axis=0, keepdims=True), (1, QK_PAD)))
        k_ref[0, h] = (k_f + one_lane).astype(k_ref.dtype)
        v_t = lax.dot_general(wv_ref[h], ckv, NT_DIMS, preferred_element_type=F32)
        vt_ref[0, h] = jnp.concatenate([v_t, ones_rows], axis=0).astype(vt_ref.dtype)
    ksq_ref[0, 0] = jnp.concatenate(k_sq_rows, axis=0)


def _front_kernel(x_ref, g_ref, wact_ref, wdec_ref, wmla_ref, wgt_ref, lb_ref,
                  gqa_ref, gkva_ref, wq_ref, wk_ref, wv_ref,
                  cos_t_ref, sin_t_ref, cosk_ref, sina_ref, sinb_ref,
                  act_ref, fb_ref, gt_ref, ofw_ref, qt_ref, k_ref, vt_ref, ksq_ref, st_ref, *, chunk):
    W = HG_WIDTH
    C = chunk

    @pl.when(pl.program_id(1) == 0)
    def _():
        st_ref[...] = jnp.zeros_like(st_ref)

    h = _rms(x_ref[0], g_ref[...]).astype(BF16)
    a = jnp.dot(h, wact_ref[...], preferred_element_type=F32)
    q, i, g = a[:, :W], a[:, W:2 * W], a[:, 2 * W:]
    q_b = (q * _sigmoid(q)).astype(BF16)
    i_b = i.astype(BF16)
    act_ref[0] = jnp.concatenate([q_b, i_b, (g * _sigmoid(g)).astype(BF16)], axis=1)
    dec = jnp.dot(h, wdec_ref[...], preferred_element_type=F32)
    fb_ref[0] = dec[:, W:]
    gt_ref[0] = _sigmoid(jnp.dot(h, wgt_ref[...], preferred_element_type=F32)).astype(BF16)

    _mla_tile(jnp.dot(h, wmla_ref[...], preferred_element_type=F32), gqa_ref, gkva_ref,
              wq_ref, wk_ref, wv_ref, cos_t_ref, sin_t_ref, cosk_ref, sina_ref, sinb_ref,
              qt_ref, k_ref, vt_ref, ksq_ref)

    def emit(c, hd, o):
        ofw_ref[0, c * C:(c + 1) * C, hd * HG_HEAD_DIM:(hd + 1) * HG_HEAD_DIM] = o

    rs = lambda c: slice(c * C, (c + 1) * C)
    _hgrn_tile(lambda c: q_b[rs(c)].astype(F32), lambda c: i_b[rs(c)], lambda c: dec[rs(c), :W],
               lb_ref[...], st_ref, emit, nc=x_ref.shape[1] // C, chunk=C, reverse=False)


def _front(x, p, g_mix, g_qa, g_kva, tabs, *, ts):
    B, S, D = x.shape
    W = HG_WIDTH
    H = p["wq_t"].shape[0]
    half = QK_ROPE // 2
    tok = lambda w: pl.BlockSpec((1, ts, w), lambda b, s: (b, s, 0))
    consts = (g_mix, p["w_act"], p["w_dec"], p["w_mla"], p["w_gt"], p["lb_fw"],
              g_qa, g_kva, p["wq_t"], p["wk"], p["wv_t"])
    lane_tab = pl.BlockSpec((half, ts), lambda b, s: (0, s))
    row_tab = pl.BlockSpec((ts, QK_PAD), lambda b, s: (s, 0))
    return pl.pallas_call(
        functools.partial(_front_kernel, chunk=HG_CHUNK),
        grid=(B, S // ts),
        in_specs=([tok(D)] + [_resident(c.shape) for c in consts]
                  + [lane_tab, lane_tab, row_tab, row_tab, row_tab]),
        out_specs=[
            tok(3 * W), tok(W), tok(p["w_gt"].shape[1]), tok(W),
            pl.BlockSpec((1, H, QK_PAD, ts), lambda b, s: (b, 0, 0, s)),
            pl.BlockSpec((1, H, ts, QK_PAD), lambda b, s: (b, 0, s, 0)),
            pl.BlockSpec((1, H, V_AUG, ts), lambda b, s: (b, 0, 0, s)),
            pl.BlockSpec((1, 1, H, QK_PAD), lambda b, s: (b, s, 0, 0)),
        ],
        out_shape=[
            jax.ShapeDtypeStruct((B, S, 3 * W), BF16),
            jax.ShapeDtypeStruct((B, S, W), F32),
            jax.ShapeDtypeStruct((B, S, p["w_gt"].shape[1]), BF16),
            jax.ShapeDtypeStruct((B, S, W), F32),
            jax.ShapeDtypeStruct((B, H, QK_PAD, S), BF16),
            jax.ShapeDtypeStruct((B, H, S, QK_PAD), BF16),
            jax.ShapeDtypeStruct((B, H, V_AUG, S), BF16),
            jax.ShapeDtypeStruct((B, S // ts, H, QK_PAD), F32),
        ],
        scratch_shapes=[pltpu.VMEM((HG_HEADS, HG_HEAD_DIM, HG_HEAD_DIM), F32)],
        compiler_params=_params("parallel", "arbitrary"),
        name="front",
    )(x, *consts, *tabs)


def _hgrn_bwd_kernel(q_ref, v_ref, f_ref, lb_ref, og_ref, ofw_ref, gon_ref, o_ref, st_ref, *,
                     chunk):
    C = chunk

    @pl.when(pl.program_id(1) == 0)
    def _():
        st_ref[...] = jnp.zeros_like(st_ref)

    def emit(c, hd, o):
        rs = slice(c * C, (c + 1) * C)
        ls = slice(hd * HG_HEAD_DIM, (hd + 1) * HG_HEAD_DIM)
        o = o + ofw_ref[0, rs, ls]
        o = _rms(o, gon_ref[:, ls]) * og_ref[0, rs, ls].astype(F32)
        o_ref[0, rs, ls] = o.astype(o_ref.dtype)

    rs = lambda c: slice(c * C, (c + 1) * C)
    _hgrn_tile(lambda c: q_ref[0, rs(c), :].astype(F32), lambda c: v_ref[0, rs(c), :],
               lambda c: f_ref[0, rs(c), :], lb_ref[...], st_ref, emit,
               nc=q_ref.shape[1] // C, chunk=C, reverse=True)


def _hgrn_bwd(act, fb, lb_row, o_fw, g_onorm, *, ts):
    B, S, W = fb.shape
    ns = S // ts
    col = lambda j: pl.BlockSpec((1, ts, W), lambda b, s: (b, ns - 1 - s, j))
    vec = pl.BlockSpec((1, W), lambda b, s: (0, 0))
    return pl.pallas_call(
        functools.partial(_hgrn_bwd_kernel, chunk=HG_CHUNK),
        grid=(B, ns),
        in_specs=[col(0), col(1), col(0), vec, col(2), col(0), vec],
        out_specs=col(0),
        out_shape=jax.ShapeDtypeStruct((B, S, W), BF16),
        scratch_shapes=[pltpu.VMEM((HG_HEADS, HG_HEAD_DIM, HG_HEAD_DIM), F32)],
        compiler_params=_params("parallel", "arbitrary"),
        name="hgrn_bwd",
    )(act, act, fb, lb_row, act, o_fw, g_onorm)


def _attn_kernel(qt_ref, k_ref, vt_ref, ksq_ref, o_ref, s_ref, acc_ref, m_ref, *, tk, tk_main):
    q_t = qt_ref[0, 0]
    nk = k_ref.shape[2] // tk

    q_f = q_t.astype(F32)
    q_sq = jnp.sum(q_f * q_f, axis=0, keepdims=True)
    k_sq_max = jnp.max(jnp.max(ksq_ref[0, 0], axis=1, keepdims=True), axis=0, keepdims=True)
    bound = jnp.sqrt(q_sq * k_sq_max) * BOUND_SLACK
    row = lax.broadcasted_iota(jnp.int32, q_f.shape, 0)
    q_off = jnp.where(row == QK_HEAD, -bound, q_f).astype(BF16)

    def bounded_step(j, carry):
        k0 = pl.multiple_of(j * tk_main, tk_main)
        s = jnp.dot(k_ref[0, 0, pl.ds(k0, tk_main), :], q_off, preferred_element_type=F32)
        p = jnp.exp2(s).astype(BF16)
        acc_ref[...] += jnp.dot(vt_ref[0, 0, :, pl.ds(k0, tk_main)], p,
                                preferred_element_type=F32)
        return carry

    acc_ref[...] = jnp.zeros_like(acc_ref)
    lax.fori_loop(0, k_ref.shape[2] // tk_main, bounded_step, 0)

    def scores(j, slot):
        k0 = pl.multiple_of(jnp.minimum(j, nk - 1) * tk, tk)
        s_ref[slot] = jnp.dot(k_ref[0, 0, pl.ds(k0, tk), :], q_t, preferred_element_type=F32)

    def softmax_pv(j, slot):
        s = s_ref[slot]
        m_old = m_ref[...]
        m_new = jnp.maximum(m_old, jnp.max(s, axis=0, keepdims=True))
        p = jnp.exp2(s - m_new).astype(BF16)
        k0 = pl.multiple_of(j * tk, tk)
        pv = jnp.dot(vt_ref[0, 0, :, pl.ds(k0, tk)], p, preferred_element_type=F32)
        acc_ref[...] = jnp.exp2(m_old - m_new) * acc_ref[...] + pv
        m_ref[...] = m_new

    def pair(i, carry):
        j = 2 * i
        scores(j + 1, 1)
        softmax_pv(j, 0)
        scores(j + 2, 0)
        softmax_pv(j + 1, 1)
        return carry

    denom_min = jnp.min(acc_ref[V_HEAD:V_HEAD + 1, :])

    @pl.when(jnp.logical_not(denom_min >= DENOM_FLOOR))
    def _():
        m_ref[...] = jnp.full_like(m_ref, -1e30)
        acc_ref[...] = jnp.zeros_like(acc_ref)
        scores(0, 0)
        lax.fori_loop(0, nk // 2, pair, 0)

    acc = acc_ref[...]
    o_ref[0, 0] = (acc[:V_HEAD] / acc[V_HEAD:V_HEAD + 1]).astype(o_ref.dtype)


def _attention(q_t, k, v_t, k_sq, *, tq, tk, tk_main):
    B, H, _, S = q_t.shape
    VA = v_t.shape[2]
    assert (S // tk) % 2 == 0
    return pl.pallas_call(
        functools.partial(_attn_kernel, tk=tk, tk_main=tk_main),
        grid=(B, H, S // tq),
        in_specs=[
            pl.BlockSpec((1, 1, QK_PAD, tq), lambda b, h, i: (b, h, 0, i)),
            pl.BlockSpec((1, 1, S, QK_PAD), lambda b, h, i: (b, h, 0, 0)),
            pl.BlockSpec((1, 1, VA, S), lambda b, h, i: (b, h, 0, 0)),
            pl.BlockSpec((1, 1) + k_sq.shape[2:], lambda b, h, i: (b, h, 0, 0)),
        ],
        out_specs=pl.BlockSpec((1, 1, V_HEAD, tq), lambda b, h, i: (b, h, 0, i)),
        out_shape=jax.ShapeDtypeStruct((B, H, V_HEAD, S), BF16),
        scratch_shapes=[pltpu.VMEM((2, tk, tq), F32), pltpu.VMEM((VA, tq), F32),
                        pltpu.VMEM((1, tq), F32)],
        compiler_params=_params("parallel", "parallel", "arbitrary"),
        name="attention",
    )(q_t, k, v_t, k_sq)


def _mix_ffn_kernel(x_ref, oh_ref, omt_ref, gt_ref, wb0_ref, wb1_ref, wo_ref,
                    gf_ref, wgu_ref, wd_ref, gfin_ref, y_ref):
    D = x_ref.shape[2]
    dff = wd_ref.shape[0]
    bp0 = jnp.dot(oh_ref[0], wb0_ref[...], preferred_element_type=F32)
    bp1 = lax.dot_general(omt_ref[0], wb1_ref[...], TN_DIMS, preferred_element_type=F32)
    gt = gt_ref[0].astype(F32)
    merged = gt[:, :D] * bp0 + gt[:, D:] * bp1
    x1 = x_ref[0] + jnp.dot(merged.astype(BF16), wo_ref[...], preferred_element_type=F32)
    h2 = _rms(x1, gf_ref[...]).astype(BF16)
    gu = jnp.dot(h2, wgu_ref[...], preferred_element_type=F32)
    gate = gu[:, :dff]
    act = (gate * _sigmoid(gate)) * gu[:, dff:]
    x2 = x1 + jnp.dot(act.astype(BF16), wd_ref[...], preferred_element_type=F32)
    y_ref[0] = _rms(x2, gfin_ref[...])


def _mix_ffn(x, o_h, o_mt, gt, wb0, wb1, wo, g_ffn, wgu, wd, g_final, *, tm):
    B, S, D = x.shape
    tok = lambda w: pl.BlockSpec((1, tm, w), lambda b, s: (b, s, 0))
    consts = (wb0, wb1, wo, g_ffn, wgu, wd, g_final)
    return pl.pallas_call(
        _mix_ffn_kernel,
        grid=(B, S // tm),
        in_specs=[
            tok(D), tok(o_h.shape[2]),
            pl.BlockSpec((1, o_mt.shape[1], tm), lambda b, s: (b, 0, s)),
            tok(gt.shape[2]),
        ] + [_resident(c.shape) for c in consts],
        out_specs=tok(D),
        out_shape=jax.ShapeDtypeStruct((B, S, D), F32),
        compiler_params=_params("parallel", "parallel"),
        name="mix_ffn",
    )(x, o_h, o_mt, gt, *consts)


def _rope_tables(S):
    d = QK_ROPE
    half = d // 2
    inv = ROPE_THETA ** (-jnp.arange(0, d, 2, dtype=F32) / d)
    ang = jnp.arange(S, dtype=F32)[:, None] * inv[None, :]
    cos, sin = jnp.cos(ang), jnp.sin(ang)
    z = lambda n: jnp.zeros((S, n), F32)
    tail = QK_PAD - QK_HEAD
    cosk = jnp.concatenate([z(QK_NOPE), cos, cos, z(tail)], axis=1)
    sina = jnp.concatenate([z(QK_NOPE + half), sin, z(tail)], axis=1)
    sinb = jnp.concatenate([z(QK_NOPE), -sin, z(half + tail)], axis=1)
    return cos.T, sin.T, cosk, sina, sinb


def _prep_weights(w_in, lb_param, w_uq, w_ukv, w_branch, w_out, w_gate_up, w_down):
    W = HG_WIDTH
    r_q, H = w_uq.shape[1], w_uq.shape[2]
    r_kv = w_ukv.shape[1]
    w = w_in[0]
    D = w.shape[0]
    c0 = 5 * W
    w_act = jnp.concatenate([w[:, :2 * W], w[:, 4 * W:c0]], axis=1)
    w_dec = w[:, 2 * W:4 * W]
    w_qa = w[:, c0:c0 + r_q]
    w_kva = w[:, c0 + r_q:c0 + r_q + r_kv]
    c1 = c0 + r_q + r_kv
    w_kr = w[:, c1:c1 + QK_ROPE]
    w_gt = w[:, c1 + QK_ROPE:]
    zpad = lambda n: jnp.zeros((D, n), w.dtype)
    w_mla = jnp.concatenate([w_qa, w_kva, zpad(QK_NOPE), w_kr, zpad(QK_PAD - QK_HEAD)], axis=1)
    wq = jnp.transpose(w_uq[0], (1, 2, 0))
    wq_t = jnp.concatenate([wq, jnp.zeros((H, QK_PAD - QK_HEAD, r_q), wq.dtype)], axis=1)
    wkv = jnp.transpose(w_ukv[0], (1, 0, 2))
    wk = jnp.concatenate([wkv[:, :, :QK_NOPE],
                          jnp.zeros((H, r_kv, QK_PAD - QK_NOPE), wkv.dtype)], axis=2)
    wv_t = jnp.transpose(wkv[:, :, QK_NOPE:], (0, 2, 1))
    lb = jax.nn.softmax(lb_param.astype(F32), axis=0)[0]
    b16 = lambda a: a.astype(BF16)
    return dict(
        w_act=b16(w_act), w_dec=b16(w_dec), w_mla=b16(w_mla), w_gt=b16(w_gt),
        wq_t=b16(wq_t), wk=b16(wk), wv_t=b16(wv_t),
        wb0=b16(w_branch[0, 0]), wb1=b16(w_branch[0, 1]), wo=b16(w_out[0]),
        wgu=b16(w_gate_up[0]), wd=b16(w_down[0]),
        lb_fw=lb[0:1], lb_bw=lb[1:2],
    )


def _pick(n, pref):
    t = min(n, pref)
    while n % t:
        t //= 2
    return t


def _trunk(x, p, g_mix, g_onorm, g_qa, g_kva, g_ffn, g_final):
    B, S, D = x.shape
    ts = _pick(S, 512)
    act, fb, gt, o_fw, q_t, k, v_t, k_sq = _front(x, p, g_mix, g_qa, g_kva, _rope_tables(S), ts=ts)
    o_h = _hgrn_bwd(act, fb, p["lb_bw"], o_fw, g_onorm, ts=ts)
    k_sq = jnp.transpose(k_sq, (0, 2, 1, 3))
    o_mt = _attention(q_t, k, v_t, k_sq, tq=_pick(S, 1024), tk=_pick(S // 2, 512),
                      tk_main=_pick(S, 8192))
    o_mt = o_mt.reshape(B, -1, S)

    return _mix_ffn(x, o_h, o_mt, gt, p["wb0"], p["wb1"], p["wo"], g_ffn, p["wgu"], p["wd"],
                    g_final[None, :], tm=_pick(S, 512))


def kernel(x_prompt, x_sample, g_mix, w_in, lb_param, g_onorm, g_qa, w_uq, g_kva, w_ukv,
           w_branch, w_out, g_ffn, w_gate_up, w_down, g_final):
    p = _prep_weights(w_in, lb_param, w_uq, w_ukv, w_branch, w_out, w_gate_up, w_down)
    args = (p, g_mix, g_onorm, g_qa, g_kva, g_ffn, g_final)
    return (_trunk(x_prompt, *args), _trunk(x_sample, *args))
```

```python
import functools

import jax
import jax.numpy as jnp
from jax import lax
from jax.experimental import pallas as pl
from jax.experimental.pallas import tpu as pltpu

F32 = jnp.float32
BF16 = jnp.bfloat16

EPS = 1e-6
ROPE_THETA = 10000.0
HG_HEADS = 4
HG_HEAD_DIM = 128
HG_WIDTH = HG_HEADS * HG_HEAD_DIM
HG_CHUNK = 64
QK_NOPE = 64
QK_ROPE = 32
V_HEAD = 64
QK_HEAD = QK_NOPE + QK_ROPE
QK_PAD = 128
V_AUG = V_HEAD + 16
LOG2E = 1.4426950408889634
BOUND_SLACK = 1.0 + 2.0 ** -6
DENOM_FLOOR = 2.0 ** -64
VMEM_LIMIT_BYTES = 56 * 1024 * 1024

NT_DIMS = (((1,), (1,)), ((), ()))
TN_DIMS = (((0,), (0,)), ((), ()))


def _sigmoid(x):
    return 1.0 / (1.0 + jnp.exp(-x))


def _rms(x, g):
    ms = jnp.mean(x * x, axis=-1, keepdims=True)
    return x * lax.rsqrt(ms + EPS) * g


def _params(*sem):
    return pltpu.CompilerParams(dimension_semantics=sem, vmem_limit_bytes=VMEM_LIMIT_BYTES)


def _resident(shape):
    zeros = (0,) * len(shape)
    return pl.BlockSpec(shape, lambda *_: zeros, pipeline_mode=pl.Buffered(1))


def _hgrn_tile(q_of, v_of, f_of, lb, st_ref, emit, *, nc, chunk, reverse):
    C = chunk
    D = HG_HEAD_DIM
    heads = range(HG_HEADS)
    rows = lax.broadcasted_iota(jnp.int32, (C, C), 0)
    cols = lax.broadcasted_iota(jnp.int32, (C, C), 1)
    tri = (cols >= rows) if reverse else (cols <= rows)
    tri_b = tri.astype(BF16)
    mid = C // 2 if reverse else C // 2 - 1
    last = 0 if reverse else C - 1

    q_in, k_in, q_inter, k_state, v_b, decay = [], [], [], [], [], []
    for c in range(nc):
        f = lb + (1.0 - lb) * _sigmoid(f_of(c))
        g = jnp.log(f)
        kk = 1.0 - f
        g_hi = g.astype(BF16)
        g_lo = (g - g_hi.astype(F32)).astype(BF16)
        b = (jnp.dot(tri_b, g_hi, preferred_element_type=F32)
             + jnp.dot(tri_b, g_lo, preferred_element_type=F32))
        ref = b[mid:mid + 1, :]
        b_last = b[last:last + 1, :]
        qi = q_of(c) * jnp.exp(b - ref)
        ki = kk * jnp.exp(ref - b)
        q_in.append(qi.astype(BF16))
        k_in.append(ki.astype(BF16))
        q_inter.append((qi * jnp.exp(ref)).astype(BF16))
        k_state.append((ki * jnp.exp(b_last - ref)).astype(BF16))
        v_b.append(v_of(c))
        decay.append(jnp.exp(b_last))

    o_intra = [[None] * HG_HEADS for _ in range(nc)]
    incr = [[None] * HG_HEADS for _ in range(nc)]
    for c in range(nc):
        for h in heads:
            ls = slice(h * D, (h + 1) * D)
            att = lax.dot_general(q_in[c][:, ls], k_in[c][:, ls], NT_DIMS,
                                  preferred_element_type=F32)
            att = jnp.where(tri, att, 0.0).astype(BF16)
            o_intra[c][h] = jnp.dot(att, v_b[c][:, ls], preferred_element_type=F32)
            incr[c][h] = lax.dot_general(v_b[c][:, ls], k_state[c][:, ls], TN_DIMS,
                                         preferred_element_type=F32)

    st = [st_ref[h] for h in heads]
    for c in (range(nc - 1, -1, -1) if reverse else range(nc)):
        for h in heads:
            ls = slice(h * D, (h + 1) * D)
            o = o_intra[c][h] + lax.dot_general(q_inter[c][:, ls], st[h].astype(BF16), NT_DIMS,
                                                preferred_element_type=F32)
            st[h] = st[h] * decay[c][:, ls] + incr[c][h]
            emit(c, h, o)
    for h in heads:
        st_ref[h] = st[h]


def _mla_tile(mla, gqa_ref, gkva_ref, wq_ref, wk_ref, wv_ref,
              cos_t_ref, sin_t_ref, cosk_ref, sina_ref, sinb_ref,
              qt_ref, k_ref, vt_ref, ksq_ref):
    heads = wq_ref.shape[0]
    scale = QK_HEAD ** -0.5 * LOG2E
    r = gqa_ref.shape[1]
    cq = _rms(mla[:, 0:r], gqa_ref[...]).astype(BF16)
    ckv = _rms(mla[:, r:2 * r], gkva_ref[...]).astype(BF16)
    kr = mla[:, 2 * r:2 * r + QK_PAD]
    half = QK_ROPE // 2
    k_pe = (kr * cosk_ref[...]
            + pltpu.roll(kr, half, 1) * sina_ref[...]
            + pltpu.roll(kr, QK_PAD - half, 1) * sinb_ref[...])
    cos_t = cos_t_ref[...]
    sin_t = sin_t_ref[...]
    pad_rows = vt_ref.shape[2] - V_HEAD
    ones_rows = (lax.broadcasted_iota(jnp.int32, (pad_rows, cos_t.shape[1]), 0) == 0).astype(F32)
    k_sq_rows = []
    one_lane = (lax.broadcasted_iota(jnp.int32, (1, QK_PAD), 1) == QK_HEAD).astype(F32)
    for h in range(heads):
        q_t = lax.dot_general(wq_ref[h], cq, NT_DIMS, preferred_element_type=F32)
        x1 = q_t[QK_NOPE:QK_NOPE + half]
        x2 = q_t[QK_NOPE + half:QK_HEAD]
        q_rot = jnp.concatenate(
            [q_t[:QK_NOPE], x1 * cos_t - x2 * sin_t, x2 * cos_t + x1 * sin_t, q_t[QK_HEAD:]],
            axis=0)
        qt_ref[0, h] = (q_rot * scale).astype(qt_ref.dtype)
        k_b = (jnp.dot(ckv, wk_ref[h], preferred_element_type=F32) + k_pe).astype(BF16)
        k_f = k_b.astype(F32)
        k_sq = jnp.sum(k_f * k_f, axis=1, keepdims=True)
        k_sq_rows.append(jnp.broadcast_to(jnp.max(k_sq, axis=0, keepdims=True), (1, QK_PAD)))
        k_ref[0, h] = (k_f + one_lane).astype(k_ref.dtype)
        v_t = lax.dot_general(wv_ref[h], ckv, NT_DIMS, preferred_element_type=F32)
        vt_ref[0, h] = jnp.concatenate([v_t, ones_rows], axis=0).astype(vt_ref.dtype)
    ksq_ref[0, 0] = jnp.concatenate(k_sq_rows, axis=0)


def _front_kernel(x_ref, g_ref, wact_ref, wdec_ref, wmla_ref, wgt_ref, lb_ref,
                  gqa_ref, gkva_ref, wq_ref, wk_ref, wv_ref,
                  cos_t_ref, sin_t_ref, cosk_ref, sina_ref, sinb_ref,
                  act_ref, fb_ref, gt_ref, ofw_ref, qt_ref, k_ref, vt_ref, ksq_ref, st_ref, *, chunk):
    W = HG_WIDTH
    C = chunk

    @pl.when(pl.program_id(1) == 0)
    def _():
        st_ref[...] = jnp.zeros_like(st_ref)

    h = _rms(x_ref[0], g_ref[...]).astype(BF16)
    a = jnp.dot(h, wact_ref[...], preferred_element_type=F32)
    q, i, g = a[:, :W], a[:, W:2 * W], a[:, 2 * W:]
    q_b = (q * _sigmoid(q)).astype(BF16)
    i_b = i.astype(BF16)
    act_ref[0] = jnp.concatenate([q_b, i_b, (g * _sigmoid(g)).astype(BF16)], axis=1)
    dec = jnp.dot(h, wdec_ref[...], preferred_element_type=F32)
    fb_ref[0] = dec[:, W:]
    gt_ref[0] = _sigmoid(jnp.dot(h, wgt_ref[...], preferred_element_type=F32)).astype(BF16)

    _mla_tile(jnp.dot(h, wmla_ref[...], preferred_element_type=F32), gqa_ref, gkva_ref,
              wq_ref, wk_ref, wv_ref, cos_t_ref, sin_t_ref, cosk_ref, sina_ref, sinb_ref,
              qt_ref, k_ref, vt_ref, ksq_ref)

    def emit(c, hd, o):
        ofw_ref[0, c * C:(c + 1) * C, hd * HG_HEAD_DIM:(hd + 1) * HG_HEAD_DIM] = o

    rs = lambda c: slice(c * C, (c + 1) * C)
    _hgrn_tile(lambda c: q_b[rs(c)].astype(F32), lambda c: i_b[rs(c)], lambda c: dec[rs(c), :W],
               lb_ref[...], st_ref, emit, nc=x_ref.shape[1] // C, chunk=C, reverse=False)


def _front(x, p, g_mix, g_qa, g_kva, tabs, *, ts):
    B, S, D = x.shape
    W = HG_WIDTH
    H = p["wq_t"].shape[0]
    half = QK_ROPE // 2
    tok = lambda w: pl.BlockSpec((1, ts, w), lambda b, s: (b, s, 0))
    consts = (g_mix, p["w_act"], p["w_dec"], p["w_mla"], p["w_gt"], p["lb_fw"],
              g_qa, g_kva, p["wq_t"], p["wk"], p["wv_t"])
    lane_tab = pl.BlockSpec((half, ts), lambda b, s: (0, s))
    row_tab = pl.BlockSpec((ts, QK_PAD), lambda b, s: (s, 0))
    return pl.pallas_call(
        functools.partial(_front_kernel, chunk=HG_CHUNK),
        grid=(B, S // ts),
        in_specs=([tok(D)] + [_resident(c.shape) for c in consts]
                  + [lane_tab, lane_tab, row_tab, row_tab, row_tab]),
        out_specs=[
            tok(3 * W), tok(W), tok(p["w_gt"].shape[1]), tok(W),
            pl.BlockSpec((1, H, QK_PAD, ts), lambda b, s: (b, 0, 0, s)),
            pl.BlockSpec((1, H, ts, QK_PAD), lambda b, s: (b, 0, s, 0)),
            pl.BlockSpec((1, H, V_AUG, ts), lambda b, s: (b, 0, 0, s)),
            pl.BlockSpec((1, 1, H, QK_PAD), lambda b, s: (b, s, 0, 0)),
        ],
        out_shape=[
            jax.ShapeDtypeStruct((B, S, 3 * W), BF16),
            jax.ShapeDtypeStruct((B, S, W), F32),
            jax.ShapeDtypeStruct((B, S, p["w_gt"].shape[1]), BF16),
            jax.ShapeDtypeStruct((B, S, W), F32),
            jax.ShapeDtypeStruct((B, H, QK_PAD, S), BF16),
            jax.ShapeDtypeStruct((B, H, S, QK_PAD), BF16),
            jax.ShapeDtypeStruct((B, H, V_AUG, S), BF16),
            jax.ShapeDtypeStruct((B, S // ts, H, QK_PAD), F32),
        ],
        scratch_shapes=[pltpu.VMEM((HG_HEADS, HG_HEAD_DIM, HG_HEAD_DIM), F32)],
        compiler_params=_params("parallel", "arbitrary"),
        name="front",
    )(x, *consts, *tabs)


def _attn_kernel(qt_ref, k_ref, vt_ref, ksq_ref, o_ref, s_ref, acc_ref, m_ref, *, tk, tk_main):
    q_t = qt_ref[0, 0]
    nk = k_ref.shape[2] // tk

    q_f = q_t.astype(F32)
    q_sq = jnp.sum(q_f * q_f, axis=0, keepdims=True)
    k_sq_max = jnp.max(jnp.max(ksq_ref[0, 0], axis=1, keepdims=True), axis=0, keepdims=True)
    bound = jnp.sqrt(q_sq * k_sq_max) * BOUND_SLACK
    row = lax.broadcasted_iota(jnp.int32, q_f.shape, 0)
    q_off = jnp.where(row == QK_HEAD, -bound, q_f).astype(BF16)

    def bounded_step(j, carry):
        k0 = pl.multiple_of(j * tk_main, tk_main)
        s = jnp.dot(k_ref[0, 0, pl.ds(k0, tk_main), :], q_off, preferred_element_type=F32)
        p = jnp.exp2(s).astype(BF16)
        acc_ref[...] += jnp.dot(vt_ref[0, 0, :, pl.ds(k0, tk_main)], p,
                                preferred_element_type=F32)
        return carry

    acc_ref[...] = jnp.zeros_like(acc_ref)
    lax.fori_loop(0, k_ref.shape[2] // tk_main, bounded_step, 0)

    def scores(j, slot):
        k0 = pl.multiple_of(jnp.minimum(j, nk - 1) * tk, tk)
        s_ref[slot] = jnp.dot(k_ref[0, 0, pl.ds(k0, tk), :], q_t, preferred_element_type=F32)

    def softmax_pv(j, slot):
        s = s_ref[slot]
        m_old = m_ref[...]
        m_new = jnp.maximum(m_old, jnp.max(s, axis=0, keepdims=True))
        p = jnp.exp2(s - m_new).astype(BF16)
        k0 = pl.multiple_of(j * tk, tk)
        pv = jnp.dot(vt_ref[0, 0, :, pl.ds(k0, tk)], p, preferred_element_type=F32)
        acc_ref[...] = jnp.exp2(m_old - m_new) * acc_ref[...] + pv
        m_ref[...] = m_new

    def pair(i, carry):
        j = 2 * i
        scores(j + 1, 1)
        softmax_pv(j, 0)
        scores(j + 2, 0)
        softmax_pv(j + 1, 1)
        return carry

    denom_min = jnp.min(acc_ref[V_HEAD:V_HEAD + 1, :])

    @pl.when(jnp.logical_not(denom_min >= DENOM_FLOOR))
    def _():
        m_ref[...] = jnp.full_like(m_ref, -1e30)
        acc_ref[...] = jnp.zeros_like(acc_ref)
        scores(0, 0)
        lax.fori_loop(0, nk // 2, pair, 0)

    acc = acc_ref[...]
    o_ref[0, 0] = (acc[:V_HEAD] / acc[V_HEAD:V_HEAD + 1]).astype(o_ref.dtype)


def _attention(q_t, k, v_t, k_sq, *, tq, tk, tk_main):
    B, H, _, S = q_t.shape
    VA = v_t.shape[2]
    assert (S // tk) % 2 == 0
    return pl.pallas_call(
        functools.partial(_attn_kernel, tk=tk, tk_main=tk_main),
        grid=(B, H, S // tq),
        in_specs=[
            pl.BlockSpec((1, 1, QK_PAD, tq), lambda b, h, i: (b, h, 0, i)),
            pl.BlockSpec((1, 1, S, QK_PAD), lambda b, h, i: (b, h, 0, 0)),
            pl.BlockSpec((1, 1, VA, S), lambda b, h, i: (b, h, 0, 0)),
            pl.BlockSpec((1, 1) + k_sq.shape[2:], lambda b, h, i: (b, h, 0, 0)),
        ],
        out_specs=pl.BlockSpec((1, 1, V_HEAD, tq), lambda b, h, i: (b, h, 0, i)),
        out_shape=jax.ShapeDtypeStruct((B, H, V_HEAD, S), BF16),
        scratch_shapes=[pltpu.VMEM((2, tk, tq), F32), pltpu.VMEM((VA, tq), F32),
                        pltpu.VMEM((1, tq), F32)],
        compiler_params=_params("parallel", "parallel", "arbitrary"),
        name="attention",
    )(q_t, k, v_t, k_sq)


def _back_kernel(x_ref, q_ref, v_ref, og_ref, f_ref, ofw_ref, omt_ref, gt_ref,
                 lb_ref, gon_ref, wb0_ref, wb1_ref, wo_ref, gf_ref, wgu_ref, wd_ref, gfin_ref,
                 y_ref, st_ref, oh_ref, *, chunk):
    C = chunk
    D = x_ref.shape[2]
    dff = wd_ref.shape[0]

    @pl.when(pl.program_id(1) == 0)
    def _():
        st_ref[...] = jnp.zeros_like(st_ref)

    def emit(c, hd, o):
        rs = slice(c * C, (c + 1) * C)
        ls = slice(hd * HG_HEAD_DIM, (hd + 1) * HG_HEAD_DIM)
        o = o + ofw_ref[0, rs, ls]
        o = _rms(o, gon_ref[:, ls]) * og_ref[0, rs, ls].astype(F32)
        oh_ref[rs, ls] = o.astype(oh_ref.dtype)

    rs = lambda c: slice(c * C, (c + 1) * C)
    _hgrn_tile(lambda c: q_ref[0, rs(c), :].astype(F32), lambda c: v_ref[0, rs(c), :],
               lambda c: f_ref[0, rs(c), :], lb_ref[...], st_ref, emit,
               nc=q_ref.shape[1] // C, chunk=C, reverse=True)

    bp0 = jnp.dot(oh_ref[...], wb0_ref[...], preferred_element_type=F32)
    bp1 = lax.dot_general(omt_ref[0], wb1_ref[...], TN_DIMS, preferred_element_type=F32)
    gt = gt_ref[0].astype(F32)
    merged = gt[:, :D] * bp0 + gt[:, D:] * bp1
    x1 = x_ref[0] + jnp.dot(merged.astype(BF16), wo_ref[...], preferred_element_type=F32)
    h2 = _rms(x1, gf_ref[...]).astype(BF16)
    gu = jnp.dot(h2, wgu_ref[...], preferred_element_type=F32)
    gate = gu[:, :dff]
    act = (gate * _sigmoid(gate)) * gu[:, dff:]
    x2 = x1 + jnp.dot(act.astype(BF16), wd_ref[...], preferred_element_type=F32)
    y_ref[0] = _rms(x2, gfin_ref[...])


def _back(x, act, fb, o_fw, o_mt, gt, p, g_onorm, g_ffn, g_final, *, ts):
    B, S, D = x.shape
    W = HG_WIDTH
    ns = S // ts
    tok = lambda w, j=0: pl.BlockSpec((1, ts, w), lambda b, s: (b, ns - 1 - s, j))
    consts = (p["lb_bw"], g_onorm, p["wb0"], p["wb1"], p["wo"], g_ffn, p["wgu"], p["wd"], g_final)
    return pl.pallas_call(
        functools.partial(_back_kernel, chunk=HG_CHUNK),
        grid=(B, ns),
        in_specs=[
            tok(D), tok(W, 0), tok(W, 1), tok(W, 2), tok(W), tok(W),
            pl.BlockSpec((1, o_mt.shape[1], ts), lambda b, s: (b, 0, ns - 1 - s)),
            tok(gt.shape[2]),
        ] + [_resident(c.shape) for c in consts],
        out_specs=tok(D),
        out_shape=jax.ShapeDtypeStruct((B, S, D), F32),
        scratch_shapes=[pltpu.VMEM((HG_HEADS, HG_HEAD_DIM, HG_HEAD_DIM), F32),
                        pltpu.VMEM((ts, W), BF16)],
        compiler_params=_params("parallel", "arbitrary"),
        name="back",
    )(x, act, act, act, fb, o_fw, o_mt, gt, *consts)


def _rope_tables(S):
    d = QK_ROPE
    half = d // 2
    inv = ROPE_THETA ** (-jnp.arange(0, d, 2, dtype=F32) / d)
    ang = jnp.arange(S, dtype=F32)[:, None] * inv[None, :]
    cos, sin = jnp.cos(ang), jnp.sin(ang)
    z = lambda n: jnp.zeros((S, n), F32)
    tail = QK_PAD - QK_HEAD
    cosk = jnp.concatenate([z(QK_NOPE), cos, cos, z(tail)], axis=1)
    sina = jnp.concatenate([z(QK_NOPE + half), sin, z(tail)], axis=1)
    sinb = jnp.concatenate([z(QK_NOPE), -sin, z(half + tail)], axis=1)
    return cos.T, sin.T, cosk, sina, sinb


def _prep_weights(w_in, lb_param, w_uq, w_ukv, w_branch, w_out, w_gate_up, w_down):
    W = HG_WIDTH
    r_q, H = w_uq.shape[1], w_uq.shape[2]
    r_kv = w_ukv.shape[1]
    w = w_in[0]
    D = w.shape[0]
    c0 = 5 * W
    w_act = jnp.concatenate([w[:, :2 * W], w[:, 4 * W:c0]], axis=1)
    w_dec = w[:, 2 * W:4 * W]
    w_qa = w[:, c0:c0 + r_q]
    w_kva = w[:, c0 + r_q:c0 + r_q + r_kv]
    c1 = c0 + r_q + r_kv
    w_kr = w[:, c1:c1 + QK_ROPE]
    w_gt = w[:, c1 + QK_ROPE:]
    zpad = lambda n: jnp.zeros((D, n), w.dtype)
    w_mla = jnp.concatenate([w_qa, w_kva, zpad(QK_NOPE), w_kr, zpad(QK_PAD - QK_HEAD)], axis=1)
    wq = jnp.transpose(w_uq[0], (1, 2, 0))
    wq_t = jnp.concatenate([wq, jnp.zeros((H, QK_PAD - QK_HEAD, r_q), wq.dtype)], axis=1)
    wkv = jnp.transpose(w_ukv[0], (1, 0, 2))
    wk = jnp.concatenate([wkv[:, :, :QK_NOPE],
                          jnp.zeros((H, r_kv, QK_PAD - QK_NOPE), wkv.dtype)], axis=2)
    wv_t = jnp.transpose(wkv[:, :, QK_NOPE:], (0, 2, 1))
    lb = jax.nn.softmax(lb_param.astype(F32), axis=0)[0]
    b16 = lambda a: a.astype(BF16)
    return dict(
        w_act=b16(w_act), w_dec=b16(w_dec), w_mla=b16(w_mla), w_gt=b16(w_gt),
        wq_t=b16(wq_t), wk=b16(wk), wv_t=b16(wv_t),
        wb0=b16(w_branch[0, 0]), wb1=b16(w_branch[0, 1]), wo=b16(w_out[0]),
        wgu=b16(w_gate_up[0]), wd=b16(w_down[0]),
        lb_fw=lb[0:1], lb_bw=lb[1:2],
    )


def _pick(n, pref):
    t = min(n, pref)
    while n % t:
        t //= 2
    return t


def _trunk(x, p, g_mix, g_onorm, g_qa, g_kva, g_ffn, g_final):
    B, S, D = x.shape
    ts = _pick(S, 512)
    act, fb, gt, o_fw, q_t, k, v_t, k_sq = _front(x, p, g_mix, g_qa, g_kva, _rope_tables(S), ts=ts)
    k_sq = jnp.transpose(k_sq, (0, 2, 1, 3))
    o_mt = _attention(q_t, k, v_t, k_sq, tq=_pick(S, 1024), tk=_pick(S // 2, 512),
                      tk_main=_pick(S, 8192))
    o_mt = o_mt.reshape(B, -1, S)
    return _back(x, act, fb, o_fw, o_mt, gt, p, g_onorm, g_ffn, g_final[None, :], ts=ts)


def kernel(x_prompt, x_sample, g_mix, w_in, lb_param, g_onorm, g_qa, w_uq, g_kva, w_ukv,
           w_branch, w_out, g_ffn, w_gate_up, w_down, g_final):
    p = _prep_weights(w_in, lb_param, w_uq, w_ukv, w_branch, w_out, w_gate_up, w_down)
    args = (p, g_mix, g_onorm, g_qa, g_kva, g_ffn, g_final)
    return (_trunk(x_prompt, *args), _trunk(x_sample, *args))
```

```python
import functools

import jax
import jax.numpy as jnp
from jax import lax
from jax.experimental import pallas as pl
from jax.experimental.pallas import tpu as pltpu

F32 = jnp.float32
BF16 = jnp.bfloat16

EPS = 1e-6
ROPE_THETA = 10000.0
HG_HEADS = 4
HG_HEAD_DIM = 128
HG_WIDTH = HG_HEADS * HG_HEAD_DIM
HG_CHUNK = 64
QK_NOPE = 64
QK_ROPE = 32
V_HEAD = 64
QK_HEAD = QK_NOPE + QK_ROPE
QK_PAD = 128
V_AUG = V_HEAD + 16
LOG2E = 1.4426950408889634
BOUND_SLACK = 1.0 + 2.0 ** -6
DENOM_FLOOR = 2.0 ** -64
OFF_ROWS = 16
VMEM_LIMIT_BYTES = 56 * 1024 * 1024

NT_DIMS = (((1,), (1,)), ((), ()))
TN_DIMS = (((0,), (0,)), ((), ()))


def _sigmoid(x):
    return 1.0 / (1.0 + jnp.exp(-x))


def _rms(x, g):
    ms = jnp.mean(x * x, axis=-1, keepdims=True)
    return x * lax.rsqrt(ms + EPS) * g


def _params(*sem):
    return pltpu.CompilerParams(dimension_semantics=sem, vmem_limit_bytes=VMEM_LIMIT_BYTES)


def _resident(shape):
    zeros = (0,) * len(shape)
    return pl.BlockSpec(shape, lambda *_: zeros, pipeline_mode=pl.Buffered(1))


def _hgrn_tile(q_of, v_of, f_of, lb, st_ref, emit, *, nc, chunk, reverse):
    C = chunk
    D = HG_HEAD_DIM
    heads = range(HG_HEADS)
    rows = lax.broadcasted_iota(jnp.int32, (C, C), 0)
    cols = lax.broadcasted_iota(jnp.int32, (C, C), 1)
    tri = (cols >= rows) if reverse else (cols <= rows)
    tri_b = tri.astype(BF16)
    mid = C // 2 if reverse else C // 2 - 1
    last = 0 if reverse else C - 1

    q_in, k_in, q_inter, k_state, v_b, decay = [], [], [], [], [], []
    for c in range(nc):
        f = lb + (1.0 - lb) * _sigmoid(f_of(c))
        g = jnp.log2(f)
        kk = 1.0 - f
        g_hi = g.astype(BF16)
        g_lo = (g - g_hi.astype(F32)).astype(BF16)
        b = (jnp.dot(tri_b, g_hi, preferred_element_type=F32)
             + jnp.dot(tri_b, g_lo, preferred_element_type=F32))
        ref = b[mid:mid + 1, :]
        b_last = b[last:last + 1, :]
        qi = q_of(c) * jnp.exp2(b - ref)
        ki = kk * jnp.exp2(ref - b)
        q_in.append(qi.astype(BF16))
        k_in.append(ki.astype(BF16))
        q_inter.append((qi * jnp.exp2(ref)).astype(BF16))
        k_state.append((ki * jnp.exp2(b_last - ref)).astype(BF16))
        v_b.append(v_of(c))
        decay.append(jnp.exp2(b_last))

    o_intra = [[None] * HG_HEADS for _ in range(nc)]
    incr = [[None] * HG_HEADS for _ in range(nc)]
    for c in range(nc):
        for h in heads:
            ls = slice(h * D, (h + 1) * D)
            att = lax.dot_general(q_in[c][:, ls], k_in[c][:, ls], NT_DIMS,
                                  preferred_element_type=F32)
            att = jnp.where(tri, att, 0.0).astype(BF16)
            o_intra[c][h] = jnp.dot(att, v_b[c][:, ls], preferred_element_type=F32)
            incr[c][h] = lax.dot_general(v_b[c][:, ls], k_state[c][:, ls], TN_DIMS,
                                         preferred_element_type=F32)

    st = [st_ref[h] for h in heads]
    for c in (range(nc - 1, -1, -1) if reverse else range(nc)):
        for h in heads:
            ls = slice(h * D, (h + 1) * D)
            o = o_intra[c][h] + lax.dot_general(q_inter[c][:, ls], st[h].astype(BF16), NT_DIMS,
                                                preferred_element_type=F32)
            st[h] = st[h] * decay[c][:, ls] + incr[c][h]
            emit(c, h, o)
    for h in heads:
        st_ref[h] = st[h]


def _mla_tile(mla, gqa_ref, gkva_ref, wq_ref, wk_ref, wv_ref,
              cos_t_ref, sin_t_ref, cosk_ref, sina_ref, sinb_ref,
              qt_ref, k_ref, vt_ref, ksq_ref, qsq_ref):
    heads = wq_ref.shape[0]
    scale = QK_HEAD ** -0.5 * LOG2E
    r = gqa_ref.shape[1]
    cq = _rms(mla[:, 0:r], gqa_ref[...]).astype(BF16)
    ckv = _rms(mla[:, r:2 * r], gkva_ref[...]).astype(BF16)
    kr = mla[:, 2 * r:2 * r + QK_PAD]
    half = QK_ROPE // 2
    k_pe = (kr * cosk_ref[...]
            + pltpu.roll(kr, half, 1) * sina_ref[...]
            + pltpu.roll(kr, QK_PAD - half, 1) * sinb_ref[...])
    cos_t = cos_t_ref[...]
    sin_t = sin_t_ref[...]
    pad_rows = vt_ref.shape[2] - V_HEAD
    ones_rows = (lax.broadcasted_iota(jnp.int32, (pad_rows, cos_t.shape[1]), 0) == 0).astype(F32)
    k_sq_rows = []
    one_lane = (lax.broadcasted_iota(jnp.int32, (1, QK_PAD), 1) == QK_HEAD).astype(F32)
    for h in range(heads):
        q_t = lax.dot_general(wq_ref[h], cq, NT_DIMS, preferred_element_type=F32)
        x1 = q_t[QK_NOPE:QK_NOPE + half]
        x2 = q_t[QK_NOPE + half:QK_HEAD]
        q_rot = jnp.concatenate(
            [q_t[:QK_NOPE], x1 * cos_t - x2 * sin_t, x2 * cos_t + x1 * sin_t, q_t[QK_HEAD:]],
            axis=0)
        q_b = (q_rot * scale).astype(qt_ref.dtype)
        qt_ref[0, h] = q_b
        q_f = q_b.astype(F32)
        qsq_ref[0, h] = jnp.sum(q_f * q_f, axis=0, keepdims=True)
        k_b = (jnp.dot(ckv, wk_ref[h], preferred_element_type=F32) + k_pe).astype(BF16)
        k_f = k_b.astype(F32)
        k_sq = jnp.sum(k_f * k_f, axis=1, keepdims=True)
        k_sq_rows.append(jnp.broadcast_to(jnp.max(k_sq, axis=0, keepdims=True), (1, QK_PAD)))
        k_ref[0, h] = (k_f + one_lane).astype(k_ref.dtype)
        v_t = lax.dot_general(wv_ref[h], ckv, NT_DIMS, preferred_element_type=F32)
        vt_ref[0, h] = jnp.concatenate([v_t, ones_rows], axis=0).astype(vt_ref.dtype)
    ksq_ref[0, 0] = jnp.concatenate(k_sq_rows, axis=0)


def _front_kernel(x_ref, g_ref, wact_ref, wdec_ref, wmla_ref, wgt_ref, lb_ref,
                  gqa_ref, gkva_ref, wq_ref, wk_ref, wv_ref,
                  cos_t_ref, sin_t_ref, cosk_ref, sina_ref, sinb_ref,
                  act_ref, fb_ref, gt_ref, ofw_ref, qt_ref, k_ref, vt_ref, ksq_ref, qsq_ref, st_ref,
                  *, chunk):
    W = HG_WIDTH
    C = chunk

    @pl.when(pl.program_id(1) == 0)
    def _():
        st_ref[...] = jnp.zeros_like(st_ref)

    h = _rms(x_ref[0], g_ref[...]).astype(BF16)
    a = jnp.dot(h, wact_ref[...], preferred_element_type=F32)
    q, i, g = a[:, :W], a[:, W:2 * W], a[:, 2 * W:]
    q_b = (q * _sigmoid(q)).astype(BF16)
    i_b = i.astype(BF16)
    act_ref[0] = jnp.concatenate([q_b, i_b, (g * _sigmoid(g)).astype(BF16)], axis=1)
    dec = jnp.dot(h, wdec_ref[...], preferred_element_type=F32)
    fb_ref[0] = dec[:, W:]
    gt_ref[0] = _sigmoid(jnp.dot(h, wgt_ref[...], preferred_element_type=F32)).astype(BF16)

    _mla_tile(jnp.dot(h, wmla_ref[...], preferred_element_type=F32), gqa_ref, gkva_ref,
              wq_ref, wk_ref, wv_ref, cos_t_ref, sin_t_ref, cosk_ref, sina_ref, sinb_ref,
              qt_ref, k_ref, vt_ref, ksq_ref, qsq_ref)

    def emit(c, hd, o):
        ofw_ref[0, c * C:(c + 1) * C, hd * HG_HEAD_DIM:(hd + 1) * HG_HEAD_DIM] = o

    rs = lambda c: slice(c * C, (c + 1) * C)
    _hgrn_tile(lambda c: q_b[rs(c)].astype(F32), lambda c: i_b[rs(c)], lambda c: dec[rs(c), :W],
               lb_ref[...], st_ref, emit, nc=x_ref.shape[1] // C, chunk=C, reverse=False)


def _front(x, p, g_mix, g_qa, g_kva, tabs, *, ts):
    B, S, D = x.shape
    W = HG_WIDTH
    H = p["wq_t"].shape[0]
    half = QK_ROPE // 2
    tok = lambda w: pl.BlockSpec((1, ts, w), lambda b, s: (b, s, 0))
    consts = (g_mix, p["w_act"], p["w_dec"], p["w_mla"], p["w_gt"], p["lb_fw"],
              g_qa, g_kva, p["wq_t"], p["wk"], p["wv_t"])
    lane_tab = pl.BlockSpec((half, ts), lambda b, s: (0, s))
    row_tab = pl.BlockSpec((ts, QK_PAD), lambda b, s: (s, 0))
    return pl.pallas_call(
        functools.partial(_front_kernel, chunk=HG_CHUNK),
        grid=(B, S // ts),
        in_specs=([tok(D)] + [_resident(c.shape) for c in consts]
                  + [lane_tab, lane_tab, row_tab, row_tab, row_tab]),
        out_specs=[
            tok(3 * W), tok(W), tok(p["w_gt"].shape[1]), tok(W),
            pl.BlockSpec((1, H, QK_PAD, ts), lambda b, s: (b, 0, 0, s)),
            pl.BlockSpec((1, H, ts, QK_PAD), lambda b, s: (b, 0, s, 0)),
            pl.BlockSpec((1, H, V_AUG, ts), lambda b, s: (b, 0, 0, s)),
            pl.BlockSpec((1, 1, H, QK_PAD), lambda b, s: (b, s, 0, 0)),
            pl.BlockSpec((1, H, 1, ts), lambda b, s: (b, 0, 0, s)),
        ],
        out_shape=[
            jax.ShapeDtypeStruct((B, S, 3 * W), BF16),
            jax.ShapeDtypeStruct((B, S, W), F32),
            jax.ShapeDtypeStruct((B, S, p["w_gt"].shape[1]), BF16),
            jax.ShapeDtypeStruct((B, S, W), F32),
            jax.ShapeDtypeStruct((B, H, QK_PAD, S), BF16),
            jax.ShapeDtypeStruct((B, H, S, QK_PAD), BF16),
            jax.ShapeDtypeStruct((B, H, V_AUG, S), BF16),
            jax.ShapeDtypeStruct((B, S // ts, H, QK_PAD), F32),
            jax.ShapeDtypeStruct((B, H, 1, S), F32),
        ],
        scratch_shapes=[pltpu.VMEM((HG_HEADS, HG_HEAD_DIM, HG_HEAD_DIM), F32)],
        compiler_params=_params("parallel", "arbitrary"),
        name="front",
    )(x, *consts, *tabs)


def _hgrn_bwd_kernel(q_ref, v_ref, f_ref, lb_ref, og_ref, ofw_ref, gon_ref, o_ref, st_ref, *,
                     chunk):
    C = chunk

    @pl.when(pl.program_id(1) == 0)
    def _():
        st_ref[...] = jnp.zeros_like(st_ref)

    def emit(c, hd, o):
        rs = slice(c * C, (c + 1) * C)
        ls = slice(hd * HG_HEAD_DIM, (hd + 1) * HG_HEAD_DIM)
        o = o + ofw_ref[0, rs, ls]
        o = _rms(o, gon_ref[:, ls]) * og_ref[0, rs, ls].astype(F32)
        o_ref[0, rs, ls] = o.astype(o_ref.dtype)

    rs = lambda c: slice(c * C, (c + 1) * C)
    _hgrn_tile(lambda c: q_ref[0, rs(c), :].astype(F32), lambda c: v_ref[0, rs(c), :],
               lambda c: f_ref[0, rs(c), :], lb_ref[...], st_ref, emit,
               nc=q_ref.shape[1] // C, chunk=C, reverse=True)


def _hgrn_bwd(act, fb, lb_row, o_fw, g_onorm, *, ts):
    B, S, W = fb.shape
    ns = S // ts
    col = lambda j: pl.BlockSpec((1, ts, W), lambda b, s: (b, ns - 1 - s, j))
    vec = pl.BlockSpec((1, W), lambda b, s: (0, 0))
    return pl.pallas_call(
        functools.partial(_hgrn_bwd_kernel, chunk=HG_CHUNK),
        grid=(B, ns),
        in_specs=[col(0), col(1), col(0), vec, col(2), col(0), vec],
        out_specs=col(0),
        out_shape=jax.ShapeDtypeStruct((B, S, W), BF16),
        scratch_shapes=[pltpu.VMEM((HG_HEADS, HG_HEAD_DIM, HG_HEAD_DIM), F32)],
        compiler_params=_params("parallel", "arbitrary"),
        name="hgrn_bwd",
    )(act, act, fb, lb_row, act, o_fw, g_onorm)


def _attn_kernel(qt_ref, k_ref, vt_ref, ksq_ref, qsq_ref, o_ref, den_ref, acc_ref, *, tk):
    q_t = qt_ref[0, 0]
    tq = q_t.shape[1]
    k_sq_max = jnp.max(jnp.max(ksq_ref[0, 0], axis=1, keepdims=True), axis=0, keepdims=True)
    bound = jnp.sqrt(qsq_ref[0, 0] * k_sq_max) * BOUND_SLACK
    off_rows = jnp.concatenate([-bound, jnp.zeros((OFF_ROWS - 1, tq), F32)], axis=0).astype(BF16)
    q_off = jnp.concatenate([q_t[:QK_HEAD], off_rows, q_t[QK_HEAD + OFF_ROWS:]], axis=0)

    def step(j, carry):
        k0 = pl.multiple_of(j * tk, tk)
        s = jnp.dot(k_ref[0, 0, pl.ds(k0, tk), :], q_off, preferred_element_type=F32)
        p = jnp.exp2(s).astype(BF16)
        acc_ref[...] += jnp.dot(vt_ref[0, 0, :, pl.ds(k0, tk)], p, preferred_element_type=F32)
        return carry

    acc_ref[...] = jnp.zeros_like(acc_ref)
    lax.fori_loop(0, k_ref.shape[2] // tk, step, 0)
    acc = acc_ref[...]
    den = acc[V_HEAD:V_HEAD + 1]
    den_ref[0, 0] = den
    o_ref[0, 0] = (acc[:V_HEAD] / den).astype(o_ref.dtype)


def _attention(q_t, k, v_t, k_sq, q_sq, *, tq, tk):
    B, H, _, S = q_t.shape
    VA = v_t.shape[2]
    per_q = lambda rows: pl.BlockSpec((1, 1, rows, tq), lambda b, h, i: (b, h, 0, i))
    whole = lambda a: pl.BlockSpec((1, 1) + a.shape[2:], lambda b, h, i: (b, h, 0, 0))
    return pl.pallas_call(
        functools.partial(_attn_kernel, tk=tk),
        grid=(B, H, S // tq),
        in_specs=[per_q(QK_PAD), whole(k), whole(v_t), whole(k_sq), per_q(1)],
        out_specs=[per_q(V_HEAD), per_q(1)],
        out_shape=[jax.ShapeDtypeStruct((B, H, V_HEAD, S), BF16),
                   jax.ShapeDtypeStruct((B, H, 1, S), F32)],
        scratch_shapes=[pltpu.VMEM((VA, tq), F32)],
        compiler_params=_params("parallel", "parallel", "arbitrary"),
        name="attention",
    )(q_t, k, v_t, k_sq, q_sq)


def _attn_online_kernel(qt_ref, k_ref, vt_ref, o_ref, s_ref, acc_ref, m_ref, *, tk):
    q_t = qt_ref[0, 0]
    nk = k_ref.shape[2] // tk

    def scores(j, slot):
        k0 = pl.multiple_of(jnp.minimum(j, nk - 1) * tk, tk)
        s_ref[slot] = jnp.dot(k_ref[0, 0, pl.ds(k0, tk), :], q_t, preferred_element_type=F32)

    def softmax_pv(j, slot):
        s = s_ref[slot]
        m_old = m_ref[...]
        m_new = jnp.maximum(m_old, jnp.max(s, axis=0, keepdims=True))
        p = jnp.exp2(s - m_new).astype(BF16)
        k0 = pl.multiple_of(j * tk, tk)
        pv = jnp.dot(vt_ref[0, 0, :, pl.ds(k0, tk)], p, preferred_element_type=F32)
        acc_ref[...] = jnp.exp2(m_old - m_new) * acc_ref[...] + pv
        m_ref[...] = m_new

    def pair(i, carry):
        j = 2 * i
        scores(j + 1, 1)
        softmax_pv(j, 0)
        scores(j + 2, 0)
        softmax_pv(j + 1, 1)
        return carry

    m_ref[...] = jnp.full_like(m_ref, -1e30)
    acc_ref[...] = jnp.zeros_like(acc_ref)
    scores(0, 0)
    lax.fori_loop(0, nk // 2, pair, 0)
    acc = acc_ref[...]
    o_ref[0, 0] = (acc[:V_HEAD] / acc[V_HEAD:V_HEAD + 1]).astype(o_ref.dtype)


def _attention_online(q_t, k, v_t, *, tq, tk):
    B, H, _, S = q_t.shape
    VA = v_t.shape[2]
    assert (S // tk) % 2 == 0
    per_q = lambda rows: pl.BlockSpec((1, 1, rows, tq), lambda b, h, i: (b, h, 0, i))
    whole = lambda a: pl.BlockSpec((1, 1) + a.shape[2:], lambda b, h, i: (b, h, 0, 0))
    return pl.pallas_call(
        functools.partial(_attn_online_kernel, tk=tk),
        grid=(B, H, S // tq),
        in_specs=[per_q(QK_PAD), whole(k), whole(v_t)],
        out_specs=per_q(V_HEAD),
        out_shape=jax.ShapeDtypeStruct((B, H, V_HEAD, S), BF16),
        scratch_shapes=[pltpu.VMEM((2, tk, tq), F32), pltpu.VMEM((VA, tq), F32),
                        pltpu.VMEM((1, tq), F32)],
        compiler_params=_params("parallel", "parallel", "arbitrary"),
        name="attention_online",
    )(q_t, k, v_t)


def _mix_ffn_kernel(x_ref, oh_ref, omt_ref, gt_ref, wb0_ref, wb1_ref, wo_ref,
                    gf_ref, wgu_ref, wd_ref, gfin_ref, y_ref):
    D = x_ref.shape[2]
    dff = wd_ref.shape[0]
    bp0 = jnp.dot(oh_ref[0], wb0_ref[...], preferred_element_type=F32)
    bp1 = lax.dot_general(omt_ref[0], wb1_ref[...], TN_DIMS, preferred_element_type=F32)
    gt = gt_ref[0].astype(F32)
    merged = gt[:, :D] * bp0 + gt[:, D:] * bp1
    x1 = x_ref[0] + jnp.dot(merged.astype(BF16), wo_ref[...], preferred_element_type=F32)
    h2 = _rms(x1, gf_ref[...]).astype(BF16)
    gu = jnp.dot(h2, wgu_ref[...], preferred_element_type=F32)
    gate = gu[:, :dff]
    act = (gate * _sigmoid(gate)) * gu[:, dff:]
    x2 = x1 + jnp.dot(act.astype(BF16), wd_ref[...], preferred_element_type=F32)
    y_ref[0] = _rms(x2, gfin_ref[...])


def _mix_ffn(x, o_h, o_mt, gt, wb0, wb1, wo, g_ffn, wgu, wd, g_final, *, tm):
    B, S, D = x.shape
    tok = lambda w: pl.BlockSpec((1, tm, w), lambda b, s: (b, s, 0))
    consts = (wb0, wb1, wo, g_ffn, wgu, wd, g_final)
    return pl.pallas_call(
        _mix_ffn_kernel,
        grid=(B, S // tm),
        in_specs=[
            tok(D), tok(o_h.shape[2]),
            pl.BlockSpec((1, o_mt.shape[1], tm), lambda b, s: (b, 0, s)),
            tok(gt.shape[2]),
        ] + [_resident(c.shape) for c in consts],
        out_specs=tok(D),
        out_shape=jax.ShapeDtypeStruct((B, S, D), F32),
        compiler_params=_params("parallel", "parallel"),
        name="mix_ffn",
    )(x, o_h, o_mt, gt, *consts)


def _rope_tables(S):
    d = QK_ROPE
    half = d // 2
    inv = ROPE_THETA ** (-jnp.arange(0, d, 2, dtype=F32) / d)
    ang = jnp.arange(S, dtype=F32)[:, None] * inv[None, :]
    cos, sin = jnp.cos(ang), jnp.sin(ang)
    z = lambda n: jnp.zeros((S, n), F32)
    tail = QK_PAD - QK_HEAD
    cosk = jnp.concatenate([z(QK_NOPE), cos, cos, z(tail)], axis=1)
    sina = jnp.concatenate([z(QK_NOPE + half), sin, z(tail)], axis=1)
    sinb = jnp.concatenate([z(QK_NOPE), -sin, z(half + tail)], axis=1)
    return cos.T, sin.T, cosk, sina, sinb


def _prep_weights(w_in, lb_param, w_uq, w_ukv, w_branch, w_out, w_gate_up, w_down):
    W = HG_WIDTH
    r_q, H = w_uq.shape[1], w_uq.shape[2]
    r_kv = w_ukv.shape[1]
    w = w_in[0]
    D = w.shape[0]
    c0 = 5 * W
    w_act = jnp.concatenate([w[:, :2 * W], w[:, 4 * W:c0]], axis=1)
    w_dec = w[:, 2 * W:4 * W]
    w_qa = w[:, c0:c0 + r_q]
    w_kva = w[:, c0 + r_q:c0 + r_q + r_kv]
    c1 = c0 + r_q + r_kv
    w_kr = w[:, c1:c1 + QK_ROPE]
    w_gt = w[:, c1 + QK_ROPE:]
    zpad = lambda n: jnp.zeros((D, n), w.dtype)
    w_mla = jnp.concatenate([w_qa, w_kva, zpad(QK_NOPE), w_kr, zpad(QK_PAD - QK_HEAD)], axis=1)
    wq = jnp.transpose(w_uq[0], (1, 2, 0))
    wq_t = jnp.concatenate([wq, jnp.zeros((H, QK_PAD - QK_HEAD, r_q), wq.dtype)], axis=1)
    wkv = jnp.transpose(w_ukv[0], (1, 0, 2))
    wk = jnp.concatenate([wkv[:, :, :QK_NOPE],
                          jnp.zeros((H, r_kv, QK_PAD - QK_NOPE), wkv.dtype)], axis=2)
    wv_t = jnp.transpose(wkv[:, :, QK_NOPE:], (0, 2, 1))
    lb = jax.nn.softmax(lb_param.astype(F32), axis=0)[0]
    b16 = lambda a: a.astype(BF16)
    return dict(
        w_act=b16(w_act), w_dec=b16(w_dec), w_mla=b16(w_mla), w_gt=b16(w_gt),
        wq_t=b16(wq_t), wk=b16(wk), wv_t=b16(wv_t),
        wb0=b16(w_branch[0, 0]), wb1=b16(w_branch[0, 1]), wo=b16(w_out[0]),
        wgu=b16(w_gate_up[0]), wd=b16(w_down[0]),
        lb_fw=lb[0:1], lb_bw=lb[1:2],
    )


def _pick(n, pref):
    t = min(n, pref)
    while n % t:
        t //= 2
    return t


def _trunk(x, p, g_mix, g_onorm, g_qa, g_kva, g_ffn, g_final):
    B, S, D = x.shape
    ts = _pick(S, 512)
    act, fb, gt, o_fw, q_t, k, v_t, k_sq, q_sq = _front(x, p, g_mix, g_qa, g_kva, _rope_tables(S),
                                                        ts=ts)
    o_h = _hgrn_bwd(act, fb, p["lb_bw"], o_fw, g_onorm, ts=ts)
    k_sq = jnp.transpose(k_sq, (0, 2, 1, 3))
    o_mt, den = _attention(q_t, k, v_t, k_sq, q_sq, tq=_pick(S, 1024), tk=_pick(S, 8192))
    o_mt = lax.cond(jnp.min(den) >= DENOM_FLOOR, lambda: o_mt,
                    lambda: _attention_online(q_t, k, v_t, tq=_pick(S, 512),
                                              tk=_pick(S // 2, 1024)))
    o_mt = o_mt.reshape(B, -1, S)

    return _mix_ffn(x, o_h, o_mt, gt, p["wb0"], p["wb1"], p["wo"], g_ffn, p["wgu"], p["wd"],
                    g_final[None, :], tm=_pick(S, 512))


def kernel(x_prompt, x_sample, g_mix, w_in, lb_param, g_onorm, g_qa, w_uq, g_kva, w_ukv,
           w_branch, w_out, g_ffn, w_gate_up, w_down, g_final):
    p = _prep_weights(w_in, lb_param, w_uq, w_ukv, w_branch, w_out, w_gate_up, w_down)
    args = (p, g_mix, g_onorm, g_qa, g_kva, g_ffn, g_final)
    return (_trunk(x_prompt, *args), _trunk(x_sample, *args))
```

```python
import functools

import jax
import jax.numpy as jnp
from jax import lax
from jax.experimental import pallas as pl
from jax.experimental.pallas import tpu as pltpu

F32 = jnp.float32
BF16 = jnp.bfloat16

EPS = 1e-6
ROPE_THETA = 10000.0
HG_HEADS = 4
HG_HEAD_DIM = 128
HG_WIDTH = HG_HEADS * HG_HEAD_DIM
HG_CHUNK = 64
QK_NOPE = 64
QK_ROPE = 32
V_HEAD = 64
QK_HEAD = QK_NOPE + QK_ROPE
QK_PAD = 128
V_AUG = V_HEAD + 16
LOG2E = 1.4426950408889634
BOUND_SLACK = 1.0 + 2.0 ** -6
DENOM_FLOOR = 2.0 ** -64
OFF_ROWS = 16
VMEM_LIMIT_BYTES = 56 * 1024 * 1024

NT_DIMS = (((1,), (1,)), ((), ()))
TN_DIMS = (((0,), (0,)), ((), ()))


def _sigmoid(x):
    return 1.0 / (1.0 + jnp.exp(-x))


def _rms(x, g):
    ms = jnp.mean(x * x, axis=-1, keepdims=True)
    return x * lax.rsqrt(ms + EPS) * g


def _params(*sem):
    return pltpu.CompilerParams(dimension_semantics=sem, vmem_limit_bytes=VMEM_LIMIT_BYTES)


def _resident(shape):
    zeros = (0,) * len(shape)
    return pl.BlockSpec(shape, lambda *_: zeros, pipeline_mode=pl.Buffered(1))


def _hgrn_tile(q_of, v_of, f_of, lb, st_ref, emit, *, nc, chunk, reverse):
    C = chunk
    D = HG_HEAD_DIM
    heads = range(HG_HEADS)
    rows = lax.broadcasted_iota(jnp.int32, (C, C), 0)
    cols = lax.broadcasted_iota(jnp.int32, (C, C), 1)
    tri = (cols >= rows) if reverse else (cols <= rows)
    tri_b = tri.astype(BF16)
    mid = C // 2 if reverse else C // 2 - 1
    last = 0 if reverse else C - 1

    q_in, k_in, q_inter, k_state, v_b, decay = [], [], [], [], [], []
    for c in range(nc):
        f = lb + (1.0 - lb) * _sigmoid(f_of(c))
        g = jnp.log2(f)
        kk = 1.0 - f
        g_hi = g.astype(BF16)
        g_lo = (g - g_hi.astype(F32)).astype(BF16)
        b = (jnp.dot(tri_b, g_hi, preferred_element_type=F32)
             + jnp.dot(tri_b, g_lo, preferred_element_type=F32))
        ref = b[mid:mid + 1, :]
        b_last = b[last:last + 1, :]
        qi = q_of(c) * jnp.exp2(b - ref)
        ki = kk * jnp.exp2(ref - b)
        q_in.append(qi.astype(BF16))
        k_in.append(ki.astype(BF16))
        q_inter.append((qi * jnp.exp2(ref)).astype(BF16))
        k_state.append((ki * jnp.exp2(b_last - ref)).astype(BF16))
        v_b.append(v_of(c))
        decay.append(jnp.exp2(b_last))

    o_intra = [[None] * HG_HEADS for _ in range(nc)]
    incr = [[None] * HG_HEADS for _ in range(nc)]
    for c in range(nc):
        for h in heads:
            ls = slice(h * D, (h + 1) * D)
            att = lax.dot_general(q_in[c][:, ls], k_in[c][:, ls], NT_DIMS,
                                  preferred_element_type=F32)
            att = jnp.where(tri, att, 0.0).astype(BF16)
            o_intra[c][h] = jnp.dot(att, v_b[c][:, ls], preferred_element_type=F32)
            incr[c][h] = lax.dot_general(v_b[c][:, ls], k_state[c][:, ls], TN_DIMS,
                                         preferred_element_type=F32)

    st = [st_ref[h] for h in heads]
    for c in (range(nc - 1, -1, -1) if reverse else range(nc)):
        for h in heads:
            ls = slice(h * D, (h + 1) * D)
            o = o_intra[c][h] + lax.dot_general(q_inter[c][:, ls], st[h].astype(BF16), NT_DIMS,
                                                preferred_element_type=F32)
            st[h] = st[h] * decay[c][:, ls] + incr[c][h]
            emit(c, h, o)
    for h in heads:
        st_ref[h] = st[h]


def _mla_tile(mla, gqa_ref, gkva_ref, wq_ref, wk_ref, wv_ref,
              cos_t_ref, sin_t_ref, cosk_ref, sina_ref, sinb_ref,
              qt_ref, k_ref, vt_ref, ksq_ref, qsq_ref):
    heads = wq_ref.shape[0] // QK_PAD
    scale = QK_HEAD ** -0.5 * LOG2E
    r = gqa_ref.shape[1]
    cq = _rms(mla[:, 0:r], gqa_ref[...]).astype(BF16)
    ckv = _rms(mla[:, r:2 * r], gkva_ref[...]).astype(BF16)
    kr = mla[:, 2 * r:2 * r + QK_PAD]
    half = QK_ROPE // 2
    k_pe = (kr * cosk_ref[...]
            + pltpu.roll(kr, half, 1) * sina_ref[...]
            + pltpu.roll(kr, QK_PAD - half, 1) * sinb_ref[...])
    cos_t = cos_t_ref[...]
    sin_t = sin_t_ref[...]
    q_all = lax.dot_general(wq_ref[...], cq, NT_DIMS, preferred_element_type=F32)
    k_all = jnp.dot(ckv, wk_ref[...], preferred_element_type=F32)
    v_all = lax.dot_general(wv_ref[...], ckv, NT_DIMS, preferred_element_type=F32)
    pad_rows = vt_ref.shape[2] - V_HEAD
    ones_rows = (lax.broadcasted_iota(jnp.int32, (pad_rows, cos_t.shape[1]), 0) == 0).astype(F32)
    k_sq_rows = []
    one_lane = (lax.broadcasted_iota(jnp.int32, (1, QK_PAD), 1) == QK_HEAD).astype(F32)
    for h in range(heads):
        q_t = q_all[h * QK_PAD:(h + 1) * QK_PAD]
        x1 = q_t[QK_NOPE:QK_NOPE + half]
        x2 = q_t[QK_NOPE + half:QK_HEAD]
        q_rot = jnp.concatenate(
            [q_t[:QK_NOPE], x1 * cos_t - x2 * sin_t, x2 * cos_t + x1 * sin_t, q_t[QK_HEAD:]],
            axis=0)
        q_b = (q_rot * scale).astype(qt_ref.dtype)
        qt_ref[0, h] = q_b
        q_f = q_b.astype(F32)
        qsq_ref[0, h] = jnp.sum(q_f * q_f, axis=0, keepdims=True)
        k_b = (k_all[:, h * QK_PAD:(h + 1) * QK_PAD] + k_pe).astype(BF16)
        k_f = k_b.astype(F32)
        k_sq = jnp.sum(k_f * k_f, axis=1, keepdims=True)
        k_sq_rows.append(jnp.broadcast_to(jnp.max(k_sq, axis=0, keepdims=True), (1, QK_PAD)))
        k_ref[0, h] = (k_f + one_lane).astype(k_ref.dtype)
        v_t = v_all[h * V_HEAD:(h + 1) * V_HEAD]
        vt_ref[0, h] = jnp.concatenate([v_t, ones_rows], axis=0).astype(vt_ref.dtype)
    ksq_ref[0, 0] = jnp.concatenate(k_sq_rows, axis=0)


def _front_kernel(x_ref, g_ref, wact_ref, wdec_ref, wmla_ref, wgt_ref, lb_ref,
                  gqa_ref, gkva_ref, wq_ref, wk_ref, wv_ref,
                  cos_t_ref, sin_t_ref, cosk_ref, sina_ref, sinb_ref,
                  act_ref, fb_ref, gt_ref, ofw_ref, qt_ref, k_ref, vt_ref, ksq_ref, qsq_ref, st_ref,
                  *, chunk):
    W = HG_WIDTH
    C = chunk

    @pl.when(pl.program_id(1) == 0)
    def _():
        st_ref[...] = jnp.zeros_like(st_ref)

    h = _rms(x_ref[0], g_ref[...]).astype(BF16)
    a = jnp.dot(h, wact_ref[...], preferred_element_type=F32)
    q, i, g = a[:, :W], a[:, W:2 * W], a[:, 2 * W:]
    q_b = (q * _sigmoid(q)).astype(BF16)
    i_b = i.astype(BF16)
    act_ref[0] = jnp.concatenate([q_b, i_b, (g * _sigmoid(g)).astype(BF16)], axis=1)
    dec = jnp.dot(h, wdec_ref[...], preferred_element_type=F32)
    fb_ref[0] = dec[:, W:]
    gt_ref[0] = _sigmoid(jnp.dot(h, wgt_ref[...], preferred_element_type=F32)).astype(BF16)

    _mla_tile(jnp.dot(h, wmla_ref[...], preferred_element_type=F32), gqa_ref, gkva_ref,
              wq_ref, wk_ref, wv_ref, cos_t_ref, sin_t_ref, cosk_ref, sina_ref, sinb_ref,
              qt_ref, k_ref, vt_ref, ksq_ref, qsq_ref)

    def emit(c, hd, o):
        ofw_ref[0, c * C:(c + 1) * C, hd * HG_HEAD_DIM:(hd + 1) * HG_HEAD_DIM] = o

    rs = lambda c: slice(c * C, (c + 1) * C)
    _hgrn_tile(lambda c: q_b[rs(c)].astype(F32), lambda c: i_b[rs(c)], lambda c: dec[rs(c), :W],
               lb_ref[...], st_ref, emit, nc=x_ref.shape[1] // C, chunk=C, reverse=False)


def _front(x, p, g_mix, g_qa, g_kva, tabs, *, ts):
    B, S, D = x.shape
    W = HG_WIDTH
    H = p["wq_t"].shape[0] // QK_PAD
    half = QK_ROPE // 2
    tok = lambda w: pl.BlockSpec((1, ts, w), lambda b, s: (b, s, 0))
    consts = (g_mix, p["w_act"], p["w_dec"], p["w_mla"], p["w_gt"], p["lb_fw"],
              g_qa, g_kva, p["wq_t"], p["wk"], p["wv_t"])
    lane_tab = pl.BlockSpec((half, ts), lambda b, s: (0, s))
    row_tab = pl.BlockSpec((ts, QK_PAD), lambda b, s: (s, 0))
    return pl.pallas_call(
        functools.partial(_front_kernel, chunk=HG_CHUNK),
        grid=(B, S // ts),
        in_specs=([tok(D)] + [_resident(c.shape) for c in consts]
                  + [lane_tab, lane_tab, row_tab, row_tab, row_tab]),
        out_specs=[
            tok(3 * W), tok(W), tok(p["w_gt"].shape[1]), tok(W),
            pl.BlockSpec((1, H, QK_PAD, ts), lambda b, s: (b, 0, 0, s)),
            pl.BlockSpec((1, H, ts, QK_PAD), lambda b, s: (b, 0, s, 0)),
            pl.BlockSpec((1, H, V_AUG, ts), lambda b, s: (b, 0, 0, s)),
            pl.BlockSpec((1, 1, H, QK_PAD), lambda b, s: (b, s, 0, 0)),
            pl.BlockSpec((1, H, 1, ts), lambda b, s: (b, 0, 0, s)),
        ],
        out_shape=[
            jax.ShapeDtypeStruct((B, S, 3 * W), BF16),
            jax.ShapeDtypeStruct((B, S, W), F32),
            jax.ShapeDtypeStruct((B, S, p["w_gt"].shape[1]), BF16),
            jax.ShapeDtypeStruct((B, S, W), F32),
            jax.ShapeDtypeStruct((B, H, QK_PAD, S), BF16),
            jax.ShapeDtypeStruct((B, H, S, QK_PAD), BF16),
            jax.ShapeDtypeStruct((B, H, V_AUG, S), BF16),
            jax.ShapeDtypeStruct((B, S // ts, H, QK_PAD), F32),
            jax.ShapeDtypeStruct((B, H, 1, S), F32),
        ],
        scratch_shapes=[pltpu.VMEM((HG_HEADS, HG_HEAD_DIM, HG_HEAD_DIM), F32)],
        compiler_params=_params("parallel", "arbitrary"),
        name="front",
    )(x, *consts, *tabs)


def _hgrn_bwd_kernel(q_ref, v_ref, f_ref, lb_ref, og_ref, ofw_ref, gon_ref, o_ref, st_ref, *,
                     chunk):
    C = chunk

    @pl.when(pl.program_id(1) == 0)
    def _():
        st_ref[...] = jnp.zeros_like(st_ref)

    def emit(c, hd, o):
        rs = slice(c * C, (c + 1) * C)
        ls = slice(hd * HG_HEAD_DIM, (hd + 1) * HG_HEAD_DIM)
        o = o + ofw_ref[0, rs, ls]
        o = _rms(o, gon_ref[:, ls]) * og_ref[0, rs, ls].astype(F32)
        o_ref[0, rs, ls] = o.astype(o_ref.dtype)

    rs = lambda c: slice(c * C, (c + 1) * C)
    _hgrn_tile(lambda c: q_ref[0, rs(c), :].astype(F32), lambda c: v_ref[0, rs(c), :],
               lambda c: f_ref[0, rs(c), :], lb_ref[...], st_ref, emit,
               nc=q_ref.shape[1] // C, chunk=C, reverse=True)


def _hgrn_bwd(act, fb, lb_row, o_fw, g_onorm, *, ts):
    B, S, W = fb.shape
    ns = S // ts
    col = lambda j: pl.BlockSpec((1, ts, W), lambda b, s: (b, ns - 1 - s, j))
    vec = pl.BlockSpec((1, W), lambda b, s: (0, 0))
    return pl.pallas_call(
        functools.partial(_hgrn_bwd_kernel, chunk=HG_CHUNK),
        grid=(B, ns),
        in_specs=[col(0), col(1), col(0), vec, col(2), col(0), vec],
        out_specs=col(0),
        out_shape=jax.ShapeDtypeStruct((B, S, W), BF16),
        scratch_shapes=[pltpu.VMEM((HG_HEADS, HG_HEAD_DIM, HG_HEAD_DIM), F32)],
        compiler_params=_params("parallel", "arbitrary"),
        name="hgrn_bwd",
    )(act, act, fb, lb_row, act, o_fw, g_onorm)


def _attn_kernel(qt_ref, k_ref, vt_ref, ksq_ref, qsq_ref, o_ref, den_ref, acc_ref, *, tk):
    q_t = qt_ref[0, 0]
    tq = q_t.shape[1]
    k_sq_max = jnp.max(jnp.max(ksq_ref[0, 0], axis=1, keepdims=True), axis=0, keepdims=True)
    bound = jnp.sqrt(qsq_ref[0, 0] * k_sq_max) * BOUND_SLACK
    off_rows = jnp.concatenate([-bound, jnp.zeros((OFF_ROWS - 1, tq), F32)], axis=0).astype(BF16)
    q_off = jnp.concatenate([q_t[:QK_HEAD], off_rows, q_t[QK_HEAD + OFF_ROWS:]], axis=0)

    def step(j, carry):
        k0 = pl.multiple_of(j * tk, tk)
        s = jnp.dot(k_ref[0, 0, pl.ds(k0, tk), :], q_off, preferred_element_type=F32)
        p = jnp.exp2(s).astype(BF16)
        acc_ref[...] += jnp.dot(vt_ref[0, 0, :, pl.ds(k0, tk)], p, preferred_element_type=F32)
        return carry

    acc_ref[...] = jnp.zeros_like(acc_ref)
    lax.fori_loop(0, k_ref.shape[2] // tk, step, 0)
    acc = acc_ref[...]
    den = acc[V_HEAD:V_HEAD + 1]
    den_ref[0, 0] = den
    o_ref[0, 0] = (acc[:V_HEAD] / den).astype(o_ref.dtype)


def _attention(q_t, k, v_t, k_sq, q_sq, *, tq, tk):
    B, H, _, S = q_t.shape
    VA = v_t.shape[2]
    per_q = lambda rows: pl.BlockSpec((1, 1, rows, tq), lambda b, h, i: (b, h, 0, i))
    whole = lambda a: pl.BlockSpec((1, 1) + a.shape[2:], lambda b, h, i: (b, h, 0, 0))
    return pl.pallas_call(
        functools.partial(_attn_kernel, tk=tk),
        grid=(B, H, S // tq),
        in_specs=[per_q(QK_PAD), whole(k), whole(v_t), whole(k_sq), per_q(1)],
        out_specs=[per_q(V_HEAD), per_q(1)],
        out_shape=[jax.ShapeDtypeStruct((B, H, V_HEAD, S), BF16),
                   jax.ShapeDtypeStruct((B, H, 1, S), F32)],
        scratch_shapes=[pltpu.VMEM((VA, tq), F32)],
        compiler_params=_params("parallel", "parallel", "arbitrary"),
        name="attention",
    )(q_t, k, v_t, k_sq, q_sq)


def _attn_online_kernel(qt_ref, k_ref, vt_ref, o_ref, s_ref, acc_ref, m_ref, *, tk):
    q_t = qt_ref[0, 0]
    nk = k_ref.shape[2] // tk

    def scores(j, slot):
        k0 = pl.multiple_of(jnp.minimum(j, nk - 1) * tk, tk)
        s_ref[slot] = jnp.dot(k_ref[0, 0, pl.ds(k0, tk), :], q_t, preferred_element_type=F32)

    def softmax_pv(j, slot):
        s = s_ref[slot]
        m_old = m_ref[...]
        m_new = jnp.maximum(m_old, jnp.max(s, axis=0, keepdims=True))
        p = jnp.exp2(s - m_new).astype(BF16)
        k0 = pl.multiple_of(j * tk, tk)
        pv = jnp.dot(vt_ref[0, 0, :, pl.ds(k0, tk)], p, preferred_element_type=F32)
        acc_ref[...] = jnp.exp2(m_old - m_new) * acc_ref[...] + pv
        m_ref[...] = m_new

    def pair(i, carry):
        j = 2 * i
        scores(j + 1, 1)
        softmax_pv(j, 0)
        scores(j + 2, 0)
        softmax_pv(j + 1, 1)
        return carry

    m_ref[...] = jnp.full_like(m_ref, -1e30)
    acc_ref[...] = jnp.zeros_like(acc_ref)
    scores(0, 0)
    lax.fori_loop(0, nk // 2, pair, 0)
    acc = acc_ref[...]
    o_ref[0, 0] = (acc[:V_HEAD] / acc[V_HEAD:V_HEAD + 1]).astype(o_ref.dtype)


def _attention_online(q_t, k, v_t, *, tq, tk):
    B, H, _, S = q_t.shape
    VA = v_t.shape[2]
    assert (S // tk) % 2 == 0
    per_q = lambda rows: pl.BlockSpec((1, 1, rows, tq), lambda b, h, i: (b, h, 0, i))
    whole = lambda a: pl.BlockSpec((1, 1) + a.shape[2:], lambda b, h, i: (b, h, 0, 0))
    return pl.pallas_call(
        functools.partial(_attn_online_kernel, tk=tk),
        grid=(B, H, S // tq),
        in_specs=[per_q(QK_PAD), whole(k), whole(v_t)],
        out_specs=per_q(V_HEAD),
        out_shape=jax.ShapeDtypeStruct((B, H, V_HEAD, S), BF16),
        scratch_shapes=[pltpu.VMEM((2, tk, tq), F32), pltpu.VMEM((VA, tq), F32),
                        pltpu.VMEM((1, tq), F32)],
        compiler_params=_params("parallel", "parallel", "arbitrary"),
        name="attention_online",
    )(q_t, k, v_t)


def _mix_ffn_kernel(x_ref, oh_ref, omt_ref, gt_ref, wb0_ref, wb1_ref, wo_ref,
                    gf_ref, wgu_ref, wd_ref, gfin_ref, y_ref):
    D = x_ref.shape[2]
    dff = wd_ref.shape[0]
    bp0 = jnp.dot(oh_ref[0], wb0_ref[...], preferred_element_type=F32)
    bp1 = lax.dot_general(omt_ref[0], wb1_ref[...], TN_DIMS, preferred_element_type=F32)
    gt = gt_ref[0].astype(F32)
    merged = gt[:, :D] * bp0 + gt[:, D:] * bp1
    x1 = x_ref[0] + jnp.dot(merged.astype(BF16), wo_ref[...], preferred_element_type=F32)
    h2 = _rms(x1, gf_ref[...]).astype(BF16)
    gu = jnp.dot(h2, wgu_ref[...], preferred_element_type=F32)
    gate = gu[:, :dff]
    act = (gate * _sigmoid(gate)) * gu[:, dff:]
    x2 = x1 + jnp.dot(act.astype(BF16), wd_ref[...], preferred_element_type=F32)
    y_ref[0] = _rms(x2, gfin_ref[...])


def _mix_ffn(x, o_h, o_mt, gt, wb0, wb1, wo, g_ffn, wgu, wd, g_final, *, tm):
    B, S, D = x.shape
    tok = lambda w: pl.BlockSpec((1, tm, w), lambda b, s: (b, s, 0))
    consts = (wb0, wb1, wo, g_ffn, wgu, wd, g_final)
    return pl.pallas_call(
        _mix_ffn_kernel,
        grid=(B, S // tm),
        in_specs=[
            tok(D), tok(o_h.shape[2]),
            pl.BlockSpec((1, o_mt.shape[1], tm), lambda b, s: (b, 0, s)),
            tok(gt.shape[2]),
        ] + [_resident(c.shape) for c in consts],
        out_specs=tok(D),
        out_shape=jax.ShapeDtypeStruct((B, S, D), F32),
        compiler_params=_params("parallel", "parallel"),
        name="mix_ffn",
    )(x, o_h, o_mt, gt, *consts)


def _rope_tables(S):
    d = QK_ROPE
    half = d // 2
    inv = ROPE_THETA ** (-jnp.arange(0, d, 2, dtype=F32) / d)
    ang = jnp.arange(S, dtype=F32)[:, None] * inv[None, :]
    cos, sin = jnp.cos(ang), jnp.sin(ang)
    z = lambda n: jnp.zeros((S, n), F32)
    tail = QK_PAD - QK_HEAD
    cosk = jnp.concatenate([z(QK_NOPE), cos, cos, z(tail)], axis=1)
    sina = jnp.concatenate([z(QK_NOPE + half), sin, z(tail)], axis=1)
    sinb = jnp.concatenate([z(QK_NOPE), -sin, z(half + tail)], axis=1)
    return cos.T, sin.T, cosk, sina, sinb


def _prep_weights(w_in, lb_param, w_uq, w_ukv, w_branch, w_out, w_gate_up, w_down):
    W = HG_WIDTH
    r_q, H = w_uq.shape[1], w_uq.shape[2]
    r_kv = w_ukv.shape[1]
    w = w_in[0]
    D = w.shape[0]
    c0 = 5 * W
    w_act = jnp.concatenate([w[:, :2 * W], w[:, 4 * W:c0]], axis=1)
    w_dec = w[:, 2 * W:4 * W]
    w_qa = w[:, c0:c0 + r_q]
    w_kva = w[:, c0 + r_q:c0 + r_q + r_kv]
    c1 = c0 + r_q + r_kv
    w_kr = w[:, c1:c1 + QK_ROPE]
    w_gt = w[:, c1 + QK_ROPE:]
    zpad = lambda n: jnp.zeros((D, n), w.dtype)
    w_mla = jnp.concatenate([w_qa, w_kva, zpad(QK_NOPE), w_kr, zpad(QK_PAD - QK_HEAD)], axis=1)
    wq = jnp.transpose(w_uq[0], (1, 2, 0))
    wq_t = jnp.concatenate([wq, jnp.zeros((H, QK_PAD - QK_HEAD, r_q), wq.dtype)], axis=1)
    wq_t = wq_t.reshape(H * QK_PAD, r_q)
    wkv = jnp.transpose(w_ukv[0], (1, 0, 2))
    wk = jnp.concatenate([wkv[:, :, :QK_NOPE],
                          jnp.zeros((H, r_kv, QK_PAD - QK_NOPE), wkv.dtype)], axis=2)
    wk = jnp.transpose(wk, (1, 0, 2)).reshape(r_kv, H * QK_PAD)
    wv_t = jnp.transpose(wkv[:, :, QK_NOPE:], (0, 2, 1)).reshape(H * V_HEAD, r_kv)
    lb = jax.nn.softmax(lb_param.astype(F32), axis=0)[0]
    b16 = lambda a: a.astype(BF16)
    return dict(
        w_act=b16(w_act), w_dec=b16(w_dec), w_mla=b16(w_mla), w_gt=b16(w_gt),
        wq_t=b16(wq_t), wk=b16(wk), wv_t=b16(wv_t),
        wb0=b16(w_branch[0, 0]), wb1=b16(w_branch[0, 1]), wo=b16(w_out[0]),
        wgu=b16(w_gate_up[0]), wd=b16(w_down[0]),
        lb_fw=lb[0:1], lb_bw=lb[1:2],
    )


def _pick(n, pref):
    t = min(n, pref)
    while n % t:
        t //= 2
    return t


def _trunk(x, p, g_mix, g_onorm, g_qa, g_kva, g_ffn, g_final):
    B, S, D = x.shape
    ts = _pick(S, 512)
    act, fb, gt, o_fw, q_t, k, v_t, k_sq, q_sq = _front(x, p, g_mix, g_qa, g_kva, _rope_tables(S),
                                                        ts=ts)
    o_h = _hgrn_bwd(act, fb, p["lb_bw"], o_fw, g_onorm, ts=ts)
    k_sq = jnp.transpose(k_sq, (0, 2, 1, 3))
    o_mt, den = _attention(q_t, k, v_t, k_sq, q_sq, tq=_pick(S, 1024), tk=_pick(S, 8192))
    o_mt = lax.cond(jnp.min(den) >= DENOM_FLOOR, lambda: o_mt,
                    lambda: _attention_online(q_t, k, v_t, tq=_pick(S, 512),
                                              tk=_pick(S // 2, 1024)))
    o_mt = o_mt.reshape(B, -1, S)

    return _mix_ffn(x, o_h, o_mt, gt, p["wb0"], p["wb1"], p["wo"], g_ffn, p["wgu"], p["wd"],
                    g_final[None, :], tm=_pick(S, 512))


def kernel(x_prompt, x_sample, g_mix, w_in, lb_param, g_onorm, g_qa, w_uq, g_kva, w_ukv,
           w_branch, w_out, g_ffn, w_gate_up, w_down, g_final):
    p = _prep_weights(w_in, lb_param, w_uq, w_ukv, w_branch, w_out, w_gate_up, w_down)
    args = (p, g_mix, g_onorm, g_qa, g_kva, g_ffn, g_final)
    return (_trunk(x_prompt, *args), _trunk(x_sample, *args))
```

```python
import functools

import jax
import jax.numpy as jnp
from jax import lax
from jax.experimental import pallas as pl
from jax.experimental.pallas import tpu as pltpu

F32 = jnp.float32
BF16 = jnp.bfloat16

EPS = 1e-6
ROPE_THETA = 10000.0
HG_HEADS = 4
HG_HEAD_DIM = 128
HG_WIDTH = HG_HEADS * HG_HEAD_DIM
HG_CHUNK = 64
QK_NOPE = 64
QK_ROPE = 32
V_HEAD = 64
QK_HEAD = QK_NOPE + QK_ROPE
QK_PAD = 128
V_AUG = V_HEAD + 16
LOG2E = 1.4426950408889634
BOUND_SLACK = 1.0 + 2.0 ** -6
DENOM_FLOOR = 2.0 ** -64
OFF_ROWS = 16
VMEM_LIMIT_BYTES = 56 * 1024 * 1024

NT_DIMS = (((1,), (1,)), ((), ()))
TN_DIMS = (((0,), (0,)), ((), ()))


def _sigmoid(x):
    return 1.0 / (1.0 + jnp.exp(-x))


def _rms(x, g):
    ms = jnp.mean(x * x, axis=-1, keepdims=True)
    return x * lax.rsqrt(ms + EPS) * g


def _params(*sem):
    return pltpu.CompilerParams(dimension_semantics=sem, vmem_limit_bytes=VMEM_LIMIT_BYTES)


def _resident(shape):
    zeros = (0,) * len(shape)
    return pl.BlockSpec(shape, lambda *_: zeros, pipeline_mode=pl.Buffered(1))


def _hgrn_tile(q_of, v_of, f_of, lb, st_ref, emit, *, nc, chunk, reverse):
    C = chunk
    D = HG_HEAD_DIM
    heads = range(HG_HEADS)
    rows = lax.broadcasted_iota(jnp.int32, (C, C), 0)
    cols = lax.broadcasted_iota(jnp.int32, (C, C), 1)
    tri = (cols >= rows) if reverse else (cols <= rows)
    tri_b = tri.astype(BF16)
    mid = C // 2 if reverse else C // 2 - 1
    last = 0 if reverse else C - 1

    q_in, k_in, q_inter, k_state, v_b, decay = [], [], [], [], [], []
    for c in range(nc):
        f = lb + (1.0 - lb) * _sigmoid(f_of(c))
        g = jnp.log2(f)
        kk = 1.0 - f
        g_hi = g.astype(BF16)
        g_lo = (g - g_hi.astype(F32)).astype(BF16)
        b = (jnp.dot(tri_b, g_hi, preferred_element_type=F32)
             + jnp.dot(tri_b, g_lo, preferred_element_type=F32))
        ref = b[mid:mid + 1, :]
        b_last = b[last:last + 1, :]
        qi = q_of(c) * jnp.exp2(b - ref)
        ki = kk * jnp.exp2(ref - b)
        q_in.append(qi.astype(BF16))
        k_in.append(ki.astype(BF16))
        q_inter.append((qi * jnp.exp2(ref)).astype(BF16))
        k_state.append((ki * jnp.exp2(b_last - ref)).astype(BF16))
        v_b.append(v_of(c))
        decay.append(jnp.exp2(b_last))

    o_intra = [[None] * HG_HEADS for _ in range(nc)]
    incr = [[None] * HG_HEADS for _ in range(nc)]
    for c in range(nc):
        for h in heads:
            ls = slice(h * D, (h + 1) * D)
            att = lax.dot_general(q_in[c][:, ls], k_in[c][:, ls], NT_DIMS,
                                  preferred_element_type=F32)
            att = jnp.where(tri, att, 0.0).astype(BF16)
            o_intra[c][h] = jnp.dot(att, v_b[c][:, ls], preferred_element_type=F32)
            incr[c][h] = lax.dot_general(v_b[c][:, ls], k_state[c][:, ls], TN_DIMS,
                                         preferred_element_type=F32)

    st = [st_ref[h] for h in heads]
    for c in (range(nc - 1, -1, -1) if reverse else range(nc)):
        for h in heads:
            ls = slice(h * D, (h + 1) * D)
            o = o_intra[c][h] + lax.dot_general(q_inter[c][:, ls], st[h].astype(BF16), NT_DIMS,
                                                preferred_element_type=F32)
            st[h] = st[h] * decay[c][:, ls] + incr[c][h]
            emit(c, h, o)
    for h in heads:
        st_ref[h] = st[h]


def _mla_tile(mla, gqa_ref, gkva_ref, wq_ref, wk_ref, wv_ref,
              cos_t_ref, sin_t_ref, cosk_ref, sina_ref, sinb_ref,
              qt_ref, k_ref, vt_ref, ksq_ref, qsq_ref):
    heads = wq_ref.shape[0] // QK_PAD
    scale = QK_HEAD ** -0.5 * LOG2E
    r = gqa_ref.shape[1]
    cq = _rms(mla[:, 0:r], gqa_ref[...]).astype(BF16)
    ckv = _rms(mla[:, r:2 * r], gkva_ref[...]).astype(BF16)
    kr = mla[:, 2 * r:2 * r + QK_PAD]
    half = QK_ROPE // 2
    k_pe = (kr * cosk_ref[...]
            + pltpu.roll(kr, half, 1) * sina_ref[...]
            + pltpu.roll(kr, QK_PAD - half, 1) * sinb_ref[...])
    cos_t = cos_t_ref[...]
    sin_t = sin_t_ref[...]
    q_all = lax.dot_general(wq_ref[...], cq, NT_DIMS, preferred_element_type=F32)
    k_all = jnp.dot(ckv, wk_ref[...], preferred_element_type=F32)
    v_all = lax.dot_general(wv_ref[...], ckv, NT_DIMS, preferred_element_type=F32)
    pad_rows = vt_ref.shape[2] - V_HEAD
    ones_rows = (lax.broadcasted_iota(jnp.int32, (pad_rows, cos_t.shape[1]), 0) == 0).astype(F32)
    k_sq_rows = []
    one_lane = (lax.broadcasted_iota(jnp.int32, (1, QK_PAD), 1) == QK_HEAD).astype(F32)
    for h in range(heads):
        q_t = q_all[h * QK_PAD:(h + 1) * QK_PAD]
        x1 = q_t[QK_NOPE:QK_NOPE + half]
        x2 = q_t[QK_NOPE + half:QK_HEAD]
        q_rot = jnp.concatenate(
            [q_t[:QK_NOPE], x1 * cos_t - x2 * sin_t, x2 * cos_t + x1 * sin_t, q_t[QK_HEAD:]],
            axis=0)
        q_b = (q_rot * scale).astype(qt_ref.dtype)
        qt_ref[0, h] = q_b
        q_f = q_b.astype(F32)
        qsq_ref[0, h] = jnp.sum(q_f * q_f, axis=0, keepdims=True)
        k_b = (k_all[:, h * QK_PAD:(h + 1) * QK_PAD] + k_pe).astype(BF16)
        k_f = k_b.astype(F32)
        k_sq = jnp.sum(k_f * k_f, axis=1, keepdims=True)
        k_sq_rows.append(jnp.broadcast_to(jnp.max(k_sq, axis=0, keepdims=True), (1, QK_PAD)))
        k_ref[0, h] = (k_f + one_lane).astype(k_ref.dtype)
        v_t = v_all[h * V_HEAD:(h + 1) * V_HEAD]
        vt_ref[0, h] = jnp.concatenate([v_t, ones_rows], axis=0).astype(vt_ref.dtype)
    ksq_ref[0, 0] = jnp.concatenate(k_sq_rows, axis=0)


def _front_kernel(x_ref, g_ref, wproj_ref, lb_ref,
                  gqa_ref, gkva_ref, wq_ref, wk_ref, wv_ref,
                  cos_t_ref, sin_t_ref, cosk_ref, sina_ref, sinb_ref,
                  act_ref, fb_ref, gt_ref, ofw_ref, qt_ref, k_ref, vt_ref, ksq_ref, qsq_ref, st_ref,
                  *, chunk):
    W = HG_WIDTH
    C = chunk

    @pl.when(pl.program_id(1) == 0)
    def _():
        st_ref[...] = jnp.zeros_like(st_ref)

    h = _rms(x_ref[0], g_ref[...]).astype(BF16)
    proj = jnp.dot(h, wproj_ref[...], preferred_element_type=F32)
    n_gt = gt_ref.shape[2]
    q, i, g = proj[:, :W], proj[:, W:2 * W], proj[:, 2 * W:3 * W]
    q_b = (q * _sigmoid(q)).astype(BF16)
    i_b = i.astype(BF16)
    act_ref[0] = jnp.concatenate([q_b, i_b, (g * _sigmoid(g)).astype(BF16)], axis=1)
    dec = proj[:, 3 * W:5 * W]
    fb_ref[0] = dec[:, W:]
    gt_ref[0] = _sigmoid(proj[:, 5 * W:5 * W + n_gt]).astype(BF16)

    _mla_tile(proj[:, 5 * W + n_gt:], gqa_ref, gkva_ref,
              wq_ref, wk_ref, wv_ref, cos_t_ref, sin_t_ref, cosk_ref, sina_ref, sinb_ref,
              qt_ref, k_ref, vt_ref, ksq_ref, qsq_ref)

    def emit(c, hd, o):
        ofw_ref[0, c * C:(c + 1) * C, hd * HG_HEAD_DIM:(hd + 1) * HG_HEAD_DIM] = o

    rs = lambda c: slice(c * C, (c + 1) * C)
    _hgrn_tile(lambda c: q_b[rs(c)].astype(F32), lambda c: i_b[rs(c)], lambda c: dec[rs(c), :W],
               lb_ref[...], st_ref, emit, nc=x_ref.shape[1] // C, chunk=C, reverse=False)


def _front(x, p, g_mix, g_qa, g_kva, tabs, *, ts):
    B, S, D = x.shape
    W = HG_WIDTH
    H = p["wq_t"].shape[0] // QK_PAD
    half = QK_ROPE // 2
    tok = lambda w: pl.BlockSpec((1, ts, w), lambda b, s: (b, s, 0))
    consts = (g_mix, p["w_proj"], p["lb_fw"],
              g_qa, g_kva, p["wq_t"], p["wk"], p["wv_t"])
    lane_tab = pl.BlockSpec((half, ts), lambda b, s: (0, s))
    row_tab = pl.BlockSpec((ts, QK_PAD), lambda b, s: (s, 0))
    return pl.pallas_call(
        functools.partial(_front_kernel, chunk=HG_CHUNK),
        grid=(B, S // ts),
        in_specs=([tok(D)] + [_resident(c.shape) for c in consts]
                  + [lane_tab, lane_tab, row_tab, row_tab, row_tab]),
        out_specs=[
            tok(3 * W), tok(W), tok(p["n_gt"]), tok(W),
            pl.BlockSpec((1, H, QK_PAD, ts), lambda b, s: (b, 0, 0, s)),
            pl.BlockSpec((1, H, ts, QK_PAD), lambda b, s: (b, 0, s, 0)),
            pl.BlockSpec((1, H, V_AUG, ts), lambda b, s: (b, 0, 0, s)),
            pl.BlockSpec((1, 1, H, QK_PAD), lambda b, s: (b, s, 0, 0)),
            pl.BlockSpec((1, H, 1, ts), lambda b, s: (b, 0, 0, s)),
        ],
        out_shape=[
            jax.ShapeDtypeStruct((B, S, 3 * W), BF16),
            jax.ShapeDtypeStruct((B, S, W), F32),
            jax.ShapeDtypeStruct((B, S, p["n_gt"]), BF16),
            jax.ShapeDtypeStruct((B, S, W), F32),
            jax.ShapeDtypeStruct((B, H, QK_PAD, S), BF16),
            jax.ShapeDtypeStruct((B, H, S, QK_PAD), BF16),
            jax.ShapeDtypeStruct((B, H, V_AUG, S), BF16),
            jax.ShapeDtypeStruct((B, S // ts, H, QK_PAD), F32),
            jax.ShapeDtypeStruct((B, H, 1, S), F32),
        ],
        scratch_shapes=[pltpu.VMEM((HG_HEADS, HG_HEAD_DIM, HG_HEAD_DIM), F32)],
        compiler_params=_params("parallel", "arbitrary"),
        name="front",
    )(x, *consts, *tabs)


def _hgrn_bwd_kernel(q_ref, v_ref, f_ref, lb_ref, og_ref, ofw_ref, gon_ref, o_ref, st_ref, *,
                     chunk):
    C = chunk

    @pl.when(pl.program_id(1) == 0)
    def _():
        st_ref[...] = jnp.zeros_like(st_ref)

    def emit(c, hd, o):
        rs = slice(c * C, (c + 1) * C)
        ls = slice(hd * HG_HEAD_DIM, (hd + 1) * HG_HEAD_DIM)
        o = o + ofw_ref[0, rs, ls]
        o = _rms(o, gon_ref[:, ls]) * og_ref[0, rs, ls].astype(F32)
        o_ref[0, rs, ls] = o.astype(o_ref.dtype)

    rs = lambda c: slice(c * C, (c + 1) * C)
    _hgrn_tile(lambda c: q_ref[0, rs(c), :].astype(F32), lambda c: v_ref[0, rs(c), :],
               lambda c: f_ref[0, rs(c), :], lb_ref[...], st_ref, emit,
               nc=q_ref.shape[1] // C, chunk=C, reverse=True)


def _hgrn_bwd(act, fb, lb_row, o_fw, g_onorm, *, ts):
    B, S, W = fb.shape
    ns = S // ts
    col = lambda j: pl.BlockSpec((1, ts, W), lambda b, s: (b, ns - 1 - s, j))
    vec = pl.BlockSpec((1, W), lambda b, s: (0, 0))
    return pl.pallas_call(
        functools.partial(_hgrn_bwd_kernel, chunk=HG_CHUNK),
        grid=(B, ns),
        in_specs=[col(0), col(1), col(0), vec, col(2), col(0), vec],
        out_specs=col(0),
        out_shape=jax.ShapeDtypeStruct((B, S, W), BF16),
        scratch_shapes=[pltpu.VMEM((HG_HEADS, HG_HEAD_DIM, HG_HEAD_DIM), F32)],
        compiler_params=_params("parallel", "arbitrary"),
        name="hgrn_bwd",
    )(act, act, fb, lb_row, act, o_fw, g_onorm)


def _attn_kernel(qt_ref, k_ref, vt_ref, ksq_ref, qsq_ref, o_ref, den_ref, acc_ref, *, tk):
    q_t = qt_ref[0, 0]
    tq = q_t.shape[1]
    k_sq_max = jnp.max(jnp.max(ksq_ref[0, 0], axis=1, keepdims=True), axis=0, keepdims=True)
    bound = jnp.sqrt(qsq_ref[0, 0] * k_sq_max) * BOUND_SLACK
    off_rows = jnp.concatenate([-bound, jnp.zeros((OFF_ROWS - 1, tq), F32)], axis=0).astype(BF16)
    q_off = jnp.concatenate([q_t[:QK_HEAD], off_rows, q_t[QK_HEAD + OFF_ROWS:]], axis=0)

    def step(j, carry):
        k0 = pl.multiple_of(j * tk, tk)
        s = jnp.dot(k_ref[0, 0, pl.ds(k0, tk), :], q_off, preferred_element_type=F32)
        p = jnp.exp2(s).astype(BF16)
        acc_ref[...] += jnp.dot(vt_ref[0, 0, :, pl.ds(k0, tk)], p, preferred_element_type=F32)
        return carry

    acc_ref[...] = jnp.zeros_like(acc_ref)
    lax.fori_loop(0, k_ref.shape[2] // tk, step, 0)
    acc = acc_ref[...]
    den = acc[V_HEAD:V_HEAD + 1]
    den_ref[0, 0] = den
    o_ref[0, 0] = (acc[:V_HEAD] / den).astype(o_ref.dtype)


def _attention(q_t, k, v_t, k_sq, q_sq, *, tq, tk):
    B, H, _, S = q_t.shape
    VA = v_t.shape[2]
    per_q = lambda rows: pl.BlockSpec((1, 1, rows, tq), lambda b, h, i: (b, h, 0, i))
    whole = lambda a: pl.BlockSpec((1, 1) + a.shape[2:], lambda b, h, i: (b, h, 0, 0))
    return pl.pallas_call(
        functools.partial(_attn_kernel, tk=tk),
        grid=(B, H, S // tq),
        in_specs=[per_q(QK_PAD), whole(k), whole(v_t), whole(k_sq), per_q(1)],
        out_specs=[per_q(V_HEAD), per_q(1)],
        out_shape=[jax.ShapeDtypeStruct((B, H, V_HEAD, S), BF16),
                   jax.ShapeDtypeStruct((B, H, 1, S), F32)],
        scratch_shapes=[pltpu.VMEM((VA, tq), F32)],
        compiler_params=_params("parallel", "parallel", "arbitrary"),
        name="attention",
    )(q_t, k, v_t, k_sq, q_sq)


def _attn_online_kernel(qt_ref, k_ref, vt_ref, o_ref, s_ref, acc_ref, m_ref, *, tk):
    q_t = qt_ref[0, 0]
    nk = k_ref.shape[2] // tk

    def scores(j, slot):
        k0 = pl.multiple_of(jnp.minimum(j, nk - 1) * tk, tk)
        s_ref[slot] = jnp.dot(k_ref[0, 0, pl.ds(k0, tk), :], q_t, preferred_element_type=F32)

    def softmax_pv(j, slot):
        s = s_ref[slot]
        m_old = m_ref[...]
        m_new = jnp.maximum(m_old, jnp.max(s, axis=0, keepdims=True))
        p = jnp.exp2(s - m_new).astype(BF16)
        k0 = pl.multiple_of(j * tk, tk)
        pv = jnp.dot(vt_ref[0, 0, :, pl.ds(k0, tk)], p, preferred_element_type=F32)
        acc_ref[...] = jnp.exp2(m_old - m_new) * acc_ref[...] + pv
        m_ref[...] = m_new

    def pair(i, carry):
        j = 2 * i
        scores(j + 1, 1)
        softmax_pv(j, 0)
        scores(j + 2, 0)
        softmax_pv(j + 1, 1)
        return carry

    m_ref[...] = jnp.full_like(m_ref, -1e30)
    acc_ref[...] = jnp.zeros_like(acc_ref)
    scores(0, 0)
    lax.fori_loop(0, nk // 2, pair, 0)
    acc = acc_ref[...]
    o_ref[0, 0] = (acc[:V_HEAD] / acc[V_HEAD:V_HEAD + 1]).astype(o_ref.dtype)


def _attention_online(q_t, k, v_t, *, tq, tk):
    B, H, _, S = q_t.shape
    VA = v_t.shape[2]
    assert (S // tk) % 2 == 0
    per_q = lambda rows: pl.BlockSpec((1, 1, rows, tq), lambda b, h, i: (b, h, 0, i))
    whole = lambda a: pl.BlockSpec((1, 1) + a.shape[2:], lambda b, h, i: (b, h, 0, 0))
    return pl.pallas_call(
        functools.partial(_attn_online_kernel, tk=tk),
        grid=(B, H, S // tq),
        in_specs=[per_q(QK_PAD), whole(k), whole(v_t)],
        out_specs=per_q(V_HEAD),
        out_shape=jax.ShapeDtypeStruct((B, H, V_HEAD, S), BF16),
        scratch_shapes=[pltpu.VMEM((2, tk, tq), F32), pltpu.VMEM((VA, tq), F32),
                        pltpu.VMEM((1, tq), F32)],
        compiler_params=_params("parallel", "parallel", "arbitrary"),
        name="attention_online",
    )(q_t, k, v_t)


def _mix_ffn_kernel(x_ref, oh_ref, omt_ref, gt_ref, wb0_ref, wb1_ref, wo_ref,
                    gf_ref, wgu_ref, wd_ref, gfin_ref, y_ref):
    D = x_ref.shape[2]
    dff = wd_ref.shape[0]
    bp0 = jnp.dot(oh_ref[0], wb0_ref[...], preferred_element_type=F32)
    bp1 = lax.dot_general(omt_ref[0], wb1_ref[...], TN_DIMS, preferred_element_type=F32)
    gt = gt_ref[0].astype(F32)
    merged = gt[:, :D] * bp0 + gt[:, D:] * bp1
    x1 = x_ref[0] + jnp.dot(merged.astype(BF16), wo_ref[...], preferred_element_type=F32)
    h2 = _rms(x1, gf_ref[...]).astype(BF16)
    gu = jnp.dot(h2, wgu_ref[...], preferred_element_type=F32)
    gate = gu[:, :dff]
    act = (gate * _sigmoid(gate)) * gu[:, dff:]
    x2 = x1 + jnp.dot(act.astype(BF16), wd_ref[...], preferred_element_type=F32)
    y_ref[0] = _rms(x2, gfin_ref[...])


def _mix_ffn(x, o_h, o_mt, gt, wb0, wb1, wo, g_ffn, wgu, wd, g_final, *, tm):
    B, S, D = x.shape
    tok = lambda w: pl.BlockSpec((1, tm, w), lambda b, s: (b, s, 0))
    consts = (wb0, wb1, wo, g_ffn, wgu, wd, g_final)
    return pl.pallas_call(
        _mix_ffn_kernel,
        grid=(B, S // tm),
        in_specs=[
            tok(D), tok(o_h.shape[2]),
            pl.BlockSpec((1, o_mt.shape[1], tm), lambda b, s: (b, 0, s)),
            tok(gt.shape[2]),
        ] + [_resident(c.shape) for c in consts],
        out_specs=tok(D),
        out_shape=jax.ShapeDtypeStruct((B, S, D), F32),
        compiler_params=_params("parallel", "parallel"),
        name="mix_ffn",
    )(x, o_h, o_mt, gt, *consts)


def _rope_tables(S):
    d = QK_ROPE
    half = d // 2
    inv = ROPE_THETA ** (-jnp.arange(0, d, 2, dtype=F32) / d)
    ang = jnp.arange(S, dtype=F32)[:, None] * inv[None, :]
    cos, sin = jnp.cos(ang), jnp.sin(ang)
    z = lambda n: jnp.zeros((S, n), F32)
    tail = QK_PAD - QK_HEAD
    cosk = jnp.concatenate([z(QK_NOPE), cos, cos, z(tail)], axis=1)
    sina = jnp.concatenate([z(QK_NOPE + half), sin, z(tail)], axis=1)
    sinb = jnp.concatenate([z(QK_NOPE), -sin, z(half + tail)], axis=1)
    return cos.T, sin.T, cosk, sina, sinb


def _prep_weights(w_in, lb_param, w_uq, w_ukv, w_branch, w_out, w_gate_up, w_down):
    W = HG_WIDTH
    r_q, H = w_uq.shape[1], w_uq.shape[2]
    r_kv = w_ukv.shape[1]
    w = w_in[0]
    D = w.shape[0]
    c0 = 5 * W
    w_act = jnp.concatenate([w[:, :2 * W], w[:, 4 * W:c0]], axis=1)
    w_dec = w[:, 2 * W:4 * W]
    w_qa = w[:, c0:c0 + r_q]
    w_kva = w[:, c0 + r_q:c0 + r_q + r_kv]
    c1 = c0 + r_q + r_kv
    w_kr = w[:, c1:c1 + QK_ROPE]
    w_gt = w[:, c1 + QK_ROPE:]
    zpad = lambda n: jnp.zeros((D, n), w.dtype)
    w_mla = jnp.concatenate([w_qa, w_kva, zpad(QK_NOPE), w_kr, zpad(QK_PAD - QK_HEAD)], axis=1)
    wq = jnp.transpose(w_uq[0], (1, 2, 0))
    wq_t = jnp.concatenate([wq, jnp.zeros((H, QK_PAD - QK_HEAD, r_q), wq.dtype)], axis=1)
    wq_t = wq_t.reshape(H * QK_PAD, r_q)
    wkv = jnp.transpose(w_ukv[0], (1, 0, 2))
    wk = jnp.concatenate([wkv[:, :, :QK_NOPE],
                          jnp.zeros((H, r_kv, QK_PAD - QK_NOPE), wkv.dtype)], axis=2)
    wk = jnp.transpose(wk, (1, 0, 2)).reshape(r_kv, H * QK_PAD)
    wv_t = jnp.transpose(wkv[:, :, QK_NOPE:], (0, 2, 1)).reshape(H * V_HEAD, r_kv)
    lb = jax.nn.softmax(lb_param.astype(F32), axis=0)[0]
    b16 = lambda a: a.astype(BF16)
    return dict(
        w_proj=b16(jnp.concatenate([w_act, w_dec, w_gt, w_mla], axis=1)), n_gt=w_gt.shape[1],
        wq_t=b16(wq_t), wk=b16(wk), wv_t=b16(wv_t),
        wb0=b16(w_branch[0, 0]), wb1=b16(w_branch[0, 1]), wo=b16(w_out[0]),
        wgu=b16(w_gate_up[0]), wd=b16(w_down[0]),
        lb_fw=lb[0:1], lb_bw=lb[1:2],
    )


def _pick(n, pref):
    t = min(n, pref)
    while n % t:
        t //= 2
    return t


def _trunk(x, p, g_mix, g_onorm, g_qa, g_kva, g_ffn, g_final):
    B, S, D = x.shape
    ts = _pick(S, 512)
    act, fb, gt, o_fw, q_t, k, v_t, k_sq, q_sq = _front(x, p, g_mix, g_qa, g_kva, _rope_tables(S),
                                                        ts=ts)
    o_h = _hgrn_bwd(act, fb, p["lb_bw"], o_fw, g_onorm, ts=ts)
    k_sq = jnp.transpose(k_sq, (0, 2, 1, 3))
    o_mt, den = _attention(q_t, k, v_t, k_sq, q_sq, tq=_pick(S, 1024), tk=_pick(S, 8192))
    o_mt = lax.cond(jnp.min(den) >= DENOM_FLOOR, lambda: o_mt,
                    lambda: _attention_online(q_t, k, v_t, tq=_pick(S, 512),
                                              tk=_pick(S // 2, 1024)))
    o_mt = o_mt.reshape(B, -1, S)

    return _mix_ffn(x, o_h, o_mt, gt, p["wb0"], p["wb1"], p["wo"], g_ffn, p["wgu"], p["wd"],
                    g_final[None, :], tm=_pick(S, 512))


def kernel(x_prompt, x_sample, g_mix, w_in, lb_param, g_onorm, g_qa, w_uq, g_kva, w_ukv,
           w_branch, w_out, g_ffn, w_gate_up, w_down, g_final):
    p = _prep_weights(w_in, lb_param, w_uq, w_ukv, w_branch, w_out, w_gate_up, w_down)
    args = (p, g_mix, g_onorm, g_qa, g_kva, g_ffn, g_final)
    return (_trunk(x_prompt, *args), _trunk(x_sample, *args))
```

```python
import functools

import jax
import jax.numpy as jnp
from jax import lax
from jax.experimental import pallas as pl
from jax.experimental.pallas import tpu as pltpu

F32 = jnp.float32
BF16 = jnp.bfloat16

EPS = 1e-6
ROPE_THETA = 10000.0
HG_HEADS = 4
HG_HEAD_DIM = 128
HG_WIDTH = HG_HEADS * HG_HEAD_DIM
HG_CHUNK = 64
QK_NOPE = 64
QK_ROPE = 32
V_HEAD = 64
QK_HEAD = QK_NOPE + QK_ROPE
QK_PAD = 128
V_AUG = V_HEAD + 16
LOG2E = 1.4426950408889634
BOUND_SLACK = 1.0 + 2.0 ** -6
DENOM_FLOOR = 2.0 ** -64
OFF_ROWS = 16
VMEM_LIMIT_BYTES = 56 * 1024 * 1024

NT_DIMS = (((1,), (1,)), ((), ()))
TN_DIMS = (((0,), (0,)), ((), ()))


def _sigmoid(x):
    return 1.0 / (1.0 + jnp.exp(-x))


def _rms(x, g):
    ms = jnp.mean(x * x, axis=-1, keepdims=True)
    return x * lax.rsqrt(ms + EPS) * g


def _params(*sem):
    return pltpu.CompilerParams(dimension_semantics=sem, vmem_limit_bytes=VMEM_LIMIT_BYTES)


def _resident(shape):
    zeros = (0,) * len(shape)
    return pl.BlockSpec(shape, lambda *_: zeros, pipeline_mode=pl.Buffered(1))


def _hgrn_tile(q_of, v_of, f_of, lb, st_ref, emit, *, nc, chunk, reverse):
    C = chunk
    D = HG_HEAD_DIM
    heads = range(HG_HEADS)
    rows = lax.broadcasted_iota(jnp.int32, (C, C), 0)
    cols = lax.broadcasted_iota(jnp.int32, (C, C), 1)
    tri = (cols >= rows) if reverse else (cols <= rows)
    tri_b = tri.astype(BF16)
    mid = C // 2 if reverse else C // 2 - 1
    last = 0 if reverse else C - 1

    q_in, k_in, q_inter, k_state, v_b, decay = [], [], [], [], [], []
    for c in range(nc):
        f = lb + (1.0 - lb) * _sigmoid(f_of(c))
        g = jnp.log2(f)
        kk = 1.0 - f
        g_hi = g.astype(BF16)
        g_lo = (g - g_hi.astype(F32)).astype(BF16)
        b = (jnp.dot(tri_b, g_hi, preferred_element_type=F32)
             + jnp.dot(tri_b, g_lo, preferred_element_type=F32))
        ref = b[mid:mid + 1, :]
        b_last = b[last:last + 1, :]
        qi = q_of(c) * jnp.exp2(b - ref)
        ki = kk * jnp.exp2(ref - b)
        q_in.append(qi.astype(BF16))
        k_in.append(ki.astype(BF16))
        q_inter.append((qi * jnp.exp2(ref)).astype(BF16))
        k_state.append((ki * jnp.exp2(b_last - ref)).astype(BF16))
        v_b.append(v_of(c))
        decay.append(jnp.exp2(b_last))

    o_intra = [[None] * HG_HEADS for _ in range(nc)]
    incr = [[None] * HG_HEADS for _ in range(nc)]
    for c in range(nc):
        for h in heads:
            ls = slice(h * D, (h + 1) * D)
            att = lax.dot_general(q_in[c][:, ls], k_in[c][:, ls], NT_DIMS,
                                  preferred_element_type=F32)
            att = jnp.where(tri, att, 0.0).astype(BF16)
            o_intra[c][h] = jnp.dot(att, v_b[c][:, ls], preferred_element_type=F32)
            incr[c][h] = lax.dot_general(v_b[c][:, ls], k_state[c][:, ls], TN_DIMS,
                                         preferred_element_type=F32)

    st = [st_ref[h] for h in heads]
    for c in (range(nc - 1, -1, -1) if reverse else range(nc)):
        for h in heads:
            ls = slice(h * D, (h + 1) * D)
            o = o_intra[c][h] + lax.dot_general(q_inter[c][:, ls], st[h].astype(BF16), NT_DIMS,
                                                preferred_element_type=F32)
            st[h] = st[h] * decay[c][:, ls] + incr[c][h]
            emit(c, h, o)
    for h in heads:
        st_ref[h] = st[h]


def _mla_tile(mla, gqa_ref, gkva_ref, wq_ref, wk_ref, wv_ref,
              cos_t_ref, sin_t_ref, cosk_ref, sina_ref, sinb_ref,
              qt_ref, k_ref, vt_ref, ksq_ref, qsq_ref):
    heads = wq_ref.shape[0] // QK_PAD
    scale = QK_HEAD ** -0.5 * LOG2E
    r = gqa_ref.shape[1]
    cq = _rms(mla[:, 0:r], gqa_ref[...]).astype(BF16)
    ckv = _rms(mla[:, r:2 * r], gkva_ref[...]).astype(BF16)
    kr = mla[:, 2 * r:2 * r + QK_PAD]
    half = QK_ROPE // 2
    k_pe = (kr * cosk_ref[...]
            + pltpu.roll(kr, half, 1) * sina_ref[...]
            + pltpu.roll(kr, QK_PAD - half, 1) * sinb_ref[...])
    cos_t = cos_t_ref[...]
    sin_t = sin_t_ref[...]
    q_all = lax.dot_general(wq_ref[...], cq, NT_DIMS, preferred_element_type=F32)
    k_all = jnp.dot(ckv, wk_ref[...], preferred_element_type=F32)
    v_all = lax.dot_general(wv_ref[...], ckv, NT_DIMS, preferred_element_type=F32)
    pad_rows = vt_ref.shape[2] - V_HEAD
    ones_rows = (lax.broadcasted_iota(jnp.int32, (pad_rows, cos_t.shape[1]), 0) == 0).astype(F32)
    k_sq_rows = []
    one_lane = (lax.broadcasted_iota(jnp.int32, (1, QK_PAD), 1) == QK_HEAD).astype(F32)
    for h in range(heads):
        q_t = q_all[h * QK_PAD:(h + 1) * QK_PAD]
        x1 = q_t[QK_NOPE:QK_NOPE + half]
        x2 = q_t[QK_NOPE + half:QK_HEAD]
        q_rot = jnp.concatenate(
            [q_t[:QK_NOPE], x1 * cos_t - x2 * sin_t, x2 * cos_t + x1 * sin_t, q_t[QK_HEAD:]],
            axis=0)
        q_b = (q_rot * scale).astype(qt_ref.dtype)
        qt_ref[0, h] = q_b
        q_f = q_b.astype(F32)
        qsq_ref[0, h] = jnp.sum(q_f * q_f, axis=0, keepdims=True)
        k_b = (k_all[:, h * QK_PAD:(h + 1) * QK_PAD] + k_pe).astype(BF16)
        k_f = k_b.astype(F32)
        k_sq = jnp.sum(k_f * k_f, axis=1, keepdims=True)
        k_sq_rows.append(jnp.broadcast_to(jnp.max(k_sq, axis=0, keepdims=True), (1, QK_PAD)))
        k_ref[0, h] = (k_f + one_lane).astype(k_ref.dtype)
        v_t = v_all[h * V_HEAD:(h + 1) * V_HEAD]
        vt_ref[0, h] = jnp.concatenate([v_t, ones_rows], axis=0).astype(vt_ref.dtype)
    ksq_ref[0, 0] = jnp.concatenate(k_sq_rows, axis=0)


def _front_kernel(x_ref, g_ref, wproj_ref, lb_ref,
                  gqa_ref, gkva_ref, wq_ref, wk_ref, wv_ref,
                  cos_t_ref, sin_t_ref, cosk_ref, sina_ref, sinb_ref,
                  act_ref, fb_ref, gt_ref, ofw_ref, qt_ref, k_ref, vt_ref, ksq_ref, qsq_ref, st_ref,
                  *, chunk):
    W = HG_WIDTH
    C = chunk

    @pl.when(pl.program_id(1) == 0)
    def _():
        st_ref[...] = jnp.zeros_like(st_ref)

    h = _rms(x_ref[0], g_ref[...]).astype(BF16)
    proj = jnp.dot(h, wproj_ref[...], preferred_element_type=F32)
    n_gt = gt_ref.shape[2]
    q, i, g = proj[:, :W], proj[:, W:2 * W], proj[:, 2 * W:3 * W]
    q_b = (q * _sigmoid(q)).astype(BF16)
    i_b = i.astype(BF16)
    act_ref[0] = jnp.concatenate([q_b, i_b, (g * _sigmoid(g)).astype(BF16)], axis=1)
    dec = proj[:, 3 * W:5 * W]
    fb_ref[0] = dec[:, W:]
    gt_ref[0] = _sigmoid(proj[:, 5 * W:5 * W + n_gt]).astype(BF16)

    _mla_tile(proj[:, 5 * W + n_gt:], gqa_ref, gkva_ref,
              wq_ref, wk_ref, wv_ref, cos_t_ref, sin_t_ref, cosk_ref, sina_ref, sinb_ref,
              qt_ref, k_ref, vt_ref, ksq_ref, qsq_ref)

    def emit(c, hd, o):
        ofw_ref[0, c * C:(c + 1) * C, hd * HG_HEAD_DIM:(hd + 1) * HG_HEAD_DIM] = o

    rs = lambda c: slice(c * C, (c + 1) * C)
    _hgrn_tile(lambda c: q_b[rs(c)].astype(F32), lambda c: i_b[rs(c)], lambda c: dec[rs(c), :W],
               lb_ref[...], st_ref, emit, nc=x_ref.shape[1] // C, chunk=C, reverse=False)


def _front(x, p, g_mix, g_qa, g_kva, tabs, *, ts):
    B, S, D = x.shape
    W = HG_WIDTH
    H = p["wq_t"].shape[0] // QK_PAD
    half = QK_ROPE // 2
    tok = lambda w: pl.BlockSpec((1, ts, w), lambda b, s: (b, s, 0))
    consts = (g_mix, p["w_proj"], p["lb_fw"],
              g_qa, g_kva, p["wq_t"], p["wk"], p["wv_t"])
    lane_tab = pl.BlockSpec((half, ts), lambda b, s: (0, s))
    row_tab = pl.BlockSpec((ts, QK_PAD), lambda b, s: (s, 0))
    return pl.pallas_call(
        functools.partial(_front_kernel, chunk=HG_CHUNK),
        grid=(B, S // ts),
        in_specs=([tok(D)] + [_resident(c.shape) for c in consts]
                  + [lane_tab, lane_tab, row_tab, row_tab, row_tab]),
        out_specs=[
            tok(3 * W), tok(W), tok(p["n_gt"]), tok(W),
            pl.BlockSpec((1, H, QK_PAD, ts), lambda b, s: (b, 0, 0, s)),
            pl.BlockSpec((1, H, ts, QK_PAD), lambda b, s: (b, 0, s, 0)),
            pl.BlockSpec((1, H, V_AUG, ts), lambda b, s: (b, 0, 0, s)),
            pl.BlockSpec((1, 1, H, QK_PAD), lambda b, s: (b, s, 0, 0)),
            pl.BlockSpec((1, H, 1, ts), lambda b, s: (b, 0, 0, s)),
        ],
        out_shape=[
            jax.ShapeDtypeStruct((B, S, 3 * W), BF16),
            jax.ShapeDtypeStruct((B, S, W), F32),
            jax.ShapeDtypeStruct((B, S, p["n_gt"]), BF16),
            jax.ShapeDtypeStruct((B, S, W), F32),
            jax.ShapeDtypeStruct((B, H, QK_PAD, S), BF16),
            jax.ShapeDtypeStruct((B, H, S, QK_PAD), BF16),
            jax.ShapeDtypeStruct((B, H, V_AUG, S), BF16),
            jax.ShapeDtypeStruct((B, S // ts, H, QK_PAD), F32),
            jax.ShapeDtypeStruct((B, H, 1, S), F32),
        ],
        scratch_shapes=[pltpu.VMEM((HG_HEADS, HG_HEAD_DIM, HG_HEAD_DIM), F32)],
        compiler_params=_params("parallel", "arbitrary"),
        name="front",
    )(x, *consts, *tabs)


def _hgrn_bwd_kernel(q_ref, v_ref, f_ref, lb_ref, og_ref, ofw_ref, gon_ref, o_ref, st_ref, *,
                     chunk):
    C = chunk

    @pl.when(pl.program_id(1) == 0)
    def _():
        st_ref[...] = jnp.zeros_like(st_ref)

    def emit(c, hd, o):
        rs = slice(c * C, (c + 1) * C)
        ls = slice(hd * HG_HEAD_DIM, (hd + 1) * HG_HEAD_DIM)
        o = o + ofw_ref[0, rs, ls]
        o = _rms(o, gon_ref[:, ls]) * og_ref[0, rs, ls].astype(F32)
        o_ref[0, rs, ls] = o.astype(o_ref.dtype)

    rs = lambda c: slice(c * C, (c + 1) * C)
    _hgrn_tile(lambda c: q_ref[0, rs(c), :].astype(F32), lambda c: v_ref[0, rs(c), :],
               lambda c: f_ref[0, rs(c), :], lb_ref[...], st_ref, emit,
               nc=q_ref.shape[1] // C, chunk=C, reverse=True)


def _hgrn_bwd(act, fb, lb_row, o_fw, g_onorm, *, ts):
    B, S, W = fb.shape
    ns = S // ts
    col = lambda j: pl.BlockSpec((1, ts, W), lambda b, s: (b, ns - 1 - s, j))
    vec = pl.BlockSpec((1, W), lambda b, s: (0, 0))
    return pl.pallas_call(
        functools.partial(_hgrn_bwd_kernel, chunk=HG_CHUNK),
        grid=(B, ns),
        in_specs=[col(0), col(1), col(0), vec, col(2), col(0), vec],
        out_specs=col(0),
        out_shape=jax.ShapeDtypeStruct((B, S, W), BF16),
        scratch_shapes=[pltpu.VMEM((HG_HEADS, HG_HEAD_DIM, HG_HEAD_DIM), F32)],
        compiler_params=_params("parallel", "arbitrary"),
        name="hgrn_bwd",
    )(act, act, fb, lb_row, act, o_fw, g_onorm)


ROW_BLOCK = 512
PV_LAG = 2


def _exact_zero_rows(x, shape):
    bits = pltpu.bitcast(x[0:8, 0:shape[1]], jnp.uint32)
    zero = lax.shift_right_logical(lax.shift_right_logical(bits, jnp.uint32(16)), jnp.uint32(16))
    zero_row = zero[0:1].astype(F32)
    return jnp.broadcast_to(zero_row, shape).astype(BF16)


def _attn_kernel(qt_ref, k_ref, vt_ref, ksq_ref, qsq_ref, o_ref, den_ref, acc_ref, *, tk):
    q_t = qt_ref[0, 0]
    tq = q_t.shape[1]
    k_sq_max = jnp.max(jnp.max(ksq_ref[0, 0], axis=1, keepdims=True), axis=0, keepdims=True)
    bound = jnp.sqrt(qsq_ref[0, 0] * k_sq_max) * BOUND_SLACK
    off_rows = jnp.concatenate([-bound, jnp.zeros((OFF_ROWS - 1, tq), F32)], axis=0).astype(BF16)
    q_off = jnp.concatenate([q_t[:QK_HEAD], off_rows, q_t[QK_HEAD + OFF_ROWS:]], axis=0)

    def step(j, carry):
        parts = []
        acc = None
        for r in range(tk // ROW_BLOCK):
            k0 = pl.multiple_of(j * tk + r * ROW_BLOCK, ROW_BLOCK)
            k_r = k_ref[0, 0, pl.ds(k0, ROW_BLOCK), :]
            if r >= PV_LAG:
                k_r = k_r + _exact_zero_rows(parts[r - PV_LAG], k_r.shape)
            s = jnp.dot(k_r, q_off, preferred_element_type=F32)
            p = jnp.exp2(s).astype(BF16)
            part = jnp.dot(vt_ref[0, 0, :, pl.ds(k0, ROW_BLOCK)], p, preferred_element_type=F32)
            parts.append(part)
            acc = part if acc is None else acc + part
        acc_ref[...] += acc
        return carry

    acc_ref[...] = jnp.zeros_like(acc_ref)
    lax.fori_loop(0, k_ref.shape[2] // tk, step, 0)
    acc = acc_ref[...]
    den = acc[V_HEAD:V_HEAD + 1]
    den_ref[0, 0] = den
    o_ref[0, 0] = (acc[:V_HEAD] / den).astype(o_ref.dtype)


def _attention(q_t, k, v_t, k_sq, q_sq, *, tq, tk):
    B, H, _, S = q_t.shape
    VA = v_t.shape[2]
    per_q = lambda rows: pl.BlockSpec((1, 1, rows, tq), lambda b, h, i: (b, h, 0, i))
    whole = lambda a: pl.BlockSpec((1, 1) + a.shape[2:], lambda b, h, i: (b, h, 0, 0))
    return pl.pallas_call(
        functools.partial(_attn_kernel, tk=tk),
        grid=(B, H, S // tq),
        in_specs=[per_q(QK_PAD), whole(k), whole(v_t), whole(k_sq), per_q(1)],
        out_specs=[per_q(V_HEAD), per_q(1)],
        out_shape=[jax.ShapeDtypeStruct((B, H, V_HEAD, S), BF16),
                   jax.ShapeDtypeStruct((B, H, 1, S), F32)],
        scratch_shapes=[pltpu.VMEM((VA, tq), F32)],
        compiler_params=_params("parallel", "parallel", "arbitrary"),
        name="attention",
    )(q_t, k, v_t, k_sq, q_sq)


def _attn_online_kernel(qt_ref, k_ref, vt_ref, o_ref, s_ref, acc_ref, m_ref, *, tk):
    q_t = qt_ref[0, 0]
    nk = k_ref.shape[2] // tk

    def scores(j, slot):
        k0 = pl.multiple_of(jnp.minimum(j, nk - 1) * tk, tk)
        s_ref[slot] = jnp.dot(k_ref[0, 0, pl.ds(k0, tk), :], q_t, preferred_element_type=F32)

    def softmax_pv(j, slot):
        s = s_ref[slot]
        m_old = m_ref[...]
        m_new = jnp.maximum(m_old, jnp.max(s, axis=0, keepdims=True))
        p = jnp.exp2(s - m_new).astype(BF16)
        k0 = pl.multiple_of(j * tk, tk)
        pv = jnp.dot(vt_ref[0, 0, :, pl.ds(k0, tk)], p, preferred_element_type=F32)
        acc_ref[...] = jnp.exp2(m_old - m_new) * acc_ref[...] + pv
        m_ref[...] = m_new

    def pair(i, carry):
        j = 2 * i
        scores(j + 1, 1)
        softmax_pv(j, 0)
        scores(j + 2, 0)
        softmax_pv(j + 1, 1)
        return carry

    m_ref[...] = jnp.full_like(m_ref, -1e30)
    acc_ref[...] = jnp.zeros_like(acc_ref)
    scores(0, 0)
    lax.fori_loop(0, nk // 2, pair, 0)
    acc = acc_ref[...]
    o_ref[0, 0] = (acc[:V_HEAD] / acc[V_HEAD:V_HEAD + 1]).astype(o_ref.dtype)


def _attention_online(q_t, k, v_t, *, tq, tk):
    B, H, _, S = q_t.shape
    VA = v_t.shape[2]
    assert (S // tk) % 2 == 0
    per_q = lambda rows: pl.BlockSpec((1, 1, rows, tq), lambda b, h, i: (b, h, 0, i))
    whole = lambda a: pl.BlockSpec((1, 1) + a.shape[2:], lambda b, h, i: (b, h, 0, 0))
    return pl.pallas_call(
        functools.partial(_attn_online_kernel, tk=tk),
        grid=(B, H, S // tq),
        in_specs=[per_q(QK_PAD), whole(k), whole(v_t)],
        out_specs=per_q(V_HEAD),
        out_shape=jax.ShapeDtypeStruct((B, H, V_HEAD, S), BF16),
        scratch_shapes=[pltpu.VMEM((2, tk, tq), F32), pltpu.VMEM((VA, tq), F32),
                        pltpu.VMEM((1, tq), F32)],
        compiler_params=_params("parallel", "parallel", "arbitrary"),
        name="attention_online",
    )(q_t, k, v_t)


def _mix_ffn_kernel(x_ref, oh_ref, omt_ref, gt_ref, wb0_ref, wb1_ref, wo_ref,
                    gf_ref, wgu_ref, wd_ref, gfin_ref, y_ref):
    D = x_ref.shape[2]
    dff = wd_ref.shape[0]
    bp0 = jnp.dot(oh_ref[0], wb0_ref[...], preferred_element_type=F32)
    bp1 = lax.dot_general(omt_ref[0], wb1_ref[...], TN_DIMS, preferred_element_type=F32)
    gt = gt_ref[0].astype(F32)
    merged = gt[:, :D] * bp0 + gt[:, D:] * bp1
    x1 = x_ref[0] + jnp.dot(merged.astype(BF16), wo_ref[...], preferred_element_type=F32)
    h2 = _rms(x1, gf_ref[...]).astype(BF16)
    gu = jnp.dot(h2, wgu_ref[...], preferred_element_type=F32)
    gate = gu[:, :dff]
    act = (gate * _sigmoid(gate)) * gu[:, dff:]
    x2 = x1 + jnp.dot(act.astype(BF16), wd_ref[...], preferred_element_type=F32)
    y_ref[0] = _rms(x2, gfin_ref[...])


def _mix_ffn(x, o_h, o_mt, gt, wb0, wb1, wo, g_ffn, wgu, wd, g_final, *, tm):
    B, S, D = x.shape
    tok = lambda w: pl.BlockSpec((1, tm, w), lambda b, s: (b, s, 0))
    consts = (wb0, wb1, wo, g_ffn, wgu, wd, g_final)
    return pl.pallas_call(
        _mix_ffn_kernel,
        grid=(B, S // tm),
        in_specs=[
            tok(D), tok(o_h.shape[2]),
            pl.BlockSpec((1, o_mt.shape[1], tm), lambda b, s: (b, 0, s)),
            tok(gt.shape[2]),
        ] + [_resident(c.shape) for c in consts],
        out_specs=tok(D),
        out_shape=jax.ShapeDtypeStruct((B, S, D), F32),
        compiler_params=_params("parallel", "parallel"),
        name="mix_ffn",
    )(x, o_h, o_mt, gt, *consts)


def _rope_tables(S):
    d = QK_ROPE
    half = d // 2
    inv = ROPE_THETA ** (-jnp.arange(0, d, 2, dtype=F32) / d)
    ang = jnp.arange(S, dtype=F32)[:, None] * inv[None, :]
    cos, sin = jnp.cos(ang), jnp.sin(ang)
    z = lambda n: jnp.zeros((S, n), F32)
    tail = QK_PAD - QK_HEAD
    cosk = jnp.concatenate([z(QK_NOPE), cos, cos, z(tail)], axis=1)
    sina = jnp.concatenate([z(QK_NOPE + half), sin, z(tail)], axis=1)
    sinb = jnp.concatenate([z(QK_NOPE), -sin, z(half + tail)], axis=1)
    return cos.T, sin.T, cosk, sina, sinb


def _prep_weights(w_in, lb_param, w_uq, w_ukv, w_branch, w_out, w_gate_up, w_down):
    W = HG_WIDTH
    r_q, H = w_uq.shape[1], w_uq.shape[2]
    r_kv = w_ukv.shape[1]
    w = w_in[0]
    D = w.shape[0]
    c0 = 5 * W
    w_act = jnp.concatenate([w[:, :2 * W], w[:, 4 * W:c0]], axis=1)
    w_dec = w[:, 2 * W:4 * W]
    w_qa = w[:, c0:c0 + r_q]
    w_kva = w[:, c0 + r_q:c0 + r_q + r_kv]
    c1 = c0 + r_q + r_kv
    w_kr = w[:, c1:c1 + QK_ROPE]
    w_gt = w[:, c1 + QK_ROPE:]
    zpad = lambda n: jnp.zeros((D, n), w.dtype)
    w_mla = jnp.concatenate([w_qa, w_kva, zpad(QK_NOPE), w_kr, zpad(QK_PAD - QK_HEAD)], axis=1)
    wq = jnp.transpose(w_uq[0], (1, 2, 0))
    wq_t = jnp.concatenate([wq, jnp.zeros((H, QK_PAD - QK_HEAD, r_q), wq.dtype)], axis=1)
    wq_t = wq_t.reshape(H * QK_PAD, r_q)
    wkv = jnp.transpose(w_ukv[0], (1, 0, 2))
    wk = jnp.concatenate([wkv[:, :, :QK_NOPE],
                          jnp.zeros((H, r_kv, QK_PAD - QK_NOPE), wkv.dtype)], axis=2)
    wk = jnp.transpose(wk, (1, 0, 2)).reshape(r_kv, H * QK_PAD)
    wv_t = jnp.transpose(wkv[:, :, QK_NOPE:], (0, 2, 1)).reshape(H * V_HEAD, r_kv)
    lb = jax.nn.softmax(lb_param.astype(F32), axis=0)[0]
    b16 = lambda a: a.astype(BF16)
    return dict(
        w_proj=b16(jnp.concatenate([w_act, w_dec, w_gt, w_mla], axis=1)), n_gt=w_gt.shape[1],
        wq_t=b16(wq_t), wk=b16(wk), wv_t=b16(wv_t),
        wb0=b16(w_branch[0, 0]), wb1=b16(w_branch[0, 1]), wo=b16(w_out[0]),
        wgu=b16(w_gate_up[0]), wd=b16(w_down[0]),
        lb_fw=lb[0:1], lb_bw=lb[1:2],
    )


def _pick(n, pref):
    t = min(n, pref)
    while n % t:
        t //= 2
    return t


def _trunk(x, p, g_mix, g_onorm, g_qa, g_kva, g_ffn, g_final):
    B, S, D = x.shape
    ts = _pick(S, 512)
    act, fb, gt, o_fw, q_t, k, v_t, k_sq, q_sq = _front(x, p, g_mix, g_qa, g_kva, _rope_tables(S),
                                                        ts=ts)
    o_h = _hgrn_bwd(act, fb, p["lb_bw"], o_fw, g_onorm, ts=ts)
    k_sq = jnp.transpose(k_sq, (0, 2, 1, 3))
    o_mt, den = _attention(q_t, k, v_t, k_sq, q_sq, tq=_pick(S, 1024), tk=_pick(S, 8192))
    o_mt = lax.cond(jnp.min(den) >= DENOM_FLOOR, lambda: o_mt,
                    lambda: _attention_online(q_t, k, v_t, tq=_pick(S, 512),
                                              tk=_pick(S // 2, 1024)))
    o_mt = o_mt.reshape(B, -1, S)

    return _mix_ffn(x, o_h, o_mt, gt, p["wb0"], p["wb1"], p["wo"], g_ffn, p["wgu"], p["wd"],
                    g_final[None, :], tm=_pick(S, 512))


def kernel(x_prompt, x_sample, g_mix, w_in, lb_param, g_onorm, g_qa, w_uq, g_kva, w_ukv,
           w_branch, w_out, g_ffn, w_gate_up, w_down, g_final):
    p = _prep_weights(w_in, lb_param, w_uq, w_ukv, w_branch, w_out, w_gate_up, w_down)
    args = (p, g_mix, g_onorm, g_qa, g_kva, g_ffn, g_final)
    return (_trunk(x_prompt, *args), _trunk(x_sample, *args))
```

```python
import functools

import jax
import jax.numpy as jnp
from jax import lax
from jax.experimental import pallas as pl
from jax.experimental.pallas import tpu as pltpu

F32 = jnp.float32
BF16 = jnp.bfloat16

EPS = 1e-6
ROPE_THETA = 10000.0
HG_HEADS = 4
HG_HEAD_DIM = 128
HG_WIDTH = HG_HEADS * HG_HEAD_DIM
HG_CHUNK = 64
QK_NOPE = 64
QK_ROPE = 32
V_HEAD = 64
QK_HEAD = QK_NOPE + QK_ROPE
QK_PAD = 128
V_AUG = V_HEAD + 16
LOG2E = 1.4426950408889634
BOUND_SLACK = 1.0 + 2.0 ** -6
DENOM_FLOOR = 2.0 ** -64
OFF_ROWS = 16
VMEM_LIMIT_BYTES = 56 * 1024 * 1024

NT_DIMS = (((1,), (1,)), ((), ()))
TN_DIMS = (((0,), (0,)), ((), ()))


def _sigmoid(x):
    return 1.0 / (1.0 + jnp.exp(-x))


def _rms(x, g):
    ms = jnp.mean(x * x, axis=-1, keepdims=True)
    return x * lax.rsqrt(ms + EPS) * g


def _params(*sem):
    return pltpu.CompilerParams(dimension_semantics=sem, vmem_limit_bytes=VMEM_LIMIT_BYTES)


def _resident(shape):
    zeros = (0,) * len(shape)
    return pl.BlockSpec(shape, lambda *_: zeros, pipeline_mode=pl.Buffered(1))


def _hgrn_tile(q_of, v_of, f_of, lb, st_ref, emit, *, nc, chunk, reverse):
    C = chunk
    D = HG_HEAD_DIM
    heads = range(HG_HEADS)
    rows = lax.broadcasted_iota(jnp.int32, (C, C), 0)
    cols = lax.broadcasted_iota(jnp.int32, (C, C), 1)
    tri = (cols >= rows) if reverse else (cols <= rows)
    tri_b = tri.astype(BF16)
    mid = C // 2 if reverse else C // 2 - 1
    last = 0 if reverse else C - 1

    q_in, k_in, q_inter, k_state, v_b, decay = [], [], [], [], [], []
    for c in range(nc):
        f = lb + (1.0 - lb) * _sigmoid(f_of(c))
        g = jnp.log2(f)
        kk = 1.0 - f
        g_hi = g.astype(BF16)
        g_lo = (g - g_hi.astype(F32)).astype(BF16)
        b = (jnp.dot(tri_b, g_hi, preferred_element_type=F32)
             + jnp.dot(tri_b, g_lo, preferred_element_type=F32))
        ref = b[mid:mid + 1, :]
        b_last = b[last:last + 1, :]
        qi = q_of(c) * jnp.exp2(b - ref)
        ki = kk * jnp.exp2(ref - b)
        q_in.append(qi.astype(BF16))
        k_in.append(ki.astype(BF16))
        q_inter.append((qi * jnp.exp2(ref)).astype(BF16))
        k_state.append((ki * jnp.exp2(b_last - ref)).astype(BF16))
        v_b.append(v_of(c))
        decay.append(jnp.exp2(b_last))

    o_intra = [[None] * HG_HEADS for _ in range(nc)]
    incr = [[None] * HG_HEADS for _ in range(nc)]
    for c in range(nc):
        for h in heads:
            ls = slice(h * D, (h + 1) * D)
            att = lax.dot_general(q_in[c][:, ls], k_in[c][:, ls], NT_DIMS,
                                  preferred_element_type=F32)
            att = jnp.where(tri, att, 0.0).astype(BF16)
            o_intra[c][h] = jnp.dot(att, v_b[c][:, ls], preferred_element_type=F32)
            incr[c][h] = lax.dot_general(v_b[c][:, ls], k_state[c][:, ls], TN_DIMS,
                                         preferred_element_type=F32)

    st = [st_ref[h] for h in heads]
    for c in (range(nc - 1, -1, -1) if reverse else range(nc)):
        for h in heads:
            ls = slice(h * D, (h + 1) * D)
            o = o_intra[c][h] + lax.dot_general(q_inter[c][:, ls], st[h].astype(BF16), NT_DIMS,
                                                preferred_element_type=F32)
            st[h] = st[h] * decay[c][:, ls] + incr[c][h]
            emit(c, h, o)
    for h in heads:
        st_ref[h] = st[h]


def _mla_tile(mla, gqa_ref, gkva_ref, wq_ref, wk_ref, wv_ref,
              cos_t_ref, sin_t_ref, cosk_ref, sina_ref, sinb_ref,
              qt_ref, k_ref, vt_ref, ksq_ref, qsq_ref):
    heads = wq_ref.shape[0] // QK_PAD
    scale = QK_HEAD ** -0.5 * LOG2E
    r = gqa_ref.shape[1]
    cq = _rms(mla[:, 0:r], gqa_ref[...]).astype(BF16)
    ckv = _rms(mla[:, r:2 * r], gkva_ref[...]).astype(BF16)
    kr = mla[:, 2 * r:2 * r + QK_PAD]
    half = QK_ROPE // 2
    k_pe = (kr * cosk_ref[...]
            + pltpu.roll(kr, half, 1) * sina_ref[...]
            + pltpu.roll(kr, QK_PAD - half, 1) * sinb_ref[...])
    cos_t = cos_t_ref[...]
    sin_t = sin_t_ref[...]
    q_all = lax.dot_general(wq_ref[...], cq, NT_DIMS, preferred_element_type=F32)
    k_all = jnp.dot(ckv, wk_ref[...], preferred_element_type=F32)
    v_all = lax.dot_general(wv_ref[...], ckv, NT_DIMS, preferred_element_type=F32)
    pad_rows = vt_ref.shape[2] - V_HEAD
    ones_rows = (lax.broadcasted_iota(jnp.int32, (pad_rows, cos_t.shape[1]), 0) == 0).astype(F32)
    k_sq_rows = []
    one_lane = (lax.broadcasted_iota(jnp.int32, (1, QK_PAD), 1) == QK_HEAD).astype(F32)
    for h in range(heads):
        q_t = q_all[h * QK_PAD:(h + 1) * QK_PAD]
        x1 = q_t[QK_NOPE:QK_NOPE + half]
        x2 = q_t[QK_NOPE + half:QK_HEAD]
        q_rot = jnp.concatenate(
            [q_t[:QK_NOPE], x1 * cos_t - x2 * sin_t, x2 * cos_t + x1 * sin_t, q_t[QK_HEAD:]],
            axis=0)
        q_b = (q_rot * scale).astype(qt_ref.dtype)
        qt_ref[0, h] = q_b
        q_f = q_b.astype(F32)
        qsq_ref[0, h] = jnp.sum(q_f * q_f, axis=0, keepdims=True)
        k_b = (k_all[:, h * QK_PAD:(h + 1) * QK_PAD] + k_pe).astype(BF16)
        k_f = k_b.astype(F32)
        k_sq = jnp.sum(k_f * k_f, axis=1, keepdims=True)
        k_sq_rows.append(jnp.broadcast_to(jnp.max(k_sq, axis=0, keepdims=True), (1, QK_PAD)))
        k_ref[0, h] = (k_f + one_lane).astype(k_ref.dtype)
        v_t = v_all[h * V_HEAD:(h + 1) * V_HEAD]
        vt_ref[0, h] = jnp.concatenate([v_t, ones_rows], axis=0).astype(vt_ref.dtype)
    ksq_ref[0, 0] = jnp.concatenate(k_sq_rows, axis=0)


def _front_kernel(x_ref, g_ref, wproj_ref, lb_ref,
                  gqa_ref, gkva_ref, wq_ref, wk_ref, wv_ref,
                  cos_t_ref, sin_t_ref, cosk_ref, sina_ref, sinb_ref,
                  act_ref, fb_ref, gt_ref, ofw_ref, qt_ref, k_ref, vt_ref, ksq_ref, qsq_ref, st_ref,
                  *, chunk):
    W = HG_WIDTH
    C = chunk

    @pl.when(pl.program_id(1) == 0)
    def _():
        st_ref[...] = jnp.zeros_like(st_ref)

    h = _rms(x_ref[0], g_ref[...]).astype(BF16)
    proj = jnp.dot(h, wproj_ref[...], preferred_element_type=F32)
    n_gt = gt_ref.shape[2]
    q, i, g = proj[:, :W], proj[:, W:2 * W], proj[:, 2 * W:3 * W]
    q_b = (q * _sigmoid(q)).astype(BF16)
    i_b = i.astype(BF16)
    act_ref[0] = jnp.concatenate([q_b, i_b, (g * _sigmoid(g)).astype(BF16)], axis=1)
    dec = proj[:, 3 * W:5 * W]
    fb_ref[0] = dec[:, W:]
    gt_ref[0] = _sigmoid(proj[:, 5 * W:5 * W + n_gt]).astype(BF16)

    _mla_tile(proj[:, 5 * W + n_gt:], gqa_ref, gkva_ref,
              wq_ref, wk_ref, wv_ref, cos_t_ref, sin_t_ref, cosk_ref, sina_ref, sinb_ref,
              qt_ref, k_ref, vt_ref, ksq_ref, qsq_ref)

    def emit(c, hd, o):
        ofw_ref[0, c * C:(c + 1) * C, hd * HG_HEAD_DIM:(hd + 1) * HG_HEAD_DIM] = o

    rs = lambda c: slice(c * C, (c + 1) * C)
    _hgrn_tile(lambda c: q_b[rs(c)].astype(F32), lambda c: i_b[rs(c)], lambda c: dec[rs(c), :W],
               lb_ref[...], st_ref, emit, nc=x_ref.shape[1] // C, chunk=C, reverse=False)


def _front(x, p, g_mix, g_qa, g_kva, tabs, *, ts):
    B, S, D = x.shape
    W = HG_WIDTH
    H = p["wq_t"].shape[0] // QK_PAD
    half = QK_ROPE // 2
    tok = lambda w: pl.BlockSpec((1, ts, w), lambda b, s: (b, s, 0))
    consts = (g_mix, p["w_proj"], p["lb_fw"],
              g_qa, g_kva, p["wq_t"], p["wk"], p["wv_t"])
    lane_tab = pl.BlockSpec((half, ts), lambda b, s: (0, s))
    row_tab = pl.BlockSpec((ts, QK_PAD), lambda b, s: (s, 0))
    return pl.pallas_call(
        functools.partial(_front_kernel, chunk=HG_CHUNK),
        grid=(B, S // ts),
        in_specs=([tok(D)] + [_resident(c.shape) for c in consts]
                  + [lane_tab, lane_tab, row_tab, row_tab, row_tab]),
        out_specs=[
            tok(3 * W), tok(W), tok(p["n_gt"]), tok(W),
            pl.BlockSpec((1, H, QK_PAD, ts), lambda b, s: (b, 0, 0, s)),
            pl.BlockSpec((1, H, ts, QK_PAD), lambda b, s: (b, 0, s, 0)),
            pl.BlockSpec((1, H, V_AUG, ts), lambda b, s: (b, 0, 0, s)),
            pl.BlockSpec((1, 1, H, QK_PAD), lambda b, s: (b, s, 0, 0)),
            pl.BlockSpec((1, H, 1, ts), lambda b, s: (b, 0, 0, s)),
        ],
        out_shape=[
            jax.ShapeDtypeStruct((B, S, 3 * W), BF16),
            jax.ShapeDtypeStruct((B, S, W), F32),
            jax.ShapeDtypeStruct((B, S, p["n_gt"]), BF16),
            jax.ShapeDtypeStruct((B, S, W), F32),
            jax.ShapeDtypeStruct((B, H, QK_PAD, S), BF16),
            jax.ShapeDtypeStruct((B, H, S, QK_PAD), BF16),
            jax.ShapeDtypeStruct((B, H, V_AUG, S), BF16),
            jax.ShapeDtypeStruct((B, S // ts, H, QK_PAD), F32),
            jax.ShapeDtypeStruct((B, H, 1, S), F32),
        ],
        scratch_shapes=[pltpu.VMEM((HG_HEADS, HG_HEAD_DIM, HG_HEAD_DIM), F32)],
        compiler_params=_params("parallel", "arbitrary"),
        name="front",
    )(x, *consts, *tabs)


def _hgrn_bwd_kernel(q_ref, v_ref, f_ref, lb_ref, og_ref, ofw_ref, gon_ref, o_ref, st_ref, *,
                     chunk):
    C = chunk

    @pl.when(pl.program_id(1) == 0)
    def _():
        st_ref[...] = jnp.zeros_like(st_ref)

    def emit(c, hd, o):
        rs = slice(c * C, (c + 1) * C)
        ls = slice(hd * HG_HEAD_DIM, (hd + 1) * HG_HEAD_DIM)
        o = o + ofw_ref[0, rs, ls]
        o = _rms(o, gon_ref[:, ls]) * og_ref[0, rs, ls].astype(F32)
        o_ref[0, rs, ls] = o.astype(o_ref.dtype)

    rs = lambda c: slice(c * C, (c + 1) * C)
    _hgrn_tile(lambda c: q_ref[0, rs(c), :].astype(F32), lambda c: v_ref[0, rs(c), :],
               lambda c: f_ref[0, rs(c), :], lb_ref[...], st_ref, emit,
               nc=q_ref.shape[1] // C, chunk=C, reverse=True)


def _hgrn_bwd(act, fb, lb_row, o_fw, g_onorm, *, ts):
    B, S, W = fb.shape
    ns = S // ts
    col = lambda j: pl.BlockSpec((1, ts, W), lambda b, s: (b, ns - 1 - s, j))
    vec = pl.BlockSpec((1, W), lambda b, s: (0, 0))
    return pl.pallas_call(
        functools.partial(_hgrn_bwd_kernel, chunk=HG_CHUNK),
        grid=(B, ns),
        in_specs=[col(0), col(1), col(0), vec, col(2), col(0), vec],
        out_specs=col(0),
        out_shape=jax.ShapeDtypeStruct((B, S, W), BF16),
        scratch_shapes=[pltpu.VMEM((HG_HEADS, HG_HEAD_DIM, HG_HEAD_DIM), F32)],
        compiler_params=_params("parallel", "arbitrary"),
        name="hgrn_bwd",
    )(act, act, fb, lb_row, act, o_fw, g_onorm)


def _attn_kernel(qt_ref, k_ref, vt_ref, ksq_ref, qsq_ref, o_ref, den_ref, acc_ref, *, tk):
    q_t = qt_ref[0, 0]
    tq = q_t.shape[1]
    k_sq_max = jnp.max(jnp.max(ksq_ref[0, 0], axis=1, keepdims=True), axis=0, keepdims=True)
    bound = jnp.sqrt(qsq_ref[0, 0] * k_sq_max) * BOUND_SLACK
    off_rows = jnp.concatenate([-bound, jnp.zeros((OFF_ROWS - 1, tq), F32)], axis=0).astype(BF16)
    q_off = jnp.concatenate([q_t[:QK_HEAD], off_rows, q_t[QK_HEAD + OFF_ROWS:]], axis=0)

    def step(j, carry):
        k0 = pl.multiple_of(j * tk, tk)
        s = jnp.dot(k_ref[0, 0, pl.ds(k0, tk), :], q_off, preferred_element_type=F32)
        p = jnp.exp2(s).astype(BF16)
        acc_ref[...] += jnp.dot(vt_ref[0, 0, :, pl.ds(k0, tk)], p, preferred_element_type=F32)
        return carry

    acc_ref[...] = jnp.zeros_like(acc_ref)
    lax.fori_loop(0, k_ref.shape[2] // tk, step, 0)
    acc = acc_ref[...]
    den = acc[V_HEAD:V_HEAD + 1]
    den_ref[0, 0] = den
    o_ref[0, 0] = (acc[:V_HEAD] / den).astype(o_ref.dtype)


def _attention(q_t, k, v_t, k_sq, q_sq, *, tq, tk):
    B, H, _, S = q_t.shape
    VA = v_t.shape[2]
    per_q = lambda rows: pl.BlockSpec((1, 1, rows, tq), lambda b, h, i: (b, h, 0, i))
    whole = lambda a: pl.BlockSpec((1, 1) + a.shape[2:], lambda b, h, i: (b, h, 0, 0))
    return pl.pallas_call(
        functools.partial(_attn_kernel, tk=tk),
        grid=(B, H, S // tq),
        in_specs=[per_q(QK_PAD), whole(k), whole(v_t), whole(k_sq), per_q(1)],
        out_specs=[per_q(V_HEAD), per_q(1)],
        out_shape=[jax.ShapeDtypeStruct((B, H, V_HEAD, S), BF16),
                   jax.ShapeDtypeStruct((B, H, 1, S), F32)],
        scratch_shapes=[pltpu.VMEM((VA, tq), F32)],
        compiler_params=_params("parallel", "parallel", "arbitrary"),
        name="attention",
    )(q_t, k, v_t, k_sq, q_sq)


def _attn_online_kernel(qt_ref, k_ref, vt_ref, o_ref, s_ref, acc_ref, m_ref, *, tk):
    q_t = qt_ref[0, 0]
    nk = k_ref.shape[2] // tk

    def scores(j, slot):
        k0 = pl.multiple_of(jnp.minimum(j, nk - 1) * tk, tk)
        s_ref[slot] = jnp.dot(k_ref[0, 0, pl.ds(k0, tk), :], q_t, preferred_element_type=F32)

    def softmax_pv(j, slot):
        s = s_ref[slot]
        m_old = m_ref[...]
        m_new = jnp.maximum(m_old, jnp.max(s, axis=0, keepdims=True))
        p = jnp.exp2(s - m_new).astype(BF16)
        k0 = pl.multiple_of(j * tk, tk)
        pv = jnp.dot(vt_ref[0, 0, :, pl.ds(k0, tk)], p, preferred_element_type=F32)
        acc_ref[...] = jnp.exp2(m_old - m_new) * acc_ref[...] + pv
        m_ref[...] = m_new

    def pair(i, carry):
        j = 2 * i
        scores(j + 1, 1)
        softmax_pv(j, 0)
        scores(j + 2, 0)
        softmax_pv(j + 1, 1)
        return carry

    m_ref[...] = jnp.full_like(m_ref, -1e30)
    acc_ref[...] = jnp.zeros_like(acc_ref)
    scores(0, 0)
    lax.fori_loop(0, nk // 2, pair, 0)
    acc = acc_ref[...]
    o_ref[0, 0] = (acc[:V_HEAD] / acc[V_HEAD:V_HEAD + 1]).astype(o_ref.dtype)


def _attention_online(q_t, k, v_t, *, tq, tk):
    B, H, _, S = q_t.shape
    VA = v_t.shape[2]
    assert (S // tk) % 2 == 0
    per_q = lambda rows: pl.BlockSpec((1, 1, rows, tq), lambda b, h, i: (b, h, 0, i))
    whole = lambda a: pl.BlockSpec((1, 1) + a.shape[2:], lambda b, h, i: (b, h, 0, 0))
    return pl.pallas_call(
        functools.partial(_attn_online_kernel, tk=tk),
        grid=(B, H, S // tq),
        in_specs=[per_q(QK_PAD), whole(k), whole(v_t)],
        out_specs=per_q(V_HEAD),
        out_shape=jax.ShapeDtypeStruct((B, H, V_HEAD, S), BF16),
        scratch_shapes=[pltpu.VMEM((2, tk, tq), F32), pltpu.VMEM((VA, tq), F32),
                        pltpu.VMEM((1, tq), F32)],
        compiler_params=_params("parallel", "parallel", "arbitrary"),
        name="attention_online",
    )(q_t, k, v_t)


ROW_GROUPS = 2


def _mix_ffn_kernel(x_ref, oh_ref, omt_ref, gt_ref, wb0_ref, wb1_ref, wo_ref,
                    gf_ref, wgu_ref, wd_ref, gfin_ref, y_ref):
    D = x_ref.shape[2]
    dff = wd_ref.shape[0]
    tm = x_ref.shape[1]
    rows = [slice(r * (tm // ROW_GROUPS), (r + 1) * (tm // ROW_GROUPS)) for r in range(ROW_GROUPS)]
    dot = functools.partial(jnp.dot, preferred_element_type=F32)
    bp0 = [dot(oh_ref[0, r, :], wb0_ref[...]) for r in rows]
    bp1 = [lax.dot_general(omt_ref[0, :, r], wb1_ref[...], TN_DIMS, preferred_element_type=F32)
           for r in rows]
    gt = [gt_ref[0, r, :].astype(F32) for r in rows]
    merged = [(g[:, :D] * a + g[:, D:] * b).astype(BF16) for g, a, b in zip(gt, bp0, bp1)]
    x1 = [x_ref[0, r, :] + dot(m, wo_ref[...]) for r, m in zip(rows, merged)]
    h2 = [_rms(x, gf_ref[...]).astype(BF16) for x in x1]
    gu = [dot(h, wgu_ref[...]) for h in h2]
    act = [((u[:, :dff] * _sigmoid(u[:, :dff])) * u[:, dff:]).astype(BF16) for u in gu]
    x2 = [x + dot(a, wd_ref[...]) for x, a in zip(x1, act)]
    for r, x in zip(rows, x2):
        y_ref[0, r, :] = _rms(x, gfin_ref[...])


def _mix_ffn(x, o_h, o_mt, gt, wb0, wb1, wo, g_ffn, wgu, wd, g_final, *, tm):
    B, S, D = x.shape
    tok = lambda w: pl.BlockSpec((1, tm, w), lambda b, s: (b, s, 0))
    consts = (wb0, wb1, wo, g_ffn, wgu, wd, g_final)
    return pl.pallas_call(
        _mix_ffn_kernel,
        grid=(B, S // tm),
        in_specs=[
            tok(D), tok(o_h.shape[2]),
            pl.BlockSpec((1, o_mt.shape[1], tm), lambda b, s: (b, 0, s)),
            tok(gt.shape[2]),
        ] + [_resident(c.shape) for c in consts],
        out_specs=tok(D),
        out_shape=jax.ShapeDtypeStruct((B, S, D), F32),
        compiler_params=_params("parallel", "parallel"),
        name="mix_ffn",
    )(x, o_h, o_mt, gt, *consts)


def _rope_tables(S):
    d = QK_ROPE
    half = d // 2
    inv = ROPE_THETA ** (-jnp.arange(0, d, 2, dtype=F32) / d)
    ang = jnp.arange(S, dtype=F32)[:, None] * inv[None, :]
    cos, sin = jnp.cos(ang), jnp.sin(ang)
    z = lambda n: jnp.zeros((S, n), F32)
    tail = QK_PAD - QK_HEAD
    cosk = jnp.concatenate([z(QK_NOPE), cos, cos, z(tail)], axis=1)
    sina = jnp.concatenate([z(QK_NOPE + half), sin, z(tail)], axis=1)
    sinb = jnp.concatenate([z(QK_NOPE), -sin, z(half + tail)], axis=1)
    return cos.T, sin.T, cosk, sina, sinb


def _prep_weights(w_in, lb_param, w_uq, w_ukv, w_branch, w_out, w_gate_up, w_down):
    W = HG_WIDTH
    r_q, H = w_uq.shape[1], w_uq.shape[2]
    r_kv = w_ukv.shape[1]
    w = w_in[0]
    D = w.shape[0]
    c0 = 5 * W
    w_act = jnp.concatenate([w[:, :2 * W], w[:, 4 * W:c0]], axis=1)
    w_dec = w[:, 2 * W:4 * W]
    w_qa = w[:, c0:c0 + r_q]
    w_kva = w[:, c0 + r_q:c0 + r_q + r_kv]
    c1 = c0 + r_q + r_kv
    w_kr = w[:, c1:c1 + QK_ROPE]
    w_gt = w[:, c1 + QK_ROPE:]
    zpad = lambda n: jnp.zeros((D, n), w.dtype)
    w_mla = jnp.concatenate([w_qa, w_kva, zpad(QK_NOPE), w_kr, zpad(QK_PAD - QK_HEAD)], axis=1)
    wq = jnp.transpose(w_uq[0], (1, 2, 0))
    wq_t = jnp.concatenate([wq, jnp.zeros((H, QK_PAD - QK_HEAD, r_q), wq.dtype)], axis=1)
    wq_t = wq_t.reshape(H * QK_PAD, r_q)
    wkv = jnp.transpose(w_ukv[0], (1, 0, 2))
    wk = jnp.concatenate([wkv[:, :, :QK_NOPE],
                          jnp.zeros((H, r_kv, QK_PAD - QK_NOPE), wkv.dtype)], axis=2)
    wk = jnp.transpose(wk, (1, 0, 2)).reshape(r_kv, H * QK_PAD)
    wv_t = jnp.transpose(wkv[:, :, QK_NOPE:], (0, 2, 1)).reshape(H * V_HEAD, r_kv)
    lb = jax.nn.softmax(lb_param.astype(F32), axis=0)[0]
    b16 = lambda a: a.astype(BF16)
    return dict(
        w_proj=b16(jnp.concatenate([w_act, w_dec, w_gt, w_mla], axis=1)), n_gt=w_gt.shape[1],
        wq_t=b16(wq_t), wk=b16(wk), wv_t=b16(wv_t),
        wb0=b16(w_branch[0, 0]), wb1=b16(w_branch[0, 1]), wo=b16(w_out[0]),
        wgu=b16(w_gate_up[0]), wd=b16(w_down[0]),
        lb_fw=lb[0:1], lb_bw=lb[1:2],
    )


def _pick(n, pref):
    t = min(n, pref)
    while n % t:
        t //= 2
    return t


def _trunk(x, p, g_mix, g_onorm, g_qa, g_kva, g_ffn, g_final):
    B, S, D = x.shape
    ts = _pick(S, 512)
    act, fb, gt, o_fw, q_t, k, v_t, k_sq, q_sq = _front(x, p, g_mix, g_qa, g_kva, _rope_tables(S),
                                                        ts=ts)
    o_h = _hgrn_bwd(act, fb, p["lb_bw"], o_fw, g_onorm, ts=ts)
    k_sq = jnp.transpose(k_sq, (0, 2, 1, 3))
    o_mt, den = _attention(q_t, k, v_t, k_sq, q_sq, tq=_pick(S, 1024), tk=_pick(S, 8192))
    o_mt = lax.cond(jnp.min(den) >= DENOM_FLOOR, lambda: o_mt,
                    lambda: _attention_online(q_t, k, v_t, tq=_pick(S, 512),
                                              tk=_pick(S // 2, 1024)))
    o_mt = o_mt.reshape(B, -1, S)

    return _mix_ffn(x, o_h, o_mt, gt, p["wb0"], p["wb1"], p["wo"], g_ffn, p["wgu"], p["wd"],
                    g_final[None, :], tm=_pick(S, 512))


def kernel(x_prompt, x_sample, g_mix, w_in, lb_param, g_onorm, g_qa, w_uq, g_kva, w_ukv,
           w_branch, w_out, g_ffn, w_gate_up, w_down, g_final):
    p = _prep_weights(w_in, lb_param, w_uq, w_ukv, w_branch, w_out, w_gate_up, w_down)
    args = (p, g_mix, g_onorm, g_qa, g_kva, g_ffn, g_final)
    return (_trunk(x_prompt, *args), _trunk(x_sample, *args))
```

```python
import functools

import jax
import jax.numpy as jnp
from jax import lax
from jax.experimental import pallas as pl
from jax.experimental.pallas import tpu as pltpu

F32 = jnp.float32
BF16 = jnp.bfloat16

EPS = 1e-6
ROPE_THETA = 10000.0
HG_HEADS = 4
HG_HEAD_DIM = 128
HG_WIDTH = HG_HEADS * HG_HEAD_DIM
HG_CHUNK = 64
QK_NOPE = 64
QK_ROPE = 32
V_HEAD = 64
QK_HEAD = QK_NOPE + QK_ROPE
QK_PAD = 128
V_AUG = V_HEAD + 16
LOG2E = 1.4426950408889634
BOUND_SLACK = 1.0 + 2.0 ** -6
DENOM_FLOOR = 2.0 ** -64
OFF_ROWS = 16
VMEM_LIMIT_BYTES = 56 * 1024 * 1024

NT_DIMS = (((1,), (1,)), ((), ()))
TN_DIMS = (((0,), (0,)), ((), ()))


def _sigmoid(x):
    return 1.0 / (1.0 + jnp.exp(-x))


def _rms(x, g):
    ms = jnp.mean(x * x, axis=-1, keepdims=True)
    return x * lax.rsqrt(ms + EPS) * g


def _params(*sem):
    return pltpu.CompilerParams(dimension_semantics=sem, vmem_limit_bytes=VMEM_LIMIT_BYTES)


def _resident(shape):
    zeros = (0,) * len(shape)
    return pl.BlockSpec(shape, lambda *_: zeros, pipeline_mode=pl.Buffered(1))


def _hgrn_tile(q_of, v_of, f_of, lb, st_ref, emit, *, nc, chunk, reverse):
    C = chunk
    D = HG_HEAD_DIM
    heads = range(HG_HEADS)
    rows = lax.broadcasted_iota(jnp.int32, (C, C), 0)
    cols = lax.broadcasted_iota(jnp.int32, (C, C), 1)
    tri = (cols >= rows) if reverse else (cols <= rows)
    tri_b = tri.astype(BF16)
    mid = C // 2 if reverse else C // 2 - 1
    last = 0 if reverse else C - 1

    q_in, k_in, q_inter, k_state, v_b, decay = [], [], [], [], [], []
    for c in range(nc):
        f = lb + (1.0 - lb) * _sigmoid(f_of(c))
        g = jnp.log2(f)
        kk = 1.0 - f
        g_hi = g.astype(BF16)
        g_lo = (g - g_hi.astype(F32)).astype(BF16)
        b = (jnp.dot(tri_b, g_hi, preferred_element_type=F32)
             + jnp.dot(tri_b, g_lo, preferred_element_type=F32))
        ref = b[mid:mid + 1, :]
        b_last = b[last:last + 1, :]
        qi = q_of(c) * jnp.exp2(b - ref)
        ki = kk * jnp.exp2(ref - b)
        q_in.append(qi.astype(BF16))
        k_in.append(ki.astype(BF16))
        q_inter.append((qi * jnp.exp2(ref)).astype(BF16))
        k_state.append((ki * jnp.exp2(b_last - ref)).astype(BF16))
        v_b.append(v_of(c))
        decay.append(jnp.exp2(b_last))

    o_intra = [[None] * HG_HEADS for _ in range(nc)]
    incr = [[None] * HG_HEADS for _ in range(nc)]
    for c in range(nc):
        for h in heads:
            ls = slice(h * D, (h + 1) * D)
            att = lax.dot_general(q_in[c][:, ls], k_in[c][:, ls], NT_DIMS,
                                  preferred_element_type=F32)
            att = jnp.where(tri, att, 0.0).astype(BF16)
            o_intra[c][h] = jnp.dot(att, v_b[c][:, ls], preferred_element_type=F32)
            incr[c][h] = lax.dot_general(v_b[c][:, ls], k_state[c][:, ls], TN_DIMS,
                                         preferred_element_type=F32)

    st = [st_ref[h] for h in heads]
    for c in (range(nc - 1, -1, -1) if reverse else range(nc)):
        for h in heads:
            ls = slice(h * D, (h + 1) * D)
            o = o_intra[c][h] + lax.dot_general(q_inter[c][:, ls], st[h].astype(BF16), NT_DIMS,
                                                preferred_element_type=F32)
            st[h] = st[h] * decay[c][:, ls] + incr[c][h]
            emit(c, h, o)
    for h in heads:
        st_ref[h] = st[h]


def _mla_tile(mla, gqa_ref, gkva_ref, wq_ref, wk_ref, wv_ref,
              cos_t_ref, sin_t_ref, cosk_ref, sina_ref, sinb_ref,
              qt_ref, k_ref, vt_ref, ksq_ref, qsq_ref):
    heads = wq_ref.shape[0] // QK_PAD
    scale = QK_HEAD ** -0.5 * LOG2E
    r = gqa_ref.shape[1]
    cq = _rms(mla[:, 0:r], gqa_ref[...]).astype(BF16)
    ckv = _rms(mla[:, r:2 * r], gkva_ref[...]).astype(BF16)
    kr = mla[:, 2 * r:2 * r + QK_PAD]
    half = QK_ROPE // 2
    k_pe = (kr * cosk_ref[...]
            + pltpu.roll(kr, half, 1) * sina_ref[...]
            + pltpu.roll(kr, QK_PAD - half, 1) * sinb_ref[...])
    cos_t = cos_t_ref[...]
    sin_t = sin_t_ref[...]
    q_all = lax.dot_general(wq_ref[...], cq, NT_DIMS, preferred_element_type=F32)
    k_all = jnp.dot(ckv, wk_ref[...], preferred_element_type=F32)
    v_all = lax.dot_general(wv_ref[...], ckv, NT_DIMS, preferred_element_type=F32)
    pad_rows = vt_ref.shape[2] - V_HEAD
    ones_rows = (lax.broadcasted_iota(jnp.int32, (pad_rows, cos_t.shape[1]), 0) == 0).astype(F32)
    k_sq_rows = []
    one_lane = (lax.broadcasted_iota(jnp.int32, (1, QK_PAD), 1) == QK_HEAD).astype(F32)
    for h in range(heads):
        q_t = q_all[h * QK_PAD:(h + 1) * QK_PAD]
        x1 = q_t[QK_NOPE:QK_NOPE + half]
        x2 = q_t[QK_NOPE + half:QK_HEAD]
        q_rot = jnp.concatenate(
            [q_t[:QK_NOPE], x1 * cos_t - x2 * sin_t, x2 * cos_t + x1 * sin_t, q_t[QK_HEAD:]],
            axis=0)
        q_b = (q_rot * scale).astype(qt_ref.dtype)
        qt_ref[0, h] = q_b
        q_f = q_b.astype(F32)
        qsq_ref[0, h] = jnp.sum(q_f * q_f, axis=0, keepdims=True)
        k_b = (k_all[:, h * QK_PAD:(h + 1) * QK_PAD] + k_pe).astype(BF16)
        k_f = k_b.astype(F32)
        k_sq = jnp.sum(k_f * k_f, axis=1, keepdims=True)
        k_sq_rows.append(jnp.broadcast_to(jnp.max(k_sq, axis=0, keepdims=True), (1, QK_PAD)))
        k_ref[0, h] = (k_f + one_lane).astype(k_ref.dtype)
        v_t = v_all[h * V_HEAD:(h + 1) * V_HEAD]
        vt_ref[0, h] = jnp.concatenate([v_t, ones_rows], axis=0).astype(vt_ref.dtype)
    ksq_ref[0, 0] = jnp.concatenate(k_sq_rows, axis=0)


def _front_kernel(x_ref, g_ref, wproj_ref, lb_ref,
                  gqa_ref, gkva_ref, wq_ref, wk_ref, wv_ref,
                  cos_t_ref, sin_t_ref, cosk_ref, sina_ref, sinb_ref,
                  act_ref, fb_ref, gt_ref, ofw_ref, qt_ref, k_ref, vt_ref, ksq_ref, qsq_ref, st_ref,
                  *, chunk):
    W = HG_WIDTH
    C = chunk

    @pl.when(pl.program_id(1) == 0)
    def _():
        st_ref[...] = jnp.zeros_like(st_ref)

    h = _rms(x_ref[0], g_ref[...]).astype(BF16)
    proj = jnp.dot(h, wproj_ref[...], preferred_element_type=F32)
    n_gt = gt_ref.shape[2]
    q, i, g = proj[:, :W], proj[:, W:2 * W], proj[:, 2 * W:3 * W]
    q_b = (q * _sigmoid(q)).astype(BF16)
    i_b = i.astype(BF16)
    act_ref[0] = jnp.concatenate([q_b, i_b, (g * _sigmoid(g)).astype(BF16)], axis=1)
    dec = proj[:, 3 * W:5 * W]
    fb_ref[0] = dec[:, W:]
    gt_ref[0] = _sigmoid(proj[:, 5 * W:5 * W + n_gt]).astype(BF16)

    _mla_tile(proj[:, 5 * W + n_gt:], gqa_ref, gkva_ref,
              wq_ref, wk_ref, wv_ref, cos_t_ref, sin_t_ref, cosk_ref, sina_ref, sinb_ref,
              qt_ref, k_ref, vt_ref, ksq_ref, qsq_ref)

    def emit(c, hd, o):
        ofw_ref[0, c * C:(c + 1) * C, hd * HG_HEAD_DIM:(hd + 1) * HG_HEAD_DIM] = o

    rs = lambda c: slice(c * C, (c + 1) * C)
    _hgrn_tile(lambda c: q_b[rs(c)].astype(F32), lambda c: i_b[rs(c)], lambda c: dec[rs(c), :W],
               lb_ref[...], st_ref, emit, nc=x_ref.shape[1] // C, chunk=C, reverse=False)


def _front(x, p, g_mix, g_qa, g_kva, tabs, *, ts):
    B, S, D = x.shape
    W = HG_WIDTH
    H = p["wq_t"].shape[0] // QK_PAD
    half = QK_ROPE // 2
    tok = lambda w: pl.BlockSpec((1, ts, w), lambda b, s: (b, s, 0))
    consts = (g_mix, p["w_proj"], p["lb_fw"],
              g_qa, g_kva, p["wq_t"], p["wk"], p["wv_t"])
    lane_tab = pl.BlockSpec((half, ts), lambda b, s: (0, s))
    row_tab = pl.BlockSpec((ts, QK_PAD), lambda b, s: (s, 0))
    return pl.pallas_call(
        functools.partial(_front_kernel, chunk=HG_CHUNK),
        grid=(B, S // ts),
        in_specs=([tok(D)] + [_resident(c.shape) for c in consts]
                  + [lane_tab, lane_tab, row_tab, row_tab, row_tab]),
        out_specs=[
            tok(3 * W), tok(W), tok(p["n_gt"]), tok(W),
            pl.BlockSpec((1, H, QK_PAD, ts), lambda b, s: (b, 0, 0, s)),
            pl.BlockSpec((1, H, ts, QK_PAD), lambda b, s: (b, 0, s, 0)),
            pl.BlockSpec((1, H, V_AUG, ts), lambda b, s: (b, 0, 0, s)),
            pl.BlockSpec((1, 1, H, QK_PAD), lambda b, s: (b, s, 0, 0)),
            pl.BlockSpec((1, H, 1, ts), lambda b, s: (b, 0, 0, s)),
        ],
        out_shape=[
            jax.ShapeDtypeStruct((B, S, 3 * W), BF16),
            jax.ShapeDtypeStruct((B, S, W), F32),
            jax.ShapeDtypeStruct((B, S, p["n_gt"]), BF16),
            jax.ShapeDtypeStruct((B, S, W), F32),
            jax.ShapeDtypeStruct((B, H, QK_PAD, S), BF16),
            jax.ShapeDtypeStruct((B, H, S, QK_PAD), BF16),
            jax.ShapeDtypeStruct((B, H, V_AUG, S), BF16),
            jax.ShapeDtypeStruct((B, S // ts, H, QK_PAD), F32),
            jax.ShapeDtypeStruct((B, H, 1, S), F32),
        ],
        scratch_shapes=[pltpu.VMEM((HG_HEADS, HG_HEAD_DIM, HG_HEAD_DIM), F32)],
        compiler_params=_params("parallel", "arbitrary"),
        name="front",
    )(x, *consts, *tabs)


def _hgrn_bwd_kernel(q_ref, v_ref, f_ref, lb_ref, og_ref, ofw_ref, gon_ref, o_ref, st_ref, *,
                     chunk):
    C = chunk

    @pl.when(pl.program_id(1) == 0)
    def _():
        st_ref[...] = jnp.zeros_like(st_ref)

    def emit(c, hd, o):
        rs = slice(c * C, (c + 1) * C)
        ls = slice(hd * HG_HEAD_DIM, (hd + 1) * HG_HEAD_DIM)
        o = o + ofw_ref[0, rs, ls]
        o = _rms(o, gon_ref[:, ls]) * og_ref[0, rs, ls].astype(F32)
        o_ref[0, rs, ls] = o.astype(o_ref.dtype)

    rs = lambda c: slice(c * C, (c + 1) * C)
    _hgrn_tile(lambda c: q_ref[0, rs(c), :].astype(F32), lambda c: v_ref[0, rs(c), :],
               lambda c: f_ref[0, rs(c), :], lb_ref[...], st_ref, emit,
               nc=q_ref.shape[1] // C, chunk=C, reverse=True)


def _hgrn_bwd(act, fb, lb_row, o_fw, g_onorm, *, ts):
    B, S, W = fb.shape
    ns = S // ts
    col = lambda j: pl.BlockSpec((1, ts, W), lambda b, s: (b, ns - 1 - s, j))
    vec = pl.BlockSpec((1, W), lambda b, s: (0, 0))
    return pl.pallas_call(
        functools.partial(_hgrn_bwd_kernel, chunk=HG_CHUNK),
        grid=(B, ns),
        in_specs=[col(0), col(1), col(0), vec, col(2), col(0), vec],
        out_specs=col(0),
        out_shape=jax.ShapeDtypeStruct((B, S, W), BF16),
        scratch_shapes=[pltpu.VMEM((HG_HEADS, HG_HEAD_DIM, HG_HEAD_DIM), F32)],
        compiler_params=_params("parallel", "arbitrary"),
        name="hgrn_bwd",
    )(act, act, fb, lb_row, act, o_fw, g_onorm)


def _attn_kernel(qt_ref, k_ref, vt_ref, ksq_ref, qsq_ref, o_ref, den_ref, acc_ref, *, tk):
    q_t = qt_ref[0, 0]
    tq = q_t.shape[1]
    k_sq_max = jnp.max(jnp.max(ksq_ref[0, 0], axis=1, keepdims=True), axis=0, keepdims=True)
    bound = jnp.sqrt(qsq_ref[0, 0] * k_sq_max) * BOUND_SLACK
    off_rows = jnp.concatenate([-bound, jnp.zeros((OFF_ROWS - 1, tq), F32)], axis=0).astype(BF16)
    q_off = jnp.concatenate([q_t[:QK_HEAD], off_rows, q_t[QK_HEAD + OFF_ROWS:]], axis=0)

    def step(j, carry):
        k0 = pl.multiple_of(j * tk, tk)
        s = jnp.dot(k_ref[0, 0, pl.ds(k0, tk), :], q_off, preferred_element_type=F32)
        p = jnp.exp2(s).astype(BF16)
        acc_ref[...] += jnp.dot(vt_ref[0, 0, :, pl.ds(k0, tk)], p, preferred_element_type=F32)
        return carry

    acc_ref[...] = jnp.zeros_like(acc_ref)
    lax.fori_loop(0, k_ref.shape[2] // tk, step, 0)
    acc = acc_ref[...]
    den = acc[V_HEAD:V_HEAD + 1]
    den_ref[0, 0] = den
    o_ref[0, 0] = (acc[:V_HEAD] / den).astype(o_ref.dtype)


def _attention(q_t, k, v_t, k_sq, q_sq, *, tq, tk):
    B, H, _, S = q_t.shape
    VA = v_t.shape[2]
    per_q = lambda rows: pl.BlockSpec((1, 1, rows, tq), lambda b, h, i: (b, h, 0, i))
    whole = lambda a: pl.BlockSpec((1, 1) + a.shape[2:], lambda b, h, i: (b, h, 0, 0))
    return pl.pallas_call(
        functools.partial(_attn_kernel, tk=tk),
        grid=(B, H, S // tq),
        in_specs=[per_q(QK_PAD), whole(k), whole(v_t), whole(k_sq), per_q(1)],
        out_specs=[per_q(V_HEAD), per_q(1)],
        out_shape=[jax.ShapeDtypeStruct((B, H, V_HEAD, S), BF16),
                   jax.ShapeDtypeStruct((B, H, 1, S), F32)],
        scratch_shapes=[pltpu.VMEM((VA, tq), F32)],
        compiler_params=_params("parallel", "parallel", "arbitrary"),
        name="attention",
    )(q_t, k, v_t, k_sq, q_sq)


def _attn_online_kernel(qt_ref, k_ref, vt_ref, o_ref, s_ref, acc_ref, m_ref, *, tk):
    q_t = qt_ref[0, 0]
    nk = k_ref.shape[2] // tk

    def scores(j, slot):
        k0 = pl.multiple_of(jnp.minimum(j, nk - 1) * tk, tk)
        s_ref[slot] = jnp.dot(k_ref[0, 0, pl.ds(k0, tk), :], q_t, preferred_element_type=F32)

    def softmax_pv(j, slot):
        s = s_ref[slot]
        m_old = m_ref[...]
        m_new = jnp.maximum(m_old, jnp.max(s, axis=0, keepdims=True))
        p = jnp.exp2(s - m_new).astype(BF16)
        k0 = pl.multiple_of(j * tk, tk)
        pv = jnp.dot(vt_ref[0, 0, :, pl.ds(k0, tk)], p, preferred_element_type=F32)
        acc_ref[...] = jnp.exp2(m_old - m_new) * acc_ref[...] + pv
        m_ref[...] = m_new

    def pair(i, carry):
        j = 2 * i
        scores(j + 1, 1)
        softmax_pv(j, 0)
        scores(j + 2, 0)
        softmax_pv(j + 1, 1)
        return carry

    m_ref[...] = jnp.full_like(m_ref, -1e30)
    acc_ref[...] = jnp.zeros_like(acc_ref)
    scores(0, 0)
    lax.fori_loop(0, nk // 2, pair, 0)
    acc = acc_ref[...]
    o_ref[0, 0] = (acc[:V_HEAD] / acc[V_HEAD:V_HEAD + 1]).astype(o_ref.dtype)


def _attention_online(q_t, k, v_t, *, tq, tk):
    B, H, _, S = q_t.shape
    VA = v_t.shape[2]
    assert (S // tk) % 2 == 0
    per_q = lambda rows: pl.BlockSpec((1, 1, rows, tq), lambda b, h, i: (b, h, 0, i))
    whole = lambda a: pl.BlockSpec((1, 1) + a.shape[2:], lambda b, h, i: (b, h, 0, 0))
    return pl.pallas_call(
        functools.partial(_attn_online_kernel, tk=tk),
        grid=(B, H, S // tq),
        in_specs=[per_q(QK_PAD), whole(k), whole(v_t)],
        out_specs=per_q(V_HEAD),
        out_shape=jax.ShapeDtypeStruct((B, H, V_HEAD, S), BF16),
        scratch_shapes=[pltpu.VMEM((2, tk, tq), F32), pltpu.VMEM((VA, tq), F32),
                        pltpu.VMEM((1, tq), F32)],
        compiler_params=_params("parallel", "parallel", "arbitrary"),
        name="attention_online",
    )(q_t, k, v_t)


ROW_GROUPS = 2


def _mix_ffn_kernel(x_ref, oh_ref, omt_ref, gt_ref, wb0_ref, wb1_ref, wo_ref,
                    gf_ref, wgu_ref, wd_ref, gfin_ref, y_ref):
    D = x_ref.shape[2]
    dff = wd_ref.shape[0]
    tm = x_ref.shape[1]
    rows = [slice(r * (tm // ROW_GROUPS), (r + 1) * (tm // ROW_GROUPS)) for r in range(ROW_GROUPS)]
    dot = functools.partial(jnp.dot, preferred_element_type=F32)
    bp0 = [dot(oh_ref[0, r, :], wb0_ref[...]) for r in rows]
    bp1 = [lax.dot_general(omt_ref[0, :, r], wb1_ref[...], TN_DIMS, preferred_element_type=F32)
           for r in rows]
    gt = [gt_ref[0, r, :].astype(F32) for r in rows]
    merged = [(g[:, :D] * a + g[:, D:] * b).astype(BF16) for g, a, b in zip(gt, bp0, bp1)]
    x1 = [x_ref[0, r, :] + dot(m, wo_ref[...]) for r, m in zip(rows, merged)]
    h2 = [_rms(x, gf_ref[...]).astype(BF16) for x in x1]
    gu = [dot(h, wgu_ref[...]) for h in h2]
    act = [((u[:, :dff] * _sigmoid(u[:, :dff])) * u[:, dff:]).astype(BF16) for u in gu]
    x2 = [x + dot(a, wd_ref[...]) for x, a in zip(x1, act)]
    for r, x in zip(rows, x2):
        y_ref[0, r, :] = _rms(x, gfin_ref[...])


def _mix_ffn(x, o_h, o_mt, gt, wb0, wb1, wo, g_ffn, wgu, wd, g_final, *, tm):
    B, S, D = x.shape
    tok = lambda w: pl.BlockSpec((1, tm, w), lambda b, s: (b, s, 0))
    consts = (wb0, wb1, wo, g_ffn, wgu, wd, g_final)
    return pl.pallas_call(
        _mix_ffn_kernel,
        grid=(B, S // tm),
        in_specs=[
            tok(D), tok(o_h.shape[2]),
            pl.BlockSpec((1, o_mt.shape[1], tm), lambda b, s: (b, 0, s)),
            tok(gt.shape[2]),
        ] + [_resident(c.shape) for c in consts],
        out_specs=tok(D),
        out_shape=jax.ShapeDtypeStruct((B, S, D), F32),
        compiler_params=_params("parallel", "parallel"),
        name="mix_ffn",
    )(x, o_h, o_mt, gt, *consts)


def _rope_tables(S):
    d = QK_ROPE
    half = d // 2
    inv = ROPE_THETA ** (-jnp.arange(0, d, 2, dtype=F32) / d)
    ang = jnp.arange(S, dtype=F32)[:, None] * inv[None, :]
    cos, sin = jnp.cos(ang), jnp.sin(ang)
    z = lambda n: jnp.zeros((S, n), F32)
    tail = QK_PAD - QK_HEAD
    cosk = jnp.concatenate([z(QK_NOPE), cos, cos, z(tail)], axis=1)
    sina = jnp.concatenate([z(QK_NOPE + half), sin, z(tail)], axis=1)
    sinb = jnp.concatenate([z(QK_NOPE), -sin, z(half + tail)], axis=1)
    return cos.T, sin.T, cosk, sina, sinb


def _prep_weights(w_in, lb_param, w_uq, w_ukv, w_branch, w_out, w_gate_up, w_down):
    W = HG_WIDTH
    r_q, H = w_uq.shape[1], w_uq.shape[2]
    r_kv = w_ukv.shape[1]
    w = w_in[0]
    D = w.shape[0]
    c0 = 5 * W
    w_act = jnp.concatenate([w[:, :2 * W], w[:, 4 * W:c0]], axis=1)
    w_dec = w[:, 2 * W:4 * W]
    w_qa = w[:, c0:c0 + r_q]
    w_kva = w[:, c0 + r_q:c0 + r_q + r_kv]
    c1 = c0 + r_q + r_kv
    w_kr = w[:, c1:c1 + QK_ROPE]
    w_gt = w[:, c1 + QK_ROPE:]
    zpad = lambda n: jnp.zeros((D, n), w.dtype)
    w_mla = jnp.concatenate([w_qa, w_kva, zpad(QK_NOPE), w_kr, zpad(QK_PAD - QK_HEAD)], axis=1)
    wq = jnp.transpose(w_uq[0], (1, 2, 0))
    wq_t = jnp.concatenate([wq, jnp.zeros((H, QK_PAD - QK_HEAD, r_q), wq.dtype)], axis=1)
    wq_t = wq_t.reshape(H * QK_PAD, r_q)
    wkv = jnp.transpose(w_ukv[0], (1, 0, 2))
    wk = jnp.concatenate([wkv[:, :, :QK_NOPE],
                          jnp.zeros((H, r_kv, QK_PAD - QK_NOPE), wkv.dtype)], axis=2)
    wk = jnp.transpose(wk, (1, 0, 2)).reshape(r_kv, H * QK_PAD)
    wv_t = jnp.transpose(wkv[:, :, QK_NOPE:], (0, 2, 1)).reshape(H * V_HEAD, r_kv)
    lb = jax.nn.softmax(lb_param.astype(F32), axis=0)[0]
    b16 = lambda a: a.astype(BF16)
    return dict(
        w_proj=b16(jnp.concatenate([w_act, w_dec, w_gt, w_mla], axis=1)), n_gt=w_gt.shape[1],
        wq_t=b16(wq_t), wk=b16(wk), wv_t=b16(wv_t),
        wb0=b16(w_branch[0, 0]), wb1=b16(w_branch[0, 1]), wo=b16(w_out[0]),
        wgu=b16(w_gate_up[0]), wd=b16(w_down[0]),
        lb_fw=lb[0:1], lb_bw=lb[1:2],
    )


SEQ_TILE = 512
ATTN_Q_TILE = 1024
ATTN_K_BLOCK = 8192
ONLINE_Q_TILE = 512
ONLINE_K_TILE = 1024


def _pick(n, pref):
    t = min(n, pref)
    while n % t:
        t //= 2
    return t


def _trunk(x, p, g_mix, g_onorm, g_qa, g_kva, g_ffn, g_final):
    B, S, D = x.shape
    ts = _pick(S, SEQ_TILE)
    act, fb, gt, o_fw, q_t, k, v_t, k_sq, q_sq = _front(x, p, g_mix, g_qa, g_kva, _rope_tables(S),
                                                        ts=ts)
    o_h = _hgrn_bwd(act, fb, p["lb_bw"], o_fw, g_onorm, ts=ts)
    k_sq = jnp.transpose(k_sq, (0, 2, 1, 3))
    o_mt, den = _attention(q_t, k, v_t, k_sq, q_sq, tq=_pick(S, ATTN_Q_TILE),
                           tk=_pick(S, ATTN_K_BLOCK))
    o_mt = lax.cond(jnp.min(den) >= DENOM_FLOOR, lambda: o_mt,
                    lambda: _attention_online(q_t, k, v_t, tq=_pick(S, ONLINE_Q_TILE),
                                              tk=_pick(S // 2, ONLINE_K_TILE)))
    o_mt = o_mt.reshape(B, -1, S)

    return _mix_ffn(x, o_h, o_mt, gt, p["wb0"], p["wb1"], p["wo"], g_ffn, p["wgu"], p["wd"],
                    g_final[None, :], tm=ts)


def kernel(x_prompt, x_sample, g_mix, w_in, lb_param, g_onorm, g_qa, w_uq, g_kva, w_ukv,
           w_branch, w_out, g_ffn, w_gate_up, w_down, g_final):
    p = _prep_weights(w_in, lb_param, w_uq, w_ukv, w_branch, w_out, w_gate_up, w_down)
    args = (p, g_mix, g_onorm, g_qa, g_kva, g_ffn, g_final)
    return (_trunk(x_prompt, *args), _trunk(x_sample, *args))
```

```python
import functools

import jax
import jax.numpy as jnp
from jax import lax
from jax.experimental import pallas as pl
from jax.experimental.pallas import tpu as pltpu

F32 = jnp.float32
BF16 = jnp.bfloat16

EPS = 1e-6
ROPE_THETA = 10000.0
HG_HEADS = 4
HG_HEAD_DIM = 128
HG_WIDTH = HG_HEADS * HG_HEAD_DIM
HG_CHUNK = 64
QK_NOPE = 64
QK_ROPE = 32
V_HEAD = 64
QK_HEAD = QK_NOPE + QK_ROPE
QK_PAD = 128
V_AUG = V_HEAD + 16
LOG2E = 1.4426950408889634
BOUND_SLACK = 1.0 + 2.0 ** -6
DENOM_FLOOR = 2.0 ** -64
OFF_ROWS = 16
ROW_GROUPS = 2
VMEM_LIMIT_BYTES = 56 * 1024 * 1024

NT_DIMS = (((1,), (1,)), ((), ()))
TN_DIMS = (((0,), (0,)), ((), ()))


def _sigmoid(x):
    return 1.0 / (1.0 + jnp.exp(-x))


def _rms(x, g):
    ms = jnp.mean(x * x, axis=-1, keepdims=True)
    return x * lax.rsqrt(ms + EPS) * g


def _params(*sem):
    return pltpu.CompilerParams(dimension_semantics=sem, vmem_limit_bytes=VMEM_LIMIT_BYTES)


def _resident(shape):
    zeros = (0,) * len(shape)
    return pl.BlockSpec(shape, lambda *_: zeros, pipeline_mode=pl.Buffered(1))


def _hgrn_tile(q_of, v_of, f_of, lb, st_ref, emit, *, nc, chunk, reverse):
    C = chunk
    D = HG_HEAD_DIM
    heads = range(HG_HEADS)
    rows = lax.broadcasted_iota(jnp.int32, (C, C), 0)
    cols = lax.broadcasted_iota(jnp.int32, (C, C), 1)
    tri = (cols >= rows) if reverse else (cols <= rows)
    tri_b = tri.astype(BF16)
    mid = C // 2 if reverse else C // 2 - 1
    last = 0 if reverse else C - 1

    q_in, k_in, q_inter, k_state, v_b, decay = [], [], [], [], [], []
    for c in range(nc):
        f = lb + (1.0 - lb) * _sigmoid(f_of(c))
        g = jnp.log2(f)
        kk = 1.0 - f
        g_hi = g.astype(BF16)
        g_lo = (g - g_hi.astype(F32)).astype(BF16)
        b = (jnp.dot(tri_b, g_hi, preferred_element_type=F32)
             + jnp.dot(tri_b, g_lo, preferred_element_type=F32))
        ref = b[mid:mid + 1, :]
        b_last = b[last:last + 1, :]
        qi = q_of(c) * jnp.exp2(b - ref)
        ki = kk * jnp.exp2(ref - b)
        q_in.append(qi.astype(BF16))
        k_in.append(ki.astype(BF16))
        q_inter.append((qi * jnp.exp2(ref)).astype(BF16))
        k_state.append((ki * jnp.exp2(b_last - ref)).astype(BF16))
        v_b.append(v_of(c))
        decay.append(jnp.exp2(b_last))

    o_intra = [[None] * HG_HEADS for _ in range(nc)]
    incr = [[None] * HG_HEADS for _ in range(nc)]
    for c in range(nc):
        for h in heads:
            ls = slice(h * D, (h + 1) * D)
            att = lax.dot_general(q_in[c][:, ls], k_in[c][:, ls], NT_DIMS,
                                  preferred_element_type=F32)
            att = jnp.where(tri, att, 0.0).astype(BF16)
            o_intra[c][h] = jnp.dot(att, v_b[c][:, ls], preferred_element_type=F32)
            incr[c][h] = lax.dot_general(v_b[c][:, ls], k_state[c][:, ls], TN_DIMS,
                                         preferred_element_type=F32)

    st = [st_ref[h] for h in heads]
    for c in (range(nc - 1, -1, -1) if reverse else range(nc)):
        for h in heads:
            ls = slice(h * D, (h + 1) * D)
            o = o_intra[c][h] + lax.dot_general(q_inter[c][:, ls], st[h].astype(BF16), NT_DIMS,
                                                preferred_element_type=F32)
            st[h] = st[h] * decay[c][:, ls] + incr[c][h]
            emit(c, h, o)
    for h in heads:
        st_ref[h] = st[h]


def _mla_tile(mla, tok, gqa_ref, gkva_ref, wq_ref, wk_ref, wv_ref,
              cos_t_ref, sin_t_ref, cosk_ref, sina_ref, sinb_ref,
              qt_ref, k_ref, vt_ref, qsq_ref):
    heads = wq_ref.shape[0] // QK_PAD
    scale = QK_HEAD ** -0.5 * LOG2E
    r = gqa_ref.shape[1]
    cq = _rms(mla[:, 0:r], gqa_ref[...]).astype(BF16)
    ckv = _rms(mla[:, r:2 * r], gkva_ref[...]).astype(BF16)
    kr = mla[:, 2 * r:2 * r + QK_PAD]
    half = QK_ROPE // 2
    k_pe = (kr * cosk_ref[tok, :]
            + pltpu.roll(kr, half, 1) * sina_ref[tok, :]
            + pltpu.roll(kr, QK_PAD - half, 1) * sinb_ref[tok, :])
    cos_t = cos_t_ref[:, tok]
    sin_t = sin_t_ref[:, tok]
    q_all = lax.dot_general(wq_ref[...], cq, NT_DIMS, preferred_element_type=F32)
    k_all = jnp.dot(ckv, wk_ref[...], preferred_element_type=F32)
    v_all = lax.dot_general(wv_ref[...], ckv, NT_DIMS, preferred_element_type=F32)
    pad_rows = vt_ref.shape[2] - V_HEAD
    ones_rows = (lax.broadcasted_iota(jnp.int32, (pad_rows, cos_t.shape[1]), 0) == 0).astype(F32)
    k_sq_rows = []
    one_lane = (lax.broadcasted_iota(jnp.int32, (1, QK_PAD), 1) == QK_HEAD).astype(F32)
    for h in range(heads):
        q_t = q_all[h * QK_PAD:(h + 1) * QK_PAD]
        x1 = q_t[QK_NOPE:QK_NOPE + half]
        x2 = q_t[QK_NOPE + half:QK_HEAD]
        q_rot = jnp.concatenate(
            [q_t[:QK_NOPE], x1 * cos_t - x2 * sin_t, x2 * cos_t + x1 * sin_t, q_t[QK_HEAD:]],
            axis=0)
        q_b = (q_rot * scale).astype(qt_ref.dtype)
        qt_ref[0, h, :, tok] = q_b
        q_f = q_b.astype(F32)
        qsq_ref[0, h, :, tok] = jnp.sum(q_f * q_f, axis=0, keepdims=True)
        k_b = (k_all[:, h * QK_PAD:(h + 1) * QK_PAD] + k_pe).astype(BF16)
        k_f = k_b.astype(F32)
        k_sq = jnp.sum(k_f * k_f, axis=1, keepdims=True)
        k_sq_rows.append(jnp.broadcast_to(jnp.max(k_sq, axis=0, keepdims=True), (1, QK_PAD)))
        k_ref[0, h, tok, :] = (k_f + one_lane).astype(k_ref.dtype)
        v_t = v_all[h * V_HEAD:(h + 1) * V_HEAD]
        vt_ref[0, h, :, tok] = jnp.concatenate([v_t, ones_rows], axis=0).astype(vt_ref.dtype)
    return jnp.concatenate(k_sq_rows, axis=0)


def _front_kernel(x_ref, g_ref, wproj_ref, lb_ref,
                  gqa_ref, gkva_ref, wq_ref, wk_ref, wv_ref,
                  cos_t_ref, sin_t_ref, cosk_ref, sina_ref, sinb_ref,
                  act_ref, fb_ref, gt_ref, ofw_ref, qt_ref, k_ref, vt_ref, ksq_ref, qsq_ref, st_ref,
                  *, chunk):
    W = HG_WIDTH
    C = chunk

    @pl.when(pl.program_id(1) == 0)
    def _():
        st_ref[...] = jnp.zeros_like(st_ref)

    n_gt = gt_ref.shape[2]
    n = x_ref.shape[1] // ROW_GROUPS
    groups = [slice(r * n, (r + 1) * n) for r in range(ROW_GROUPS)]
    hs = [_rms(x_ref[0, r, :], g_ref[...]).astype(BF16) for r in groups]
    projs = [jnp.dot(h, wproj_ref[...], preferred_element_type=F32) for h in hs]
    q_bs, i_bs, decs = [], [], []
    for r, proj in zip(groups, projs):
        q, i, g = proj[:, :W], proj[:, W:2 * W], proj[:, 2 * W:3 * W]
        q_bs.append((q * _sigmoid(q)).astype(BF16))
        i_bs.append(i.astype(BF16))
        act_ref[0, r, :] = jnp.concatenate([q_bs[-1], i_bs[-1], (g * _sigmoid(g)).astype(BF16)],
                                           axis=1)
        decs.append(proj[:, 3 * W:5 * W])
        fb_ref[0, r, :] = decs[-1][:, W:]
        gt_ref[0, r, :] = _sigmoid(proj[:, 5 * W:5 * W + n_gt]).astype(BF16)
    k_sq = [_mla_tile(proj[:, 5 * W + n_gt:], r, gqa_ref, gkva_ref, wq_ref, wk_ref, wv_ref,
                      cos_t_ref, sin_t_ref, cosk_ref, sina_ref, sinb_ref,
                      qt_ref, k_ref, vt_ref, qsq_ref) for r, proj in zip(groups, projs)]
    ksq_ref[0, 0] = functools.reduce(jnp.maximum, k_sq)
    q_b = jnp.concatenate(q_bs, axis=0)
    i_b = jnp.concatenate(i_bs, axis=0)
    dec = jnp.concatenate(decs, axis=0)

    def emit(c, hd, o):
        ofw_ref[0, c * C:(c + 1) * C, hd * HG_HEAD_DIM:(hd + 1) * HG_HEAD_DIM] = o

    rs = lambda c: slice(c * C, (c + 1) * C)
    _hgrn_tile(lambda c: q_b[rs(c)].astype(F32), lambda c: i_b[rs(c)], lambda c: dec[rs(c), :W],
               lb_ref[...], st_ref, emit, nc=x_ref.shape[1] // C, chunk=C, reverse=False)


def _front(x, p, g_mix, g_qa, g_kva, tabs, *, ts):
    B, S, D = x.shape
    W = HG_WIDTH
    H = p["wq_t"].shape[0] // QK_PAD
    half = QK_ROPE // 2
    tok = lambda w: pl.BlockSpec((1, ts, w), lambda b, s: (b, s, 0))
    consts = (g_mix, p["w_proj"], p["lb_fw"],
              g_qa, g_kva, p["wq_t"], p["wk"], p["wv_t"])
    lane_tab = pl.BlockSpec((half, ts), lambda b, s: (0, s))
    row_tab = pl.BlockSpec((ts, QK_PAD), lambda b, s: (s, 0))
    return pl.pallas_call(
        functools.partial(_front_kernel, chunk=HG_CHUNK),
        grid=(B, S // ts),
        in_specs=([tok(D)] + [_resident(c.shape) for c in consts]
                  + [lane_tab, lane_tab, row_tab, row_tab, row_tab]),
        out_specs=[
            tok(3 * W), tok(W), tok(p["n_gt"]), tok(W),
            pl.BlockSpec((1, H, QK_PAD, ts), lambda b, s: (b, 0, 0, s)),
            pl.BlockSpec((1, H, ts, QK_PAD), lambda b, s: (b, 0, s, 0)),
            pl.BlockSpec((1, H, V_AUG, ts), lambda b, s: (b, 0, 0, s)),
            pl.BlockSpec((1, 1, H, QK_PAD), lambda b, s: (b, s, 0, 0)),
            pl.BlockSpec((1, H, 1, ts), lambda b, s: (b, 0, 0, s)),
        ],
        out_shape=[
            jax.ShapeDtypeStruct((B, S, 3 * W), BF16),
            jax.ShapeDtypeStruct((B, S, W), F32),
            jax.ShapeDtypeStruct((B, S, p["n_gt"]), BF16),
            jax.ShapeDtypeStruct((B, S, W), F32),
            jax.ShapeDtypeStruct((B, H, QK_PAD, S), BF16),
            jax.ShapeDtypeStruct((B, H, S, QK_PAD), BF16),
            jax.ShapeDtypeStruct((B, H, V_AUG, S), BF16),
            jax.ShapeDtypeStruct((B, S // ts, H, QK_PAD), F32),
            jax.ShapeDtypeStruct((B, H, 1, S), F32),
        ],
        scratch_shapes=[pltpu.VMEM((HG_HEADS, HG_HEAD_DIM, HG_HEAD_DIM), F32)],
        compiler_params=_params("parallel", "arbitrary"),
        name="front",
    )(x, *consts, *tabs)


def _hgrn_bwd_kernel(q_ref, v_ref, f_ref, lb_ref, og_ref, ofw_ref, gon_ref, o_ref, st_ref, *,
                     chunk):
    C = chunk

    @pl.when(pl.program_id(1) == 0)
    def _():
        st_ref[...] = jnp.zeros_like(st_ref)

    def emit(c, hd, o):
        rs = slice(c * C, (c + 1) * C)
        ls = slice(hd * HG_HEAD_DIM, (hd + 1) * HG_HEAD_DIM)
        o = o + ofw_ref[0, rs, ls]
        o = _rms(o, gon_ref[:, ls]) * og_ref[0, rs, ls].astype(F32)
        o_ref[0, rs, ls] = o.astype(o_ref.dtype)

    rs = lambda c: slice(c * C, (c + 1) * C)
    _hgrn_tile(lambda c: q_ref[0, rs(c), :].astype(F32), lambda c: v_ref[0, rs(c), :],
               lambda c: f_ref[0, rs(c), :], lb_ref[...], st_ref, emit,
               nc=q_ref.shape[1] // C, chunk=C, reverse=True)


def _hgrn_bwd(act, fb, lb_row, o_fw, g_onorm, *, ts):
    B, S, W = fb.shape
    ns = S // ts
    col = lambda j: pl.BlockSpec((1, ts, W), lambda b, s: (b, ns - 1 - s, j))
    vec = pl.BlockSpec((1, W), lambda b, s: (0, 0))
    return pl.pallas_call(
        functools.partial(_hgrn_bwd_kernel, chunk=HG_CHUNK),
        grid=(B, ns),
        in_specs=[col(0), col(1), col(0), vec, col(2), col(0), vec],
        out_specs=col(0),
        out_shape=jax.ShapeDtypeStruct((B, S, W), BF16),
        scratch_shapes=[pltpu.VMEM((HG_HEADS, HG_HEAD_DIM, HG_HEAD_DIM), F32)],
        compiler_params=_params("parallel", "arbitrary"),
        name="hgrn_bwd",
    )(act, act, fb, lb_row, act, o_fw, g_onorm)


def _attn_kernel(qt_ref, k_ref, vt_ref, ksq_ref, qsq_ref, o_ref, den_ref, acc_ref, *, tk):
    q_t = qt_ref[0, 0]
    tq = q_t.shape[1]
    k_sq_max = jnp.max(jnp.max(ksq_ref[0, 0], axis=1, keepdims=True), axis=0, keepdims=True)
    bound = jnp.sqrt(qsq_ref[0, 0] * k_sq_max) * BOUND_SLACK
    off_rows = jnp.concatenate([-bound, jnp.zeros((OFF_ROWS - 1, tq), F32)], axis=0).astype(BF16)
    q_off = jnp.concatenate([q_t[:QK_HEAD], off_rows, q_t[QK_HEAD + OFF_ROWS:]], axis=0)

    def step(j, carry):
        k0 = pl.multiple_of(j * tk, tk)
        s = jnp.dot(k_ref[0, 0, pl.ds(k0, tk), :], q_off, preferred_element_type=F32)
        p = jnp.exp2(s).astype(BF16)
        acc_ref[...] += jnp.dot(vt_ref[0, 0, :, pl.ds(k0, tk)], p, preferred_element_type=F32)
        return carry

    acc_ref[...] = jnp.zeros_like(acc_ref)
    lax.fori_loop(0, k_ref.shape[2] // tk, step, 0)
    acc = acc_ref[...]
    den = acc[V_HEAD:V_HEAD + 1]
    den_ref[0, 0] = den
    o_ref[0, 0] = (acc[:V_HEAD] / den).astype(o_ref.dtype)


def _attention(q_t, k, v_t, k_sq, q_sq, *, tq, tk):
    B, H, _, S = q_t.shape
    VA = v_t.shape[2]
    per_q = lambda rows: pl.BlockSpec((1, 1, rows, tq), lambda b, h, i: (b, h, 0, i))
    whole = lambda a: pl.BlockSpec((1, 1) + a.shape[2:], lambda b, h, i: (b, h, 0, 0))
    return pl.pallas_call(
        functools.partial(_attn_kernel, tk=tk),
        grid=(B, H, S // tq),
        in_specs=[per_q(QK_PAD), whole(k), whole(v_t), whole(k_sq), per_q(1)],
        out_specs=[per_q(V_HEAD), per_q(1)],
        out_shape=[jax.ShapeDtypeStruct((B, H, V_HEAD, S), BF16),
                   jax.ShapeDtypeStruct((B, H, 1, S), F32)],
        scratch_shapes=[pltpu.VMEM((VA, tq), F32)],
        compiler_params=_params("parallel", "parallel", "arbitrary"),
        name="attention",
    )(q_t, k, v_t, k_sq, q_sq)


def _attn_online_kernel(qt_ref, k_ref, vt_ref, o_ref, s_ref, acc_ref, m_ref, *, tk):
    q_t = qt_ref[0, 0]
    nk = k_ref.shape[2] // tk

    def scores(j, slot):
        k0 = pl.multiple_of(jnp.minimum(j, nk - 1) * tk, tk)
        s_ref[slot] = jnp.dot(k_ref[0, 0, pl.ds(k0, tk), :], q_t, preferred_element_type=F32)

    def softmax_pv(j, slot):
        s = s_ref[slot]
        m_old = m_ref[...]
        m_new = jnp.maximum(m_old, jnp.max(s, axis=0, keepdims=True))
        p = jnp.exp2(s - m_new).astype(BF16)
        k0 = pl.multiple_of(j * tk, tk)
        pv = jnp.dot(vt_ref[0, 0, :, pl.ds(k0, tk)], p, preferred_element_type=F32)
        acc_ref[...] = jnp.exp2(m_old - m_new) * acc_ref[...] + pv
        m_ref[...] = m_new

    def pair(i, carry):
        j = 2 * i
        scores(j + 1, 1)
        softmax_pv(j, 0)
        scores(j + 2, 0)
        softmax_pv(j + 1, 1)
        return carry

    m_ref[...] = jnp.full_like(m_ref, -1e30)
    acc_ref[...] = jnp.zeros_like(acc_ref)
    scores(0, 0)
    lax.fori_loop(0, nk // 2, pair, 0)
    acc = acc_ref[...]
    o_ref[0, 0] = (acc[:V_HEAD] / acc[V_HEAD:V_HEAD + 1]).astype(o_ref.dtype)


def _attention_online(q_t, k, v_t, *, tq, tk):
    B, H, _, S = q_t.shape
    VA = v_t.shape[2]
    assert (S // tk) % 2 == 0
    per_q = lambda rows: pl.BlockSpec((1, 1, rows, tq), lambda b, h, i: (b, h, 0, i))
    whole = lambda a: pl.BlockSpec((1, 1) + a.shape[2:], lambda b, h, i: (b, h, 0, 0))
    return pl.pallas_call(
        functools.partial(_attn_online_kernel, tk=tk),
        grid=(B, H, S // tq),
        in_specs=[per_q(QK_PAD), whole(k), whole(v_t)],
        out_specs=per_q(V_HEAD),
        out_shape=jax.ShapeDtypeStruct((B, H, V_HEAD, S), BF16),
        scratch_shapes=[pltpu.VMEM((2, tk, tq), F32), pltpu.VMEM((VA, tq), F32),
                        pltpu.VMEM((1, tq), F32)],
        compiler_params=_params("parallel", "parallel", "arbitrary"),
        name="attention_online",
    )(q_t, k, v_t)


def _mix_ffn_kernel(x_ref, oh_ref, omt_ref, gt_ref, wb0_ref, wb1_ref, wo_ref,
                    gf_ref, wgu_ref, wd_ref, gfin_ref, y_ref):
    D = x_ref.shape[2]
    dff = wd_ref.shape[0]
    tm = x_ref.shape[1]
    rows = [slice(r * (tm // ROW_GROUPS), (r + 1) * (tm // ROW_GROUPS)) for r in range(ROW_GROUPS)]
    dot = functools.partial(jnp.dot, preferred_element_type=F32)
    bp0 = [dot(oh_ref[0, r, :], wb0_ref[...]) for r in rows]
    bp1 = [lax.dot_general(omt_ref[0, :, r], wb1_ref[...], TN_DIMS, preferred_element_type=F32)
           for r in rows]
    gt = [gt_ref[0, r, :].astype(F32) for r in rows]
    merged = [(g[:, :D] * a + g[:, D:] * b).astype(BF16) for g, a, b in zip(gt, bp0, bp1)]
    x1 = [x_ref[0, r, :] + dot(m, wo_ref[...]) for r, m in zip(rows, merged)]
    h2 = [_rms(x, gf_ref[...]).astype(BF16) for x in x1]
    gu = [dot(h, wgu_ref[...]) for h in h2]
    act = [((u[:, :dff] * _sigmoid(u[:, :dff])) * u[:, dff:]).astype(BF16) for u in gu]
    x2 = [x + dot(a, wd_ref[...]) for x, a in zip(x1, act)]
    for r, x in zip(rows, x2):
        y_ref[0, r, :] = _rms(x, gfin_ref[...])


def _mix_ffn(x, o_h, o_mt, gt, wb0, wb1, wo, g_ffn, wgu, wd, g_final, *, tm):
    B, S, D = x.shape
    tok = lambda w: pl.BlockSpec((1, tm, w), lambda b, s: (b, s, 0))
    consts = (wb0, wb1, wo, g_ffn, wgu, wd, g_final)
    return pl.pallas_call(
        _mix_ffn_kernel,
        grid=(B, S // tm),
        in_specs=[
            tok(D), tok(o_h.shape[2]),
            pl.BlockSpec((1, o_mt.shape[1], tm), lambda b, s: (b, 0, s)),
            tok(gt.shape[2]),
        ] + [_resident(c.shape) for c in consts],
        out_specs=tok(D),
        out_shape=jax.ShapeDtypeStruct((B, S, D), F32),
        compiler_params=_params("parallel", "parallel"),
        name="mix_ffn",
    )(x, o_h, o_mt, gt, *consts)


def _rope_tables(S):
    d = QK_ROPE
    half = d // 2
    inv = ROPE_THETA ** (-jnp.arange(0, d, 2, dtype=F32) / d)
    ang = jnp.arange(S, dtype=F32)[:, None] * inv[None, :]
    cos, sin = jnp.cos(ang), jnp.sin(ang)
    z = lambda n: jnp.zeros((S, n), F32)
    tail = QK_PAD - QK_HEAD
    cosk = jnp.concatenate([z(QK_NOPE), cos, cos, z(tail)], axis=1)
    sina = jnp.concatenate([z(QK_NOPE + half), sin, z(tail)], axis=1)
    sinb = jnp.concatenate([z(QK_NOPE), -sin, z(half + tail)], axis=1)
    return cos.T, sin.T, cosk, sina, sinb


def _prep_weights(w_in, lb_param, w_uq, w_ukv, w_branch, w_out, w_gate_up, w_down):
    W = HG_WIDTH
    r_q, H = w_uq.shape[1], w_uq.shape[2]
    r_kv = w_ukv.shape[1]
    w = w_in[0]
    D = w.shape[0]
    c0 = 5 * W
    w_act = jnp.concatenate([w[:, :2 * W], w[:, 4 * W:c0]], axis=1)
    w_dec = w[:, 2 * W:4 * W]
    w_qa = w[:, c0:c0 + r_q]
    w_kva = w[:, c0 + r_q:c0 + r_q + r_kv]
    c1 = c0 + r_q + r_kv
    w_kr = w[:, c1:c1 + QK_ROPE]
    w_gt = w[:, c1 + QK_ROPE:]
    zpad = lambda n: jnp.zeros((D, n), w.dtype)
    w_mla = jnp.concatenate([w_qa, w_kva, zpad(QK_NOPE), w_kr, zpad(QK_PAD - QK_HEAD)], axis=1)
    wq = jnp.transpose(w_uq[0], (1, 2, 0))
    wq_t = jnp.concatenate([wq, jnp.zeros((H, QK_PAD - QK_HEAD, r_q), wq.dtype)], axis=1)
    wq_t = wq_t.reshape(H * QK_PAD, r_q)
    wkv = jnp.transpose(w_ukv[0], (1, 0, 2))
    wk = jnp.concatenate([wkv[:, :, :QK_NOPE],
                          jnp.zeros((H, r_kv, QK_PAD - QK_NOPE), wkv.dtype)], axis=2)
    wk = jnp.transpose(wk, (1, 0, 2)).reshape(r_kv, H * QK_PAD)
    wv_t = jnp.transpose(wkv[:, :, QK_NOPE:], (0, 2, 1)).reshape(H * V_HEAD, r_kv)
    lb = jax.nn.softmax(lb_param.astype(F32), axis=0)[0]
    b16 = lambda a: a.astype(BF16)
    return dict(
        w_proj=b16(jnp.concatenate([w_act, w_dec, w_gt, w_mla], axis=1)), n_gt=w_gt.shape[1],
        wq_t=b16(wq_t), wk=b16(wk), wv_t=b16(wv_t),
        wb0=b16(w_branch[0, 0]), wb1=b16(w_branch[0, 1]), wo=b16(w_out[0]),
        wgu=b16(w_gate_up[0]), wd=b16(w_down[0]),
        lb_fw=lb[0:1], lb_bw=lb[1:2],
    )


SEQ_TILE = 512
ATTN_Q_TILE = 1024
ATTN_K_BLOCK = 8192
ONLINE_Q_TILE = 512
ONLINE_K_TILE = 1024


def _pick(n, pref):
    t = min(n, pref)
    while n % t:
        t //= 2
    return t


def _trunk(x, p, g_mix, g_onorm, g_qa, g_kva, g_ffn, g_final):
    B, S, D = x.shape
    ts = _pick(S, SEQ_TILE)
    act, fb, gt, o_fw, q_t, k, v_t, k_sq, q_sq = _front(x, p, g_mix, g_qa, g_kva, _rope_tables(S),
                                                        ts=ts)
    o_h = _hgrn_bwd(act, fb, p["lb_bw"], o_fw, g_onorm, ts=ts)
    k_sq = jnp.transpose(k_sq, (0, 2, 1, 3))
    o_mt, den = _attention(q_t, k, v_t, k_sq, q_sq, tq=_pick(S, ATTN_Q_TILE),
                           tk=_pick(S, ATTN_K_BLOCK))
    o_mt = lax.cond(jnp.min(den) >= DENOM_FLOOR, lambda: o_mt,
                    lambda: _attention_online(q_t, k, v_t, tq=_pick(S, ONLINE_Q_TILE),
                                              tk=_pick(S // 2, ONLINE_K_TILE)))
    o_mt = o_mt.reshape(B, -1, S)

    return _mix_ffn(x, o_h, o_mt, gt, p["wb0"], p["wb1"], p["wo"], g_ffn, p["wgu"], p["wd"],
                    g_final[None, :], tm=ts)


def kernel(x_prompt, x_sample, g_mix, w_in, lb_param, g_onorm, g_qa, w_uq, g_kva, w_ukv,
           w_branch, w_out, g_ffn, w_gate_up, w_down, g_final):
    p = _prep_weights(w_in, lb_param, w_uq, w_ukv, w_branch, w_out, w_gate_up, w_down)
    args = (p, g_mix, g_onorm, g_qa, g_kva, g_ffn, g_final)
    return (_trunk(x_prompt, *args), _trunk(x_sample, *args))
```

```python
import functools

import jax
import jax.numpy as jnp
from jax import lax
from jax.experimental import pallas as pl
from jax.experimental.pallas import tpu as pltpu

F32 = jnp.float32
BF16 = jnp.bfloat16

EPS = 1e-6
ROPE_THETA = 10000.0
HG_HEADS = 4
HG_HEAD_DIM = 128
HG_WIDTH = HG_HEADS * HG_HEAD_DIM
HG_CHUNK = 64
QK_NOPE = 64
QK_ROPE = 32
V_HEAD = 64
QK_HEAD = QK_NOPE + QK_ROPE
QK_PAD = 128
V_AUG = V_HEAD + 16
LOG2E = 1.4426950408889634
BOUND_SLACK = 1.0 + 2.0 ** -6
DENOM_FLOOR = 2.0 ** -64
OFF_ROWS = 16
ROW_GROUPS = 2
VMEM_LIMIT_BYTES = 56 * 1024 * 1024

NT_DIMS = (((1,), (1,)), ((), ()))
TN_DIMS = (((0,), (0,)), ((), ()))


def _sigmoid(x):
    return 1.0 / (1.0 + jnp.exp(-x))


def _rms(x, g):
    ms = jnp.mean(x * x, axis=-1, keepdims=True)
    return x * lax.rsqrt(ms + EPS) * g


def _params(*sem):
    return pltpu.CompilerParams(dimension_semantics=sem, vmem_limit_bytes=VMEM_LIMIT_BYTES)


def _resident(shape):
    zeros = (0,) * len(shape)
    return pl.BlockSpec(shape, lambda *_: zeros, pipeline_mode=pl.Buffered(1))


def _hgrn_tile(q_of, v_of, f_of, lb, st_ref, emit, *, nc, chunk, reverse):
    C = chunk
    D = HG_HEAD_DIM
    heads = range(HG_HEADS)
    rows = lax.broadcasted_iota(jnp.int32, (C, C), 0)
    cols = lax.broadcasted_iota(jnp.int32, (C, C), 1)
    tri = (cols >= rows) if reverse else (cols <= rows)
    tri_b = tri.astype(BF16)
    mid = C // 2 if reverse else C // 2 - 1
    last = 0 if reverse else C - 1

    q_in, k_in, q_inter, k_state, v_b, decay = [], [], [], [], [], []
    for c in range(nc):
        f = lb + (1.0 - lb) * _sigmoid(f_of(c))
        g = jnp.log2(f)
        kk = 1.0 - f
        g_hi = g.astype(BF16)
        g_lo = (g - g_hi.astype(F32)).astype(BF16)
        b = (jnp.dot(tri_b, g_hi, preferred_element_type=F32)
             + jnp.dot(tri_b, g_lo, preferred_element_type=F32))
        ref = b[mid:mid + 1, :]
        b_last = b[last:last + 1, :]
        qi = q_of(c) * jnp.exp2(b - ref)
        ki = kk * jnp.exp2(ref - b)
        q_in.append(qi.astype(BF16))
        k_in.append(ki.astype(BF16))
        q_inter.append((qi * jnp.exp2(ref)).astype(BF16))
        k_state.append((ki * jnp.exp2(b_last - ref)).astype(BF16))
        v_b.append(v_of(c))
        decay.append(jnp.exp2(b_last))

    o_intra = [[None] * HG_HEADS for _ in range(nc)]
    incr = [[None] * HG_HEADS for _ in range(nc)]
    for c in range(nc):
        for h in heads:
            ls = slice(h * D, (h + 1) * D)
            att = lax.dot_general(q_in[c][:, ls], k_in[c][:, ls], NT_DIMS,
                                  preferred_element_type=F32)
            att = jnp.where(tri, att, 0.0).astype(BF16)
            o_intra[c][h] = jnp.dot(att, v_b[c][:, ls], preferred_element_type=F32)
            incr[c][h] = lax.dot_general(v_b[c][:, ls], k_state[c][:, ls], TN_DIMS,
                                         preferred_element_type=F32)

    st = [st_ref[h] for h in heads]
    for c in (range(nc - 1, -1, -1) if reverse else range(nc)):
        for h in heads:
            ls = slice(h * D, (h + 1) * D)
            o = o_intra[c][h] + lax.dot_general(q_inter[c][:, ls], st[h].astype(BF16), NT_DIMS,
                                                preferred_element_type=F32)
            st[h] = st[h] * decay[c][:, ls] + incr[c][h]
            emit(c, h, o)
    for h in heads:
        st_ref[h] = st[h]


def _mla_tile(mla, tok, gqa_ref, gkva_ref, wq_ref, wk_ref, wv_ref,
              cos_t_ref, sin_t_ref, cosk_ref, sina_ref, sinb_ref,
              qt_ref, k_ref, vt_ref, qsq_ref):
    heads = wq_ref.shape[0] // QK_PAD
    scale = QK_HEAD ** -0.5 * LOG2E
    r = gqa_ref.shape[1]
    cq = _rms(mla[:, 0:r], gqa_ref[...]).astype(BF16)
    ckv = _rms(mla[:, r:2 * r], gkva_ref[...]).astype(BF16)
    kr = mla[:, 2 * r:2 * r + QK_PAD]
    half = QK_ROPE // 2
    k_pe = (kr * cosk_ref[tok, :]
            + pltpu.roll(kr, half, 1) * sina_ref[tok, :]
            + pltpu.roll(kr, QK_PAD - half, 1) * sinb_ref[tok, :])
    cos_t = cos_t_ref[:, tok]
    sin_t = sin_t_ref[:, tok]
    q_all = lax.dot_general(wq_ref[...], cq, NT_DIMS, preferred_element_type=F32)
    k_all = jnp.dot(ckv, wk_ref[...], preferred_element_type=F32)
    v_all = lax.dot_general(wv_ref[...], ckv, NT_DIMS, preferred_element_type=F32)
    pad_rows = vt_ref.shape[2] - V_HEAD
    ones_rows = (lax.broadcasted_iota(jnp.int32, (pad_rows, cos_t.shape[1]), 0) == 0).astype(F32)
    k_sq_rows = []
    one_lane = (lax.broadcasted_iota(jnp.int32, (1, QK_PAD), 1) == QK_HEAD).astype(F32)
    for h in range(heads):
        q_t = q_all[h * QK_PAD:(h + 1) * QK_PAD]
        x1 = q_t[QK_NOPE:QK_NOPE + half]
        x2 = q_t[QK_NOPE + half:QK_HEAD]
        q_rot = jnp.concatenate(
            [q_t[:QK_NOPE], x1 * cos_t - x2 * sin_t, x2 * cos_t + x1 * sin_t, q_t[QK_HEAD:]],
            axis=0)
        q_b = (q_rot * scale).astype(qt_ref.dtype)
        qt_ref[0, h, :, tok] = q_b
        q_f = q_b.astype(F32)
        qsq_ref[0, h, :, tok] = jnp.sum(q_f * q_f, axis=0, keepdims=True)
        k_b = (k_all[:, h * QK_PAD:(h + 1) * QK_PAD] + k_pe).astype(BF16)
        k_f = k_b.astype(F32)
        k_sq = jnp.sum(k_f * k_f, axis=1, keepdims=True)
        k_sq_rows.append(jnp.broadcast_to(jnp.max(k_sq, axis=0, keepdims=True), (1, QK_PAD)))
        k_ref[0, h, tok, :] = (k_f + one_lane).astype(k_ref.dtype)
        v_t = v_all[h * V_HEAD:(h + 1) * V_HEAD]
        vt_ref[0, h, :, tok] = jnp.concatenate([v_t, ones_rows], axis=0).astype(vt_ref.dtype)
    return jnp.concatenate(k_sq_rows, axis=0)


def _front_kernel(x_ref, g_ref, wproj_ref, lb_ref,
                  gqa_ref, gkva_ref, wq_ref, wk_ref, wv_ref,
                  cos_t_ref, sin_t_ref, cosk_ref, sina_ref, sinb_ref,
                  act_ref, fb_ref, gt_ref, ofw_ref, qt_ref, k_ref, vt_ref, ksq_ref, qsq_ref, st_ref,
                  *, chunk):
    W = HG_WIDTH
    C = chunk

    @pl.when(pl.program_id(1) == 0)
    def _():
        st_ref[...] = jnp.zeros_like(st_ref)

    n_gt = gt_ref.shape[2]
    n = x_ref.shape[1] // ROW_GROUPS
    groups = [slice(r * n, (r + 1) * n) for r in range(ROW_GROUPS)]
    hs = [_rms(x_ref[0, r, :], g_ref[...]).astype(BF16) for r in groups]
    projs = [jnp.dot(h, wproj_ref[...], preferred_element_type=F32) for h in hs]
    q_bs, i_bs, decs = [], [], []
    for r, proj in zip(groups, projs):
        q, i, g = proj[:, :W], proj[:, W:2 * W], proj[:, 2 * W:3 * W]
        q_bs.append((q * _sigmoid(q)).astype(BF16))
        i_bs.append(i.astype(BF16))
        act_ref[0, r, :] = jnp.concatenate([q_bs[-1], i_bs[-1], (g * _sigmoid(g)).astype(BF16)],
                                           axis=1)
        decs.append(proj[:, 3 * W:5 * W])
        fb_ref[0, r, :] = decs[-1][:, W:]
        gt_ref[0, r, :] = _sigmoid(proj[:, 5 * W:5 * W + n_gt]).astype(BF16)
    k_sq = [_mla_tile(proj[:, 5 * W + n_gt:], r, gqa_ref, gkva_ref, wq_ref, wk_ref, wv_ref,
                      cos_t_ref, sin_t_ref, cosk_ref, sina_ref, sinb_ref,
                      qt_ref, k_ref, vt_ref, qsq_ref) for r, proj in zip(groups, projs)]
    ksq_ref[0, 0] = functools.reduce(jnp.maximum, k_sq)
    q_b = jnp.concatenate(q_bs, axis=0)
    i_b = jnp.concatenate(i_bs, axis=0)
    dec = jnp.concatenate(decs, axis=0)

    def emit(c, hd, o):
        ofw_ref[0, c * C:(c + 1) * C, hd * HG_HEAD_DIM:(hd + 1) * HG_HEAD_DIM] = o

    rs = lambda c: slice(c * C, (c + 1) * C)
    _hgrn_tile(lambda c: q_b[rs(c)].astype(F32), lambda c: i_b[rs(c)], lambda c: dec[rs(c), :W],
               lb_ref[...], st_ref, emit, nc=x_ref.shape[1] // C, chunk=C, reverse=False)


def _front(x, p, g_mix, g_qa, g_kva, tabs, *, ts):
    B, S, D = x.shape
    W = HG_WIDTH
    H = p["wq_t"].shape[0] // QK_PAD
    half = QK_ROPE // 2
    tok = lambda w: pl.BlockSpec((1, ts, w), lambda b, s: (b, s, 0))
    consts = (g_mix, p["w_proj"], p["lb_fw"],
              g_qa, g_kva, p["wq_t"], p["wk"], p["wv_t"])
    lane_tab = pl.BlockSpec((half, ts), lambda b, s: (0, s))
    row_tab = pl.BlockSpec((ts, QK_PAD), lambda b, s: (s, 0))
    return pl.pallas_call(
        functools.partial(_front_kernel, chunk=HG_CHUNK),
        grid=(B, S // ts),
        in_specs=([tok(D)] + [_resident(c.shape) for c in consts]
                  + [lane_tab, lane_tab, row_tab, row_tab, row_tab]),
        out_specs=[
            tok(3 * W), tok(W), tok(p["n_gt"]), tok(W),
            pl.BlockSpec((1, H, QK_PAD, ts), lambda b, s: (b, 0, 0, s)),
            pl.BlockSpec((1, H, ts, QK_PAD), lambda b, s: (b, 0, s, 0)),
            pl.BlockSpec((1, H, V_AUG, ts), lambda b, s: (b, 0, 0, s)),
            pl.BlockSpec((1, 1, H, QK_PAD), lambda b, s: (b, s, 0, 0)),
            pl.BlockSpec((1, H, 1, ts), lambda b, s: (b, 0, 0, s)),
        ],
        out_shape=[
            jax.ShapeDtypeStruct((B, S, 3 * W), BF16),
            jax.ShapeDtypeStruct((B, S, W), F32),
            jax.ShapeDtypeStruct((B, S, p["n_gt"]), BF16),
            jax.ShapeDtypeStruct((B, S, W), F32),
            jax.ShapeDtypeStruct((B, H, QK_PAD, S), BF16),
            jax.ShapeDtypeStruct((B, H, S, QK_PAD), BF16),
            jax.ShapeDtypeStruct((B, H, V_AUG, S), BF16),
            jax.ShapeDtypeStruct((B, S // ts, H, QK_PAD), F32),
            jax.ShapeDtypeStruct((B, H, 1, S), F32),
        ],
        scratch_shapes=[pltpu.VMEM((HG_HEADS, HG_HEAD_DIM, HG_HEAD_DIM), F32)],
        compiler_params=_params("parallel", "arbitrary"),
        name="front",
    )(x, *consts, *tabs)


def _hgrn_bwd_kernel(q_ref, v_ref, f_ref, lb_ref, og_ref, ofw_ref, gon_ref, o_ref, st_ref, *,
                     chunk):
    C = chunk

    @pl.when(pl.program_id(1) == 0)
    def _():
        st_ref[...] = jnp.zeros_like(st_ref)

    def emit(c, hd, o):
        rs = slice(c * C, (c + 1) * C)
        ls = slice(hd * HG_HEAD_DIM, (hd + 1) * HG_HEAD_DIM)
        o = o + ofw_ref[0, rs, ls]
        o = _rms(o, gon_ref[:, ls]) * og_ref[0, rs, ls].astype(F32)
        o_ref[0, rs, ls] = o.astype(o_ref.dtype)

    rs = lambda c: slice(c * C, (c + 1) * C)
    _hgrn_tile(lambda c: q_ref[0, rs(c), :].astype(F32), lambda c: v_ref[0, rs(c), :],
               lambda c: f_ref[0, rs(c), :], lb_ref[...], st_ref, emit,
               nc=q_ref.shape[1] // C, chunk=C, reverse=True)


def _hgrn_bwd(act, fb, lb_row, o_fw, g_onorm, *, ts):
    B, S, W = fb.shape
    ns = S // ts
    col = lambda j: pl.BlockSpec((1, ts, W), lambda b, s: (b, ns - 1 - s, j))
    vec = pl.BlockSpec((1, W), lambda b, s: (0, 0))
    return pl.pallas_call(
        functools.partial(_hgrn_bwd_kernel, chunk=HG_CHUNK),
        grid=(B, ns),
        in_specs=[col(0), col(1), col(0), vec, col(2), col(0), vec],
        out_specs=col(0),
        out_shape=jax.ShapeDtypeStruct((B, S, W), BF16),
        scratch_shapes=[pltpu.VMEM((HG_HEADS, HG_HEAD_DIM, HG_HEAD_DIM), F32)],
        compiler_params=_params("parallel", "arbitrary"),
        name="hgrn_bwd",
    )(act, act, fb, lb_row, act, o_fw, g_onorm)


def _attn_kernel(qt_ref, k_ref, vt_ref, ksq_ref, qsq_ref, o_ref, den_ref, acc_ref, *, tk):
    q_t = qt_ref[0, 0]
    tq = q_t.shape[1]
    k_sq_max = jnp.max(jnp.max(ksq_ref[0, 0], axis=1, keepdims=True), axis=0, keepdims=True)
    bound = jnp.sqrt(qsq_ref[0, 0] * k_sq_max) * BOUND_SLACK
    off_rows = jnp.concatenate([-bound, jnp.zeros((OFF_ROWS - 1, tq), F32)], axis=0).astype(BF16)
    q_off = jnp.concatenate([q_t[:QK_HEAD], off_rows, q_t[QK_HEAD + OFF_ROWS:]], axis=0)

    def step(j, carry):
        k0 = pl.multiple_of(j * tk, tk)
        s = jnp.dot(k_ref[0, 0, pl.ds(k0, tk), :], q_off, preferred_element_type=F32)
        p = jnp.exp2(s).astype(BF16)
        acc_ref[...] += jnp.dot(vt_ref[0, 0, :, pl.ds(k0, tk)], p, preferred_element_type=F32)
        return carry

    acc_ref[...] = jnp.zeros_like(acc_ref)
    lax.fori_loop(0, k_ref.shape[2] // tk, step, 0)
    acc = acc_ref[...]
    den = acc[V_HEAD:V_HEAD + 1]
    den_ref[0, 0] = den
    o_ref[0, 0] = (acc[:V_HEAD] / den).astype(o_ref.dtype)


def _attention(q_t, k, v_t, k_sq, q_sq, *, tq, tk):
    B, H, _, S = q_t.shape
    VA = v_t.shape[2]
    per_q = lambda rows: pl.BlockSpec((1, 1, rows, tq), lambda b, h, i: (b, h, 0, i))
    whole = lambda a: pl.BlockSpec((1, 1) + a.shape[2:], lambda b, h, i: (b, h, 0, 0))
    return pl.pallas_call(
        functools.partial(_attn_kernel, tk=tk),
        grid=(B, H, S // tq),
        in_specs=[per_q(QK_PAD), whole(k), whole(v_t), whole(k_sq), per_q(1)],
        out_specs=[per_q(V_HEAD), per_q(1)],
        out_shape=[jax.ShapeDtypeStruct((B, H, V_HEAD, S), BF16),
                   jax.ShapeDtypeStruct((B, H, 1, S), F32)],
        scratch_shapes=[pltpu.VMEM((VA, tq), F32)],
        compiler_params=_params("parallel", "parallel", "arbitrary"),
        name="attention",
    )(q_t, k, v_t, k_sq, q_sq)


def _attn_online_kernel(qt_ref, k_ref, vt_ref, o_ref, s_ref, acc_ref, m_ref, *, tk):
    q_t = qt_ref[0, 0]
    nk = k_ref.shape[2] // tk

    def scores(j, slot):
        k0 = pl.multiple_of(jnp.minimum(j, nk - 1) * tk, tk)
        s_ref[slot] = jnp.dot(k_ref[0, 0, pl.ds(k0, tk), :], q_t, preferred_element_type=F32)

    def softmax_pv(j, slot):
        s = s_ref[slot]
        m_old = m_ref[...]
        m_new = jnp.maximum(m_old, jnp.max(s, axis=0, keepdims=True))
        p = jnp.exp2(s - m_new).astype(BF16)
        k0 = pl.multiple_of(j * tk, tk)
        pv = jnp.dot(vt_ref[0, 0, :, pl.ds(k0, tk)], p, preferred_element_type=F32)
        acc_ref[...] = jnp.exp2(m_old - m_new) * acc_ref[...] + pv
        m_ref[...] = m_new

    def pair(i, carry):
        j = 2 * i
        scores(j + 1, 1)
        softmax_pv(j, 0)
        scores(j + 2, 0)
        softmax_pv(j + 1, 1)
        return carry

    m_ref[...] = jnp.full_like(m_ref, -1e30)
    acc_ref[...] = jnp.zeros_like(acc_ref)
    scores(0, 0)
    lax.fori_loop(0, nk // 2, pair, 0)
    acc = acc_ref[...]
    o_ref[0, 0] = (acc[:V_HEAD] / acc[V_HEAD:V_HEAD + 1]).astype(o_ref.dtype)


def _attention_online(q_t, k, v_t, *, tq, tk):
    B, H, _, S = q_t.shape
    VA = v_t.shape[2]
    assert (S // tk) % 2 == 0
    per_q = lambda rows: pl.BlockSpec((1, 1, rows, tq), lambda b, h, i: (b, h, 0, i))
    whole = lambda a: pl.BlockSpec((1, 1) + a.shape[2:], lambda b, h, i: (b, h, 0, 0))
    return pl.pallas_call(
        functools.partial(_attn_online_kernel, tk=tk),
        grid=(B, H, S // tq),
        in_specs=[per_q(QK_PAD), whole(k), whole(v_t)],
        out_specs=per_q(V_HEAD),
        out_shape=jax.ShapeDtypeStruct((B, H, V_HEAD, S), BF16),
        scratch_shapes=[pltpu.VMEM((2, tk, tq), F32), pltpu.VMEM((VA, tq), F32),
                        pltpu.VMEM((1, tq), F32)],
        compiler_params=_params("parallel", "parallel", "arbitrary"),
        name="attention_online",
    )(q_t, k, v_t)


def _mix_ffn_kernel(x_ref, oh_ref, omt_ref, gt_ref, wb0_ref, wb1_ref, wo_ref,
                    gf_ref, wgu_ref, wd_ref, gfin_ref, y_ref):
    D = x_ref.shape[2]
    dff = wd_ref.shape[0]
    tm = x_ref.shape[1]
    rows = [slice(r * (tm // ROW_GROUPS), (r + 1) * (tm // ROW_GROUPS)) for r in range(ROW_GROUPS)]
    dot = functools.partial(jnp.dot, preferred_element_type=F32)
    bp0 = [dot(oh_ref[0, r, :], wb0_ref[...]) for r in rows]
    bp1 = [lax.dot_general(omt_ref[0, :, r], wb1_ref[...], TN_DIMS, preferred_element_type=F32)
           for r in rows]
    gt = [gt_ref[0, r, :].astype(F32) for r in rows]
    merged = [(g[:, :D] * a + g[:, D:] * b).astype(BF16) for g, a, b in zip(gt, bp0, bp1)]
    x1 = [x_ref[0, r, :] + dot(m, wo_ref[...]) for r, m in zip(rows, merged)]
    h2 = [_rms(x, gf_ref[...]).astype(BF16) for x in x1]
    gu = [dot(h, wgu_ref[...]) for h in h2]
    act = [((u[:, :dff] * _sigmoid(u[:, :dff])) * u[:, dff:]).astype(BF16) for u in gu]
    x2 = [x + dot(a, wd_ref[...]) for x, a in zip(x1, act)]
    for r, x in zip(rows, x2):
        y_ref[0, r, :] = _rms(x, gfin_ref[...])


def _mix_ffn(x, o_h, o_mt, gt, wb0, wb1, wo, g_ffn, wgu, wd, g_final, *, tm):
    B, S, D = x.shape
    tok = lambda w: pl.BlockSpec((1, tm, w), lambda b, s: (b, s, 0))
    consts = (wb0, wb1, wo, g_ffn, wgu, wd, g_final)
    return pl.pallas_call(
        _mix_ffn_kernel,
        grid=(B, S // tm),
        in_specs=[
            tok(D), tok(o_h.shape[2]),
            pl.BlockSpec((1, o_mt.shape[1], tm), lambda b, s: (b, 0, s)),
            tok(gt.shape[2]),
        ] + [_resident(c.shape) for c in consts],
        out_specs=tok(D),
        out_shape=jax.ShapeDtypeStruct((B, S, D), F32),
        compiler_params=_params("parallel", "parallel"),
        name="mix_ffn",
    )(x, o_h, o_mt, gt, *consts)


def _rope_tables(S):
    d = QK_ROPE
    half = d // 2
    inv = ROPE_THETA ** (-jnp.arange(0, d, 2, dtype=F32) / d)
    ang = jnp.arange(S, dtype=F32)[:, None] * inv[None, :]
    cos, sin = jnp.cos(ang), jnp.sin(ang)
    z = lambda n: jnp.zeros((S, n), F32)
    tail = QK_PAD - QK_HEAD
    cosk = jnp.concatenate([z(QK_NOPE), cos, cos, z(tail)], axis=1)
    sina = jnp.concatenate([z(QK_NOPE + half), sin, z(tail)], axis=1)
    sinb = jnp.concatenate([z(QK_NOPE), -sin, z(half + tail)], axis=1)
    return cos.T, sin.T, cosk, sina, sinb


def _prep_weights(w_in, lb_param, w_uq, w_ukv, w_branch, w_out, w_gate_up, w_down):
    W = HG_WIDTH
    r_q, H = w_uq.shape[1], w_uq.shape[2]
    r_kv = w_ukv.shape[1]
    w = w_in[0]
    D = w.shape[0]
    c0 = 5 * W
    w_act = jnp.concatenate([w[:, :2 * W], w[:, 4 * W:c0]], axis=1)
    w_dec = w[:, 2 * W:4 * W]
    w_qa = w[:, c0:c0 + r_q]
    w_kva = w[:, c0 + r_q:c0 + r_q + r_kv]
    c1 = c0 + r_q + r_kv
    w_kr = w[:, c1:c1 + QK_ROPE]
    w_gt = w[:, c1 + QK_ROPE:]
    zpad = lambda n: jnp.zeros((D, n), w.dtype)
    w_mla = jnp.concatenate([w_qa, w_kva, zpad(QK_NOPE), w_kr, zpad(QK_PAD - QK_HEAD)], axis=1)
    wq = jnp.transpose(w_uq[0], (1, 2, 0))
    wq_t = jnp.concatenate([wq, jnp.zeros((H, QK_PAD - QK_HEAD, r_q), wq.dtype)], axis=1)
    wq_t = wq_t.reshape(H * QK_PAD, r_q)
    wkv = jnp.transpose(w_ukv[0], (1, 0, 2))
    wk = jnp.concatenate([wkv[:, :, :QK_NOPE],
                          jnp.zeros((H, r_kv, QK_PAD - QK_NOPE), wkv.dtype)], axis=2)
    wk = jnp.transpose(wk, (1, 0, 2)).reshape(r_kv, H * QK_PAD)
    wv_t = jnp.transpose(wkv[:, :, QK_NOPE:], (0, 2, 1)).reshape(H * V_HEAD, r_kv)
    lb = jax.nn.softmax(lb_param.astype(F32), axis=0)[0]
    b16 = lambda a: a.astype(BF16)
    return dict(
        w_proj=b16(jnp.concatenate([w_act, w_dec, w_gt, w_mla], axis=1)), n_gt=w_gt.shape[1],
        wq_t=b16(wq_t), wk=b16(wk), wv_t=b16(wv_t),
        wb0=b16(w_branch[0, 0]), wb1=b16(w_branch[0, 1]), wo=b16(w_out[0]),
        wgu=b16(w_gate_up[0]), wd=b16(w_down[0]),
        lb_fw=lb[0:1], lb_bw=lb[1:2],
    )


SEQ_TILE = 512
BWD_TILE = 1024
ATTN_Q_TILE = 1024
ATTN_K_BLOCK = 8192
ONLINE_Q_TILE = 512
ONLINE_K_TILE = 1024


def _pick(n, pref):
    t = min(n, pref)
    while n % t:
        t //= 2
    return t


def _trunk(x, p, g_mix, g_onorm, g_qa, g_kva, g_ffn, g_final):
    B, S, D = x.shape
    ts = _pick(S, SEQ_TILE)
    act, fb, gt, o_fw, q_t, k, v_t, k_sq, q_sq = _front(x, p, g_mix, g_qa, g_kva, _rope_tables(S),
                                                        ts=ts)
    o_h = _hgrn_bwd(act, fb, p["lb_bw"], o_fw, g_onorm, ts=_pick(S, BWD_TILE))
    k_sq = jnp.transpose(k_sq, (0, 2, 1, 3))
    o_mt, den = _attention(q_t, k, v_t, k_sq, q_sq, tq=_pick(S, ATTN_Q_TILE),
                           tk=_pick(S, ATTN_K_BLOCK))
    o_mt = lax.cond(jnp.min(den) >= DENOM_FLOOR, lambda: o_mt,
                    lambda: _attention_online(q_t, k, v_t, tq=_pick(S, ONLINE_Q_TILE),
                                              tk=_pick(S // 2, ONLINE_K_TILE)))
    o_mt = o_mt.reshape(B, -1, S)

    return _mix_ffn(x, o_h, o_mt, gt, p["wb0"], p["wb1"], p["wo"], g_ffn, p["wgu"], p["wd"],
                    g_final[None, :], tm=ts)


def kernel(x_prompt, x_sample, g_mix, w_in, lb_param, g_onorm, g_qa, w_uq, g_kva, w_ukv,
           w_branch, w_out, g_ffn, w_gate_up, w_down, g_final):
    p = _prep_weights(w_in, lb_param, w_uq, w_ukv, w_branch, w_out, w_gate_up, w_down)
    args = (p, g_mix, g_onorm, g_qa, g_kva, g_ffn, g_final)
    return (_trunk(x_prompt, *args), _trunk(x_sample, *args))
```

```python
import functools

import jax
import jax.numpy as jnp
from jax import lax
from jax.experimental import pallas as pl
from jax.experimental.pallas import tpu as pltpu

F32 = jnp.float32
BF16 = jnp.bfloat16

EPS = 1e-6
ROPE_THETA = 10000.0
HG_HEADS = 4
HG_HEAD_DIM = 128
HG_WIDTH = HG_HEADS * HG_HEAD_DIM
HG_CHUNK = 64
QK_NOPE = 64
QK_ROPE = 32
V_HEAD = 64
QK_HEAD = QK_NOPE + QK_ROPE
QK_PAD = 128
V_AUG = V_HEAD + 16
LOG2E = 1.4426950408889634
BOUND_SLACK = 1.0 + 2.0 ** -6
DENOM_FLOOR = 2.0 ** -64
OFF_ROWS = 16
ROW_GROUPS = 2
VMEM_LIMIT_BYTES = 56 * 1024 * 1024

NT_DIMS = (((1,), (1,)), ((), ()))
TN_DIMS = (((0,), (0,)), ((), ()))


def _sigmoid(x):
    return 1.0 / (1.0 + jnp.exp(-x))


def _rms(x, g):
    ms = jnp.mean(x * x, axis=-1, keepdims=True)
    return x * lax.rsqrt(ms + EPS) * g


def _params(*sem):
    return pltpu.CompilerParams(dimension_semantics=sem, vmem_limit_bytes=VMEM_LIMIT_BYTES)


def _resident(shape):
    zeros = (0,) * len(shape)
    return pl.BlockSpec(shape, lambda *_: zeros, pipeline_mode=pl.Buffered(1))


def _hgrn_tile(q_of, v_of, f_of, lb, st_ref, emit, *, nc, chunk, reverse):
    C = chunk
    D = HG_HEAD_DIM
    heads = range(HG_HEADS)
    rows = lax.broadcasted_iota(jnp.int32, (C, C), 0)
    cols = lax.broadcasted_iota(jnp.int32, (C, C), 1)
    tri = (cols >= rows) if reverse else (cols <= rows)
    tri_b = tri.astype(BF16)
    mid = C // 2 if reverse else C // 2 - 1
    last = 0 if reverse else C - 1

    q_in, k_in, q_inter, k_state, v_b, decay = [], [], [], [], [], []
    for c in range(nc):
        f = lb + (1.0 - lb) * _sigmoid(f_of(c))
        g = jnp.log2(f)
        kk = 1.0 - f
        g_hi = g.astype(BF16)
        g_lo = (g - g_hi.astype(F32)).astype(BF16)
        b = (jnp.dot(tri_b, g_hi, preferred_element_type=F32)
             + jnp.dot(tri_b, g_lo, preferred_element_type=F32))
        ref = b[mid:mid + 1, :]
        b_last = b[last:last + 1, :]
        qi = q_of(c) * jnp.exp2(b - ref)
        ki = kk * jnp.exp2(ref - b)
        q_in.append(qi.astype(BF16))
        k_in.append(ki.astype(BF16))
        q_inter.append((qi * jnp.exp2(ref)).astype(BF16))
        k_state.append((ki * jnp.exp2(b_last - ref)).astype(BF16))
        v_b.append(v_of(c))
        decay.append(jnp.exp2(b_last))

    o_intra = [[None] * HG_HEADS for _ in range(nc)]
    incr = [[None] * HG_HEADS for _ in range(nc)]
    for c in range(nc):
        for h in heads:
            ls = slice(h * D, (h + 1) * D)
            att = lax.dot_general(q_in[c][:, ls], k_in[c][:, ls], NT_DIMS,
                                  preferred_element_type=F32)
            att = jnp.where(tri, att, 0.0).astype(BF16)
            o_intra[c][h] = jnp.dot(att, v_b[c][:, ls], preferred_element_type=F32)
            incr[c][h] = lax.dot_general(v_b[c][:, ls], k_state[c][:, ls], TN_DIMS,
                                         preferred_element_type=F32)

    st = [st_ref[h] for h in heads]
    for c in (range(nc - 1, -1, -1) if reverse else range(nc)):
        for h in heads:
            ls = slice(h * D, (h + 1) * D)
            o = o_intra[c][h] + lax.dot_general(q_inter[c][:, ls], st[h].astype(BF16), NT_DIMS,
                                                preferred_element_type=F32)
            st[h] = st[h] * decay[c][:, ls] + incr[c][h]
            emit(c, h, o)
    for h in heads:
        st_ref[h] = st[h]


def _mla_tile(mla, tok, gqa_ref, gkva_ref, wq_ref, wk_ref, wv_ref,
              cos_t_ref, sin_t_ref, cosk_ref, sina_ref, sinb_ref,
              qt_ref, k_ref, vt_ref, qsq_ref):
    heads = wq_ref.shape[0] // QK_PAD
    scale = QK_HEAD ** -0.5 * LOG2E
    r = gqa_ref.shape[1]
    cq = _rms(mla[:, 0:r], gqa_ref[...]).astype(BF16)
    ckv = _rms(mla[:, r:2 * r], gkva_ref[...]).astype(BF16)
    kr = mla[:, 2 * r:2 * r + QK_PAD]
    half = QK_ROPE // 2
    k_pe = (kr * cosk_ref[tok, :]
            + pltpu.roll(kr, half, 1) * sina_ref[tok, :]
            + pltpu.roll(kr, QK_PAD - half, 1) * sinb_ref[tok, :])
    cos_t = cos_t_ref[:, tok]
    sin_t = sin_t_ref[:, tok]
    q_all = lax.dot_general(wq_ref[...], cq, NT_DIMS, preferred_element_type=F32)
    k_all = jnp.dot(ckv, wk_ref[...], preferred_element_type=F32)
    v_all = lax.dot_general(wv_ref[...], ckv, NT_DIMS, preferred_element_type=F32)
    pad_rows = vt_ref.shape[2] - V_HEAD
    ones_rows = (lax.broadcasted_iota(jnp.int32, (pad_rows, cos_t.shape[1]), 0) == 0).astype(F32)
    k_sq_rows = []
    one_lane = (lax.broadcasted_iota(jnp.int32, (1, QK_PAD), 1) == QK_HEAD).astype(F32)
    for h in range(heads):
        q_t = q_all[h * QK_PAD:(h + 1) * QK_PAD]
        x1 = q_t[QK_NOPE:QK_NOPE + half]
        x2 = q_t[QK_NOPE + half:QK_HEAD]
        q_rot = jnp.concatenate(
            [q_t[:QK_NOPE], x1 * cos_t - x2 * sin_t, x2 * cos_t + x1 * sin_t, q_t[QK_HEAD:]],
            axis=0)
        q_b = (q_rot * scale).astype(qt_ref.dtype)
        qt_ref[0, h, :, tok] = q_b
        q_f = q_b.astype(F32)
        qsq_ref[0, h, :, tok] = jnp.sum(q_f * q_f, axis=0, keepdims=True)
        k_b = (k_all[:, h * QK_PAD:(h + 1) * QK_PAD] + k_pe).astype(BF16)
        k_f = k_b.astype(F32)
        k_sq = jnp.sum(k_f * k_f, axis=1, keepdims=True)
        k_sq_rows.append(jnp.broadcast_to(jnp.max(k_sq, axis=0, keepdims=True), (1, QK_PAD)))
        k_ref[0, h, tok, :] = (k_f + one_lane).astype(k_ref.dtype)
        v_t = v_all[h * V_HEAD:(h + 1) * V_HEAD]
        vt_ref[0, h, :, tok] = jnp.concatenate([v_t, ones_rows], axis=0).astype(vt_ref.dtype)
    return jnp.concatenate(k_sq_rows, axis=0)


def _front_kernel(x_ref, g_ref, wproj_ref, lb_ref,
                  gqa_ref, gkva_ref, wq_ref, wk_ref, wv_ref,
                  cos_t_ref, sin_t_ref, cosk_ref, sina_ref, sinb_ref,
                  act_ref, fb_ref, gt_ref, ofw_ref, qt_ref, k_ref, vt_ref, ksq_ref, qsq_ref, st_ref,
                  *, chunk):
    W = HG_WIDTH
    C = chunk

    @pl.when(pl.program_id(1) == 0)
    def _():
        st_ref[...] = jnp.zeros_like(st_ref)

    n_gt = gt_ref.shape[2]
    n = x_ref.shape[1] // ROW_GROUPS
    groups = [slice(r * n, (r + 1) * n) for r in range(ROW_GROUPS)]
    hs = [_rms(x_ref[0, r, :], g_ref[...]).astype(BF16) for r in groups]
    projs = [jnp.dot(h, wproj_ref[...], preferred_element_type=F32) for h in hs]
    q_bs, i_bs, decs = [], [], []
    for r, proj in zip(groups, projs):
        q, i, g = proj[:, :W], proj[:, W:2 * W], proj[:, 2 * W:3 * W]
        q_bs.append((q * _sigmoid(q)).astype(BF16))
        i_bs.append(i.astype(BF16))
        act_ref[0, r, :] = jnp.concatenate([q_bs[-1], i_bs[-1], (g * _sigmoid(g)).astype(BF16)],
                                           axis=1)
        decs.append(proj[:, 3 * W:5 * W])
        fb_ref[0, r, :] = decs[-1][:, W:]
        gt_ref[0, r, :] = _sigmoid(proj[:, 5 * W:5 * W + n_gt]).astype(BF16)
    k_sq = [_mla_tile(proj[:, 5 * W + n_gt:], r, gqa_ref, gkva_ref, wq_ref, wk_ref, wv_ref,
                      cos_t_ref, sin_t_ref, cosk_ref, sina_ref, sinb_ref,
                      qt_ref, k_ref, vt_ref, qsq_ref) for r, proj in zip(groups, projs)]
    ksq_ref[0, 0] = functools.reduce(jnp.maximum, k_sq)
    q_b = jnp.concatenate(q_bs, axis=0)
    i_b = jnp.concatenate(i_bs, axis=0)
    dec = jnp.concatenate(decs, axis=0)

    def emit(c, hd, o):
        ofw_ref[0, c * C:(c + 1) * C, hd * HG_HEAD_DIM:(hd + 1) * HG_HEAD_DIM] = o

    rs = lambda c: slice(c * C, (c + 1) * C)
    _hgrn_tile(lambda c: q_b[rs(c)].astype(F32), lambda c: i_b[rs(c)], lambda c: dec[rs(c), :W],
               lb_ref[...], st_ref, emit, nc=x_ref.shape[1] // C, chunk=C, reverse=False)


def _front(x, p, g_mix, g_qa, g_kva, tabs, *, ts):
    B, S, D = x.shape
    W = HG_WIDTH
    H = p["wq_t"].shape[0] // QK_PAD
    half = QK_ROPE // 2
    tok = lambda w: pl.BlockSpec((1, ts, w), lambda b, s: (b, s, 0))
    consts = (g_mix, p["w_proj"], p["lb_fw"],
              g_qa, g_kva, p["wq_t"], p["wk"], p["wv_t"])
    lane_tab = pl.BlockSpec((half, ts), lambda b, s: (0, s))
    row_tab = pl.BlockSpec((ts, QK_PAD), lambda b, s: (s, 0))
    return pl.pallas_call(
        functools.partial(_front_kernel, chunk=HG_CHUNK),
        grid=(B, S // ts),
        in_specs=([tok(D)] + [_resident(c.shape) for c in consts]
                  + [lane_tab, lane_tab, row_tab, row_tab, row_tab]),
        out_specs=[
            tok(3 * W), tok(W), tok(p["n_gt"]), tok(W),
            pl.BlockSpec((1, H, QK_PAD, ts), lambda b, s: (b, 0, 0, s)),
            pl.BlockSpec((1, H, ts, QK_PAD), lambda b, s: (b, 0, s, 0)),
            pl.BlockSpec((1, H, V_AUG, ts), lambda b, s: (b, 0, 0, s)),
            pl.BlockSpec((1, 1, H, QK_PAD), lambda b, s: (b, s, 0, 0)),
            pl.BlockSpec((1, H, 1, ts), lambda b, s: (b, 0, 0, s)),
        ],
        out_shape=[
            jax.ShapeDtypeStruct((B, S, 3 * W), BF16),
            jax.ShapeDtypeStruct((B, S, W), F32),
            jax.ShapeDtypeStruct((B, S, p["n_gt"]), BF16),
            jax.ShapeDtypeStruct((B, S, W), F32),
            jax.ShapeDtypeStruct((B, H, QK_PAD, S), BF16),
            jax.ShapeDtypeStruct((B, H, S, QK_PAD), BF16),
            jax.ShapeDtypeStruct((B, H, V_AUG, S), BF16),
            jax.ShapeDtypeStruct((B, S // ts, H, QK_PAD), F32),
            jax.ShapeDtypeStruct((B, H, 1, S), F32),
        ],
        scratch_shapes=[pltpu.VMEM((HG_HEADS, HG_HEAD_DIM, HG_HEAD_DIM), F32)],
        compiler_params=_params("parallel", "arbitrary"),
        name="front",
    )(x, *consts, *tabs)


def _hgrn_bwd_kernel(q_ref, v_ref, f_ref, lb_ref, og_ref, ofw_ref, gon_ref, o_ref, st_ref, *,
                     chunk):
    C = chunk

    @pl.when(pl.program_id(1) == 0)
    def _():
        st_ref[...] = jnp.zeros_like(st_ref)

    def emit(c, hd, o):
        rs = slice(c * C, (c + 1) * C)
        ls = slice(hd * HG_HEAD_DIM, (hd + 1) * HG_HEAD_DIM)
        o = o + ofw_ref[0, rs, ls]
        o = _rms(o, gon_ref[:, ls]) * og_ref[0, rs, ls].astype(F32)
        o_ref[0, rs, ls] = o.astype(o_ref.dtype)

    rs = lambda c: slice(c * C, (c + 1) * C)
    _hgrn_tile(lambda c: q_ref[0, rs(c), :].astype(F32), lambda c: v_ref[0, rs(c), :],
               lambda c: f_ref[0, rs(c), :], lb_ref[...], st_ref, emit,
               nc=q_ref.shape[1] // C, chunk=C, reverse=True)


def _hgrn_bwd(act, fb, lb_row, o_fw, g_onorm, *, ts):
    B, S, W = fb.shape
    ns = S // ts
    col = lambda j: pl.BlockSpec((1, ts, W), lambda b, s: (b, ns - 1 - s, j))
    vec = pl.BlockSpec((1, W), lambda b, s: (0, 0))
    return pl.pallas_call(
        functools.partial(_hgrn_bwd_kernel, chunk=HG_CHUNK),
        grid=(B, ns),
        in_specs=[col(0), col(1), col(0), vec, col(2), col(0), vec],
        out_specs=col(0),
        out_shape=jax.ShapeDtypeStruct((B, S, W), BF16),
        scratch_shapes=[pltpu.VMEM((HG_HEADS, HG_HEAD_DIM, HG_HEAD_DIM), F32)],
        compiler_params=_params("parallel", "arbitrary"),
        name="hgrn_bwd",
    )(act, act, fb, lb_row, act, o_fw, g_onorm)


def _attn_kernel(qt_ref, k_ref, vt_ref, ksq_ref, qsq_ref, o_ref, den_ref, acc_ref, *, tk):
    q_t = qt_ref[0, 0]
    tq = q_t.shape[1]
    k_sq_max = jnp.max(jnp.max(ksq_ref[0, 0], axis=1, keepdims=True), axis=0, keepdims=True)
    bound = jnp.sqrt(qsq_ref[0, 0] * k_sq_max) * BOUND_SLACK
    off_rows = jnp.concatenate([-bound, jnp.zeros((OFF_ROWS - 1, tq), F32)], axis=0).astype(BF16)
    q_off = jnp.concatenate([q_t[:QK_HEAD], off_rows, q_t[QK_HEAD + OFF_ROWS:]], axis=0)

    def step(j, carry):
        k0 = pl.multiple_of(j * tk, tk)
        s = jnp.dot(k_ref[0, 0, pl.ds(k0, tk), :], q_off, preferred_element_type=F32)
        p = jnp.exp2(s).astype(BF16)
        acc_ref[...] += jnp.dot(vt_ref[0, 0, :, pl.ds(k0, tk)], p, preferred_element_type=F32)
        return carry

    acc_ref[...] = jnp.zeros_like(acc_ref)
    lax.fori_loop(0, k_ref.shape[2] // tk, step, 0)
    acc = acc_ref[...]
    den = acc[V_HEAD:V_HEAD + 1]
    den_ref[0, 0] = den
    o_ref[0, 0] = (acc[:V_HEAD] / den).astype(o_ref.dtype)


def _attention(q_t, k, v_t, k_sq, q_sq, *, tq, tk):
    B, H, _, S = q_t.shape
    VA = v_t.shape[2]
    per_q = lambda rows: pl.BlockSpec((1, 1, rows, tq), lambda b, h, i: (b, h, 0, i))
    whole = lambda a: pl.BlockSpec((1, 1) + a.shape[2:], lambda b, h, i: (b, h, 0, 0))
    return pl.pallas_call(
        functools.partial(_attn_kernel, tk=tk),
        grid=(B, H, S // tq),
        in_specs=[per_q(QK_PAD), whole(k), whole(v_t), whole(k_sq), per_q(1)],
        out_specs=[per_q(V_HEAD), per_q(1)],
        out_shape=[jax.ShapeDtypeStruct((B, H, V_HEAD, S), BF16),
                   jax.ShapeDtypeStruct((B, H, 1, S), F32)],
        scratch_shapes=[pltpu.VMEM((VA, tq), F32)],
        compiler_params=_params("parallel", "parallel", "arbitrary"),
        name="attention",
    )(q_t, k, v_t, k_sq, q_sq)


def _attn_online_kernel(qt_ref, k_ref, vt_ref, o_ref, s_ref, acc_ref, m_ref, *, tk):
    q_t = qt_ref[0, 0]
    nk = k_ref.shape[2] // tk

    def scores(j, slot):
        k0 = pl.multiple_of(jnp.minimum(j, nk - 1) * tk, tk)
        s_ref[slot] = jnp.dot(k_ref[0, 0, pl.ds(k0, tk), :], q_t, preferred_element_type=F32)

    def softmax_pv(j, slot):
        s = s_ref[slot]
        m_old = m_ref[...]
        m_new = jnp.maximum(m_old, jnp.max(s, axis=0, keepdims=True))
        p = jnp.exp2(s - m_new).astype(BF16)
        k0 = pl.multiple_of(j * tk, tk)
        pv = jnp.dot(vt_ref[0, 0, :, pl.ds(k0, tk)], p, preferred_element_type=F32)
        acc_ref[...] = jnp.exp2(m_old - m_new) * acc_ref[...] + pv
        m_ref[...] = m_new

    def pair(i, carry):
        j = 2 * i
        scores(j + 1, 1)
        softmax_pv(j, 0)
        scores(j + 2, 0)
        softmax_pv(j + 1, 1)
        return carry

    m_ref[...] = jnp.full_like(m_ref, -1e30)
    acc_ref[...] = jnp.zeros_like(acc_ref)
    scores(0, 0)
    lax.fori_loop(0, nk // 2, pair, 0)
    acc = acc_ref[...]
    o_ref[0, 0] = (acc[:V_HEAD] / acc[V_HEAD:V_HEAD + 1]).astype(o_ref.dtype)


def _attention_online(q_t, k, v_t, *, tq, tk):
    B, H, _, S = q_t.shape
    VA = v_t.shape[2]
    assert (S // tk) % 2 == 0
    per_q = lambda rows: pl.BlockSpec((1, 1, rows, tq), lambda b, h, i: (b, h, 0, i))
    whole = lambda a: pl.BlockSpec((1, 1) + a.shape[2:], lambda b, h, i: (b, h, 0, 0))
    return pl.pallas_call(
        functools.partial(_attn_online_kernel, tk=tk),
        grid=(B, H, S // tq),
        in_specs=[per_q(QK_PAD), whole(k), whole(v_t)],
        out_specs=per_q(V_HEAD),
        out_shape=jax.ShapeDtypeStruct((B, H, V_HEAD, S), BF16),
        scratch_shapes=[pltpu.VMEM((2, tk, tq), F32), pltpu.VMEM((VA, tq), F32),
                        pltpu.VMEM((1, tq), F32)],
        compiler_params=_params("parallel", "parallel", "arbitrary"),
        name="attention_online",
    )(q_t, k, v_t)


def _mix_ffn_kernel(x_ref, oh_ref, omt_ref, gt_ref, wb0_ref, wb1_ref, wo_ref,
                    gf_ref, wgu_ref, wd_ref, gfin_ref, y_ref):
    D = x_ref.shape[2]
    dff = wd_ref.shape[0]
    tm = x_ref.shape[1]
    rows = [slice(r * (tm // ROW_GROUPS), (r + 1) * (tm // ROW_GROUPS)) for r in range(ROW_GROUPS)]
    dot = functools.partial(jnp.dot, preferred_element_type=F32)
    bp0 = [dot(oh_ref[0, r, :], wb0_ref[...]) for r in rows]
    bp1 = [lax.dot_general(omt_ref[0, :, r], wb1_ref[...], TN_DIMS, preferred_element_type=F32)
           for r in rows]
    gt = [gt_ref[0, r, :].astype(F32) for r in rows]
    merged = [(g[:, :D] * a + g[:, D:] * b).astype(BF16) for g, a, b in zip(gt, bp0, bp1)]
    x1 = [x_ref[0, r, :] + dot(m, wo_ref[...]) for r, m in zip(rows, merged)]
    h2 = [_rms(x, gf_ref[...]).astype(BF16) for x in x1]
    gu = [dot(h, wgu_ref[...]) for h in h2]
    act = [((u[:, :dff] * _sigmoid(u[:, :dff])) * u[:, dff:]).astype(BF16) for u in gu]
    x2 = [x + dot(a, wd_ref[...]) for x, a in zip(x1, act)]
    for r, x in zip(rows, x2):
        y_ref[0, r, :] = _rms(x, gfin_ref[...])


def _mix_ffn(x, o_h, o_mt, gt, wb0, wb1, wo, g_ffn, wgu, wd, g_final, *, tm):
    B, S, D = x.shape
    tok = lambda w: pl.BlockSpec((1, tm, w), lambda b, s: (b, s, 0))
    consts = (wb0, wb1, wo, g_ffn, wgu, wd, g_final)
    return pl.pallas_call(
        _mix_ffn_kernel,
        grid=(B, S // tm),
        in_specs=[
            tok(D), tok(o_h.shape[2]),
            pl.BlockSpec((1, o_mt.shape[1], tm), lambda b, s: (b, 0, s)),
            tok(gt.shape[2]),
        ] + [_resident(c.shape) for c in consts],
        out_specs=tok(D),
        out_shape=jax.ShapeDtypeStruct((B, S, D), F32),
        compiler_params=_params("parallel", "parallel"),
        name="mix_ffn",
    )(x, o_h, o_mt, gt, *consts)


def _rope_tables(S):
    d = QK_ROPE
    half = d // 2
    inv = ROPE_THETA ** (-jnp.arange(0, d, 2, dtype=F32) / d)
    ang = jnp.arange(S, dtype=F32)[:, None] * inv[None, :]
    cos, sin = jnp.cos(ang), jnp.sin(ang)
    z = lambda n: jnp.zeros((S, n), F32)
    tail = QK_PAD - QK_HEAD
    cosk = jnp.concatenate([z(QK_NOPE), cos, cos, z(tail)], axis=1)
    sina = jnp.concatenate([z(QK_NOPE + half), sin, z(tail)], axis=1)
    sinb = jnp.concatenate([z(QK_NOPE), -sin, z(half + tail)], axis=1)
    return cos.T, sin.T, cosk, sina, sinb


def _prep_weights(w_in, lb_param, w_uq, w_ukv, w_branch, w_out, w_gate_up, w_down):
    W = HG_WIDTH
    r_q, H = w_uq.shape[1], w_uq.shape[2]
    r_kv = w_ukv.shape[1]
    w = w_in[0]
    D = w.shape[0]
    c0 = 5 * W
    w_act = jnp.concatenate([w[:, :2 * W], w[:, 4 * W:c0]], axis=1)
    w_dec = w[:, 2 * W:4 * W]
    w_qa = w[:, c0:c0 + r_q]
    w_kva = w[:, c0 + r_q:c0 + r_q + r_kv]
    c1 = c0 + r_q + r_kv
    w_kr = w[:, c1:c1 + QK_ROPE]
    w_gt = w[:, c1 + QK_ROPE:]
    zpad = lambda n: jnp.zeros((D, n), w.dtype)
    w_mla = jnp.concatenate([w_qa, w_kva, zpad(QK_NOPE), w_kr, zpad(QK_PAD - QK_HEAD)], axis=1)
    wq = jnp.transpose(w_uq[0], (1, 2, 0))
    wq_t = jnp.concatenate([wq, jnp.zeros((H, QK_PAD - QK_HEAD, r_q), wq.dtype)], axis=1)
    wq_t = wq_t.reshape(H * QK_PAD, r_q)
    wkv = jnp.transpose(w_ukv[0], (1, 0, 2))
    wk = jnp.concatenate([wkv[:, :, :QK_NOPE],
                          jnp.zeros((H, r_kv, QK_PAD - QK_NOPE), wkv.dtype)], axis=2)
    wk = jnp.transpose(wk, (1, 0, 2)).reshape(r_kv, H * QK_PAD)
    wv_t = jnp.transpose(wkv[:, :, QK_NOPE:], (0, 2, 1)).reshape(H * V_HEAD, r_kv)
    lb = jax.nn.softmax(lb_param.astype(F32), axis=0)[0]
    b16 = lambda a: a.astype(BF16)
    return dict(
        w_proj=b16(jnp.concatenate([w_act, w_dec, w_gt, w_mla], axis=1)), n_gt=w_gt.shape[1],
        wq_t=b16(wq_t), wk=b16(wk), wv_t=b16(wv_t),
        wb0=b16(w_branch[0, 0]), wb1=b16(w_branch[0, 1]), wo=b16(w_out[0]),
        wgu=b16(w_gate_up[0]), wd=b16(w_down[0]),
        lb_fw=lb[0:1], lb_bw=lb[1:2],
    )


SEQ_TILE = 512
BWD_TILE = 1024
ATTN_Q_TILE = 1024
ATTN_K_BLOCK = 8192
ONLINE_Q_TILE = 512
ONLINE_K_TILE = 1024


def _pick(n, pref):
    t = min(n, pref)
    while n % t:
        t //= 2
    return t


def _trunk(x, p, tabs, g_mix, g_onorm, g_qa, g_kva, g_ffn, g_final):
    B, S, D = x.shape
    ts = _pick(S, SEQ_TILE)
    act, fb, gt, o_fw, q_t, k, v_t, k_sq, q_sq = _front(x, p, g_mix, g_qa, g_kva, tabs, ts=ts)
    o_h = _hgrn_bwd(act, fb, p["lb_bw"], o_fw, g_onorm, ts=_pick(S, BWD_TILE))
    k_sq = jnp.transpose(k_sq, (0, 2, 1, 3))
    o_mt, den = _attention(q_t, k, v_t, k_sq, q_sq, tq=_pick(S, ATTN_Q_TILE),
                           tk=_pick(S, ATTN_K_BLOCK))
    o_mt = lax.cond(jnp.min(den) >= DENOM_FLOOR, lambda: o_mt,
                    lambda: _attention_online(q_t, k, v_t, tq=_pick(S, ONLINE_Q_TILE),
                                              tk=_pick(S // 2, ONLINE_K_TILE)))
    o_mt = o_mt.reshape(B, -1, S)

    return _mix_ffn(x, o_h, o_mt, gt, p["wb0"], p["wb1"], p["wo"], g_ffn, p["wgu"], p["wd"],
                    g_final[None, :], tm=ts)


def kernel(x_prompt, x_sample, g_mix, w_in, lb_param, g_onorm, g_qa, w_uq, g_kva, w_ukv,
           w_branch, w_out, g_ffn, w_gate_up, w_down, g_final):
    p = _prep_weights(w_in, lb_param, w_uq, w_ukv, w_branch, w_out, w_gate_up, w_down)
    tabs = _rope_tables(max(x_prompt.shape[1], x_sample.shape[1]))
    args = (p, tabs, g_mix, g_onorm, g_qa, g_kva, g_ffn, g_final)
    return (_trunk(x_prompt, *args), _trunk(x_sample, *args))
```
